```python
import math
import jax, jax.numpy as jnp
from jax import lax
import numpy as np

D_MODEL = 2048
BATCH = 4
SEQ = 2048
DEPTH = 1
DEC_BATCH = 128
DEC_SEQ = 4
PAST_LEN = 16384
PAGE_SIZE = 128

MIX_WIDTH = D_MODEL
POOL_WIDTH = MIX_WIDTH // 2
SSM_WIDTH = MIX_WIDTH - POOL_WIDTH
POOL_WINDOWS = (2, 4, 8, 16)
N_POOL_GROUPS = len(POOL_WINDOWS)
POOL_GROUP = POOL_WIDTH // N_POOL_GROUPS
POOL_BUF = max(POOL_WINDOWS) - 1
SSM_GROUP = 16
N_SSM_GROUPS = SSM_WIDTH // SSM_GROUP
SSM_STATE = 64
N_MEM = 256
N_XHEADS = 4
XHEAD_DIM = D_MODEL // N_XHEADS
D_FF = -(-8 * D_MODEL // (3 * 256)) * 256
EPS = 1e-6
DT_MIN = 1e-3
DT_MAX = 1e-1

kernel_name = "hymba_pool_s5_xattn_step"


def _normal(k, shape, scale):
    return scale * jax.random.normal(k, shape, jnp.float32)


def _rmsnorm(x, g):
    xf = x.astype(jnp.float32)
    r = lax.rsqrt(jnp.mean(xf * xf, axis=-1, keepdims=True) + EPS)
    return (xf * r * g.astype(jnp.float32)).astype(x.dtype)


def _pool_mixer(u, buf, pos0, w_pool, pool_scale):
    f32 = jnp.float32
    bsz, t, c = u.shape
    z = jnp.concatenate([buf.astype(f32), u.astype(f32)], axis=1)
    cs = jnp.concatenate([jnp.zeros((bsz, 1, c), f32), jnp.cumsum(z, axis=1)], axis=1)
    end = cs[:, POOL_BUF + 1:]
    pos = pos0 + jnp.arange(t)
    means = []
    for g, w in enumerate(POOL_WINDOWS):
        sl = slice(g * POOL_GROUP, (g + 1) * POOL_GROUP)
        start = cs[:, POOL_BUF + 1 - w:POOL_BUF + 1 - w + t, sl]
        cnt = jnp.minimum(pos + 1, w).astype(f32)[None, :, None]
        means.append((end[..., sl] - start) / cnt)
    pooled = (jnp.concatenate(means, axis=-1) - u.astype(f32)).reshape(bsz, t, N_POOL_GROUPS, POOL_GROUP)
    out = jnp.einsum('btgc,gcd->btgd', pooled, w_pool.astype(f32)).reshape(bsz, t, POOL_WIDTH)
    out = out * pool_scale.astype(f32)
    new_buf = z[:, -POOL_BUF:].astype(u.dtype)
    return out.astype(u.dtype), new_buf


def _ssm_combine(e1, e2):
    a1, b1 = e1
    a2, b2 = e2
    return a1 * a2, a2 * b1 + b2


def _s5_mixer(u, h_re, h_im, lam_re, lam_im, log_step, b_re, b_im, c_re, c_im, d_skip, w_glu, b_glu):
    f32 = jnp.float32
    bsz, t, _ = u.shape
    uf = u.astype(f32).reshape(bsz, t, N_SSM_GROUPS, SSM_GROUP)
    lam = lax.complex(lam_re.astype(f32), lam_im.astype(f32))
    delta = jnp.exp(log_step.astype(f32))[:, None]
    a_bar = jnp.exp(lam * delta)
    b_bar = ((a_bar - 1.0) / lam)[..., None] * lax.complex(b_re.astype(f32), b_im.astype(f32))
    bu = jnp.einsum('gpc,btgc->btgp', b_bar, uf.astype(jnp.complex64))
    h0 = lax.complex(h_re.astype(f32), h_im.astype(f32))
    bu = bu.at[:, 0].add(a_bar * h0)
    a = jnp.broadcast_to(a_bar, bu.shape)
    _, s = lax.associative_scan(_ssm_combine, (a, bu), axis=1)
    y = (jnp.einsum('gcp,btgp->btgc', c_re.astype(f32), jnp.real(s))
         - jnp.einsum('gcp,btgp->btgc', c_im.astype(f32), jnp.imag(s)))
    y = (y + d_skip.astype(f32).reshape(N_SSM_GROUPS, SSM_GROUP) * uf).reshape(bsz, t, SSM_WIDTH)
    g = jax.nn.gelu(y)
    out = g * jax.nn.sigmoid(g @ w_glu.astype(f32) + b_glu.astype(f32))
    h_last = s[:, -1]
    return out.astype(u.dtype), jnp.real(h_last).astype(h_re.dtype), jnp.imag(h_last).astype(h_im.dtype)


def _memory_kv(mem, g_mem, w_k, w_v):
    bsz, m, _ = mem.shape
    mn = _rmsnorm(mem, g_mem)
    k = (mn @ w_k).reshape(bsz, m, N_XHEADS, XHEAD_DIM)
    v = (mn @ w_v).reshape(bsz, m, N_XHEADS, XHEAD_DIM)
    return k, v


def _cross_attn(h, mem_k, mem_v, w_q, w_o):
    f32 = jnp.float32
    bsz, t, _ = h.shape
    q = (h @ w_q).reshape(bsz, t, N_XHEADS, XHEAD_DIM)
    sc = jnp.einsum('bthd,bmhd->bhtm', q.astype(f32), mem_k.astype(f32)) * (XHEAD_DIM ** -0.5)
    p = jax.nn.softmax(sc, axis=-1)
    o = jnp.einsum('bhtm,bmhd->bthd', p, mem_v.astype(f32)).astype(h.dtype).reshape(bsz, t, D_MODEL)
    return o @ w_o


def _layer(x, pool_buf, h_re, h_im, mem_k, mem_v, pos0, lw):
    h = _rmsnorm(x, lw['g_mix'])
    p = h @ lw['w_in']
    u_pool, u_ssm = p[..., :POOL_WIDTH], p[..., POOL_WIDTH:]
    pool_out, new_buf = _pool_mixer(u_pool, pool_buf, pos0, lw['w_pool'], lw['pool_scale'])
    ssm_out, new_re, new_im = _s5_mixer(u_ssm, h_re, h_im, lw['lam_re'], lw['lam_im'], lw['log_step'],
                                        lw['b_re'], lw['b_im'], lw['c_re'], lw['c_im'], lw['d'],
                                        lw['w_glu'], lw['b_glu'])
    x = x + jnp.concatenate([pool_out, ssm_out], axis=-1) @ lw['w_out']
    x = x + _cross_attn(_rmsnorm(x, lw['g_cross']), mem_k, mem_v, lw['w_q'], lw['w_o'])
    h = _rmsnorm(x, lw['g_ffn'])
    x = x + (jax.nn.silu(h @ lw['w_gate']) * (h @ lw['w_up'])) @ lw['w_down']
    return x, new_buf, new_re, new_im


def setup_inputs(seed: int = 0) -> dict:
    key = jax.random.key(seed)
    k = jax.random.split(key, 34)
    f32 = jnp.float32
    G, P = N_SSM_GROUPS, SSM_STATE
    lam_im = jnp.broadcast_to(jnp.pi * jnp.arange(P, dtype=f32), (DEPTH, G, P))
    return {
        'x_prompt': _normal(k[0], (BATCH, SEQ, D_MODEL), 1.0),
        'x_sample': _normal(k[1], (DEC_BATCH, DEC_SEQ, D_MODEL), 1.0),
        'mem_prompt': _normal(k[2], (BATCH, N_MEM, D_MODEL), 1.0),
        'state_pool_buf': _normal(k[3], (DEPTH, DEC_BATCH, POOL_BUF, POOL_WIDTH), 1.0),
        'state_ssm_re': _normal(k[4], (DEPTH, DEC_BATCH, G, P), 0.1),
        'state_ssm_im': _normal(k[5], (DEPTH, DEC_BATCH, G, P), 0.1),
        'cache_mem_k': _normal(k[6], (DEPTH, DEC_BATCH, N_MEM, N_XHEADS, XHEAD_DIM), 1.0),
        'cache_mem_v': _normal(k[7], (DEPTH, DEC_BATCH, N_MEM, N_XHEADS, XHEAD_DIM), 1.0),
        'g_mix': 1.0 + _normal(k[8], (DEPTH, D_MODEL), 0.02),
        'w_in': _normal(k[9], (DEPTH, D_MODEL, MIX_WIDTH), D_MODEL ** -0.5),
        'w_pool': _normal(k[10], (DEPTH, N_POOL_GROUPS, POOL_GROUP, POOL_GROUP), POOL_GROUP ** -0.5),
        'pool_scale': 1.0 + _normal(k[11], (DEPTH, POOL_WIDTH), 0.02),
        'ssm_lam_re': -0.5 + _normal(k[12], (DEPTH, G, P), 0.01),
        'ssm_lam_im': lam_im,
        'ssm_log_step': jax.random.uniform(k[13], (DEPTH, G), f32, math.log(DT_MIN), math.log(DT_MAX)),
        'ssm_b_re': _normal(k[14], (DEPTH, G, P, SSM_GROUP), (2 * SSM_GROUP) ** -0.5),
        'ssm_b_im': _normal(k[15], (DEPTH, G, P, SSM_GROUP), (2 * SSM_GROUP) ** -0.5),
        'ssm_c_re': _normal(k[16], (DEPTH, G, SSM_GROUP, P), P ** -0.5),
        'ssm_c_im': _normal(k[17], (DEPTH, G, SSM_GROUP, P), P ** -0.5),
        'ssm_d': _normal(k[18], (DEPTH, SSM_WIDTH), 1.0),
        'w_glu': _normal(k[19], (DEPTH, SSM_WIDTH, SSM_WIDTH), SSM_WIDTH ** -0.5),
        'b_glu': _normal(k[20], (DEPTH, SSM_WIDTH), 0.01),
        'w_out': _normal(k[21], (DEPTH, MIX_WIDTH, D_MODEL), MIX_WIDTH ** -0.5),
        'g_cross': 1.0 + _normal(k[22], (DEPTH, D_MODEL), 0.02),
        'g_mem': 1.0 + _normal(k[23], (DEPTH, D_MODEL), 0.02),
        'w_q': _normal(k[24], (DEPTH, D_MODEL, D_MODEL), D_MODEL ** -0.5),
        'w_k': _normal(k[25], (DEPTH, D_MODEL, D_MODEL), D_MODEL ** -0.5),
        'w_v': _normal(k[26], (DEPTH, D_MODEL, D_MODEL), D_MODEL ** -0.5),
        'w_o': _normal(k[27], (DEPTH, D_MODEL, D_MODEL), D_MODEL ** -0.5),
        'g_ffn': 1.0 + _normal(k[28], (DEPTH, D_MODEL), 0.02),
        'w_gate': _normal(k[29], (DEPTH, D_MODEL, D_FF), D_MODEL ** -0.5),
        'w_up': _normal(k[30], (DEPTH, D_MODEL, D_FF), D_MODEL ** -0.5),
        'w_down': _normal(k[31], (DEPTH, D_FF, D_MODEL), D_FF ** -0.5),
        'g_final': 1.0 + _normal(k[32], (D_MODEL,), 0.02),
    }


def reference(x_prompt, x_sample, mem_prompt, state_pool_buf, state_ssm_re, state_ssm_im,
              cache_mem_k, cache_mem_v, g_mix, w_in, w_pool, pool_scale, ssm_lam_re, ssm_lam_im,
              ssm_log_step, ssm_b_re, ssm_b_im, ssm_c_re, ssm_c_im, ssm_d, w_glu, b_glu, w_out,
              g_cross, g_mem, w_q, w_k, w_v, w_o, g_ffn, w_gate, w_up, w_down, g_final):
    bsz = x_prompt.shape[0]
    yp, ys = x_prompt, x_sample
    pb_p, re_p, im_p, mk_p, mv_p, pb_s, re_s, im_s = [], [], [], [], [], [], [], []
    for l in range(DEPTH):
        lw = {
            'g_mix': g_mix[l], 'w_in': w_in[l], 'w_pool': w_pool[l], 'pool_scale': pool_scale[l],
            'lam_re': ssm_lam_re[l], 'lam_im': ssm_lam_im[l], 'log_step': ssm_log_step[l],
            'b_re': ssm_b_re[l], 'b_im': ssm_b_im[l], 'c_re': ssm_c_re[l], 'c_im': ssm_c_im[l],
            'd': ssm_d[l], 'w_glu': w_glu[l], 'b_glu': b_glu[l], 'w_out': w_out[l],
            'g_cross': g_cross[l], 'w_q': w_q[l], 'w_o': w_o[l],
            'g_ffn': g_ffn[l], 'w_gate': w_gate[l], 'w_up': w_up[l], 'w_down': w_down[l],
        }
        mk, mv = _memory_kv(mem_prompt, g_mem[l], w_k[l], w_v[l])
        zero_buf = jnp.zeros((bsz, POOL_BUF, POOL_WIDTH), x_prompt.dtype)
        zero_h = jnp.zeros((bsz, N_SSM_GROUPS, SSM_STATE), state_ssm_re.dtype)
        yp, nb, nr, ni = _layer(yp, zero_buf, zero_h, zero_h, mk, mv, 0, lw)
        pb_p.append(nb); re_p.append(nr); im_p.append(ni); mk_p.append(mk); mv_p.append(mv)
        ys, nb, nr, ni = _layer(ys, state_pool_buf[l], state_ssm_re[l], state_ssm_im[l],
                                cache_mem_k[l], cache_mem_v[l], PAST_LEN, lw)
        pb_s.append(nb); re_s.append(nr); im_s.append(ni)
    y_prompt = _rmsnorm(yp, g_final)
    y_sample = _rmsnorm(ys, g_final)
    return (y_prompt, y_sample, jnp.stack(pb_p), jnp.stack(re_p), jnp.stack(im_p), jnp.stack(mk_p),
            jnp.stack(mv_p), jnp.stack(pb_s), jnp.stack(re_s), jnp.stack(im_s))
```

```python
import functools

import jax
import jax.numpy as jnp
from jax import lax
from jax.experimental import pallas as pl
from jax.experimental.pallas import tpu as pltpu

F32 = jnp.float32
BF16 = jnp.bfloat16

D_MODEL = 2048
POOL_WIDTH = 1024
SSM_WIDTH = 1024
POOL_WINDOWS = (2, 4, 8, 16)
POOL_GROUP = POOL_WIDTH // len(POOL_WINDOWS)
POOL_BUF = max(POOL_WINDOWS) - 1
SSM_GROUP = 16
N_SSM_GROUPS = SSM_WIDTH // SSM_GROUP
SSM_STATE = 64
N_MEM = 256
N_XHEADS = 4
XHEAD_DIM = D_MODEL // N_XHEADS
D_FF = 5632
EPS = 1e-6
PAST_LEN = 16384

LANE = 128
N_LANE_BLOCKS = SSM_WIDTH // LANE
GROUPS_PER_BLOCK = LANE // SSM_GROUP
STATE_BLOCK = GROUPS_PER_BLOCK * SSM_STATE
MIB = 1024 * 1024
HI = lax.Precision.HIGHEST


def _params(semantics, vmem_mib):
    return pltpu.CompilerParams(dimension_semantics=semantics, vmem_limit_bytes=vmem_mib * MIB)


def _const_spec(shape):
    return pl.BlockSpec(shape, lambda *_: (0,) * len(shape), pipeline_mode=pl.Buffered(1))


def _rms(x, g):
    r = lax.rsqrt(jnp.mean(x * x, axis=-1, keepdims=True) + EPS)
    return x * r * g


def _dot(a, b):
    return jnp.dot(a, b, preferred_element_type=F32)


def _in_proj_kernel(x_ref, g_ref, w_ref, up_ref, us_ref):
    h = _rms(x_ref[...], g_ref[...]).astype(BF16)
    u = _dot(h, w_ref[...])
    up_ref[...] = u[:, :POOL_WIDTH]
    for j in range(N_LANE_BLOCKS):
        us_ref[j] = u[:, POOL_WIDTH + j * LANE:POOL_WIDTH + (j + 1) * LANE]


def _in_proj(x, g, w, tm):
    rows = x.shape[0]
    return pl.pallas_call(
        _in_proj_kernel,
        grid=(rows // tm,),
        in_specs=[pl.BlockSpec((tm, D_MODEL), lambda i: (i, 0)),
                  _const_spec((1, D_MODEL)),
                  _const_spec((D_MODEL, D_MODEL))],
        out_specs=[pl.BlockSpec((tm, POOL_WIDTH), lambda i: (i, 0)),
                   pl.BlockSpec((N_LANE_BLOCKS, tm, LANE), lambda i: (0, i, 0))],
        out_shape=[jax.ShapeDtypeStruct((rows, POOL_WIDTH), F32),
                   jax.ShapeDtypeStruct((N_LANE_BLOCKS, rows, LANE), F32)],
        compiler_params=_params(("arbitrary",), 40),
        name="in_proj",
    )(x, g, w)


def _pool_kernel(*refs, L, R, tiles_per_seq, pos0, has_buf):
    W = POOL_WIDTH
    if has_buf:
        u_ref, buf_ref, wp_ref, sc_ref, o_ref, nb_ref = refs
        it = 0
    else:
        u_ref, wp_ref, sc_ref, o_ref, carry_ref = refs
        assert L == 8
        it = lax.rem(pl.program_id(0), tiles_per_seq)

        @pl.when(it == 0)
        def _():
            carry_ref[...] = jnp.zeros_like(carry_ref)

    row = lax.broadcasted_iota(jnp.int32, (R, 1), 0)
    hist = 16

    for g, w in enumerate(POOL_WINDOWS):
        lo = g * POOL_GROUP
        cache = {}

        def slab(idx, lo=lo, cache=cache):
            if idx in cache:
                return cache[idx]
            if idx >= hist:
                k = idx - hist
                v = u_ref[:, k * W + lo:k * W + lo + POOL_GROUP]
            elif has_buf:
                k = idx - 1
                v = buf_ref[:, k * W + lo:k * W + lo + POOL_GROUP]
            else:
                k, back = (idx - 8, 1) if idx >= 8 else (idx, 2)
                rolled = pltpu.roll(slab(hist + k), back, 0)
                last = carry_ref[7:8, k * W + lo:k * W + lo + POOL_GROUP]
                if back == 1:
                    v = jnp.where(row == 0, last, rolled)
                else:
                    prev = carry_ref[6:7, k * W + lo:k * W + lo + POOL_GROUP]
                    v = jnp.where(row == 0, prev, jnp.where(row == 1, last, rolled))
            cache[idx] = v
            return v

        pooled = []
        for t in range(L):
            acc = slab(hist + t)
            for back in range(1, w):
                acc = acc + slab(hist + t - back)
            if pos0 + 1 >= w:
                cnt = float(w)
            else:
                pos = pos0 + (it * R + row) * L + t
                cnt = jnp.minimum(pos + 1, w).astype(F32)
            pooled.append((acc / cnt - slab(hist + t)).astype(BF16))
        out = _dot(jnp.concatenate(pooled, axis=0), wp_ref[g]) * sc_ref[:, lo:lo + POOL_GROUP]
        for t in range(L):
            o_ref[:, t * W + lo:t * W + lo + POOL_GROUP] = out[t * R:(t + 1) * R]

    if has_buf:
        keep = POOL_BUF - L
        nb_ref[:, :keep * W] = buf_ref[:, L * W:]
        nb_ref[:, keep * W:] = u_ref[...]
    else:
        carry_ref[...] = u_ref[R - 8:R, :]


def _pool_mix(u_chunks, buf_chunks, w_pool, scale, *, L, R, rows_per_seq, pos0):
    nc = u_chunks.shape[0]
    has_buf = buf_chunks is not None
    tiles_per_seq = max(rows_per_seq // R, 1)
    kern = functools.partial(_pool_kernel, L=L, R=R, tiles_per_seq=tiles_per_seq, pos0=pos0,
                             has_buf=has_buf)
    row_spec = pl.BlockSpec((R, L * POOL_WIDTH), lambda i: (i, 0))
    w_specs = [_const_spec((len(POOL_WINDOWS), POOL_GROUP, POOL_GROUP)), _const_spec((1, POOL_WIDTH))]
    if has_buf:
        buf_spec = pl.BlockSpec((R, POOL_BUF * POOL_WIDTH), lambda i: (i, 0))
        return pl.pallas_call(
            kern, grid=(nc // R,),
            in_specs=[row_spec, buf_spec] + w_specs,
            out_specs=[row_spec, buf_spec],
            out_shape=[jax.ShapeDtypeStruct(u_chunks.shape, F32),
                       jax.ShapeDtypeStruct(buf_chunks.shape, F32)],
            compiler_params=_params(("arbitrary",), 48),
            name="pool_mix_buf",
        )(u_chunks, buf_chunks, w_pool, scale)
    return pl.pallas_call(
        kern, grid=(nc // R,),
        in_specs=[row_spec] + w_specs,
        out_specs=row_spec,
        out_shape=jax.ShapeDtypeStruct(u_chunks.shape, F32),
        scratch_shapes=[pltpu.VMEM((8, L * POOL_WIDTH), F32)],
        compiler_params=_params(("arbitrary",), 48),
        name="pool_mix",
    )(u_chunks, w_pool, scale)


def _cmul(ar, ai, br, bi):
    return ar * br - ai * bi, ar * bi + ai * br


def _cexp(lam_re, lam_im, log_step):
    delta = jnp.exp(log_step)
    mag = jnp.exp(lam_re * delta)
    ang = lam_im * delta
    return mag * jnp.cos(ang), mag * jnp.sin(ang)


def _ssm_prep_kernel(row_ref, col_ref, flat_ref, t_ref, w_ref, v_ref, al_ref, *, L):
    lr, li, ls, br, bi = (row_ref[i] for i in range(5))
    ar, ai = _cexp(lr, li, ls)
    den = lr * lr + li * li
    xr = ar - 1.0
    fr = (xr * lr + ai * li) / den
    fi = (ai * lr - xr * li) / den
    zs = [_cmul(fr, fi, br, bi)]
    for _ in range(1, L):
        zs.append(_cmul(ar, ai, *zs[-1]))

    lrc, lic, lsc, cr, ci = (col_ref[i] for i in range(5))
    acr, aci = _cexp(lrc, lic, lsc)
    xs = []
    cur = (cr, ci)
    for _ in range(L):
        cur = _cmul(acr, aci, *cur)
        xs.append(cur)

    def iota(shape, dim):
        return lax.broadcasted_iota(jnp.int32, shape, dim)

    same_tt = (iota((LANE, LANE), 0) >> 4) == (iota((LANE, LANE), 1) >> 4)
    c_stack = jnp.concatenate([cr, -ci], axis=0)
    zero_tile = jnp.zeros((LANE, LANE), BF16)
    lag = []
    for d in range(L):
        z_stack = jnp.concatenate([zs[d][0], zs[d][1]], axis=1)
        tile = jnp.dot(z_stack, c_stack, precision=HI, preferred_element_type=F32)
        lag.append(jnp.where(same_tt, tile, 0.0).astype(BF16))
    for k in range(L):
        for t in range(L):
            t_ref[0, k * LANE:(k + 1) * LANE, t * LANE:(t + 1) * LANE] = lag[t - k] if t >= k else zero_tile

    rep = ((iota((SSM_STATE, STATE_BLOCK), 1) & (SSM_STATE - 1)) == iota((SSM_STATE, STATE_BLOCK), 0)).astype(F32)
    same_w = (iota((LANE, STATE_BLOCK), 0) >> 4) == (iota((LANE, STATE_BLOCK), 1) >> 6)
    for k in range(L):
        zr, zi = zs[L - 1 - k]
        wr = jnp.where(same_w, jnp.dot(zr, rep, precision=HI, preferred_element_type=F32), 0.0)
        wi = jnp.where(same_w, jnp.dot(zi, rep, precision=HI, preferred_element_type=F32), 0.0)
        w_ref[0, k * LANE:(k + 1) * LANE, :] = jnp.concatenate([wr, wi], axis=1).astype(BF16)

    rep_t = ((iota((STATE_BLOCK, SSM_STATE), 0) & (SSM_STATE - 1)) == iota((STATE_BLOCK, SSM_STATE), 1)).astype(F32)
    same_v = (iota((STATE_BLOCK, LANE), 0) >> 6) == (iota((STATE_BLOCK, LANE), 1) >> 4)
    for t in range(L):
        xr_t, xi_t = xs[t]
        vr = jnp.where(same_v, jnp.dot(rep_t, xr_t, precision=HI, preferred_element_type=F32), 0.0)
        vi = jnp.where(same_v, jnp.dot(rep_t, xi_t, precision=HI, preferred_element_type=F32), 0.0)
        v_ref[0, :, t * LANE:(t + 1) * LANE] = jnp.concatenate([vr, -vi], axis=0).astype(BF16)

    flr, fli, fls = (flat_ref[i, 0] for i in range(3))
    far, fai = _cexp(flr, fli, fls)
    pr, pi = far, fai
    for _ in range(L - 1):
        pr, pi = _cmul(far, fai, pr, pi)
    al_ref[0] = jnp.concatenate([pr, pi], axis=1)


def _ssm_prep(row5, col5, flat3, L):
    lk = L * LANE
    return pl.pallas_call(
        functools.partial(_ssm_prep_kernel, L=L),
        grid=(N_LANE_BLOCKS,),
        in_specs=[pl.BlockSpec((5, LANE, SSM_STATE), lambda j: (0, j, 0)),
                  pl.BlockSpec((5, SSM_STATE, LANE), lambda j: (0, 0, j)),
                  pl.BlockSpec((3, 1, 1, STATE_BLOCK), lambda j: (0, j, 0, 0))],
        out_specs=[pl.BlockSpec((1, lk, lk), lambda j: (j, 0, 0)),
                   pl.BlockSpec((1, lk, 2 * STATE_BLOCK), lambda j: (j, 0, 0)),
                   pl.BlockSpec((1, 2 * STATE_BLOCK, lk), lambda j: (j, 0, 0)),
                   pl.BlockSpec((1, 1, 2 * STATE_BLOCK), lambda j: (j, 0, 0))],
        out_shape=[jax.ShapeDtypeStruct((N_LANE_BLOCKS, lk, lk), BF16),
                   jax.ShapeDtypeStruct((N_LANE_BLOCKS, lk, 2 * STATE_BLOCK), BF16),
                   jax.ShapeDtypeStruct((N_LANE_BLOCKS, 2 * STATE_BLOCK, lk), BF16),
                   jax.ShapeDtypeStruct((N_LANE_BLOCKS, 1, 2 * STATE_BLOCK), F32)],
        compiler_params=_params(("arbitrary",), 40),
        name=f"ssm_prep_L{L}",
    )(row5, col5, flat3)


def _ssm_kernel(u_ref, t_ref, w_ref, v_ref, al_ref, d_ref, h0_ref, g_ref, st_ref, *scratch, B, Rn, seq):
    M = B * Rn
    lk = u_ref.shape[-1]
    sb = STATE_BLOCK
    u = u_ref[0].reshape(M, lk)
    ub = u.astype(BF16)
    e = _dot(ub, w_ref[0])
    ar = al_ref[0][:, :sb]
    ai = al_ref[0][:, sb:]
    if seq:
        e_scr, s_scr, c_scr = scratch

        @pl.when(pl.program_id(1) == 0)
        def _():
            c_scr[...] = h0_ref[0]

        e_scr[...] = e

        def body(r, carry):
            new = []
            for b in range(B):
                s = carry[b]
                s_scr[pl.ds(b * Rn + r, 1), :] = s
                eb = e_scr[pl.ds(b * Rn + r, 1), :]
                sr, si = s[:, :sb], s[:, sb:]
                nr = ar * sr - ai * si + eb[:, :sb]
                ni = ar * si + ai * sr + eb[:, sb:]
                new.append(jnp.concatenate([nr, ni], axis=1))
            return tuple(new)

        fin = lax.fori_loop(0, Rn, body, tuple(c_scr[b:b + 1, :] for b in range(B)))
        for b in range(B):
            c_scr[b:b + 1, :] = fin[b]
        st_ref[0] = c_scr[...]
        s_start = s_scr[...]
    else:
        s_start = h0_ref[0]
        sr, si = s_start[:, :sb], s_start[:, sb:]
        st_ref[0] = jnp.concatenate([ar * sr - ai * si + e[:, :sb], ar * si + ai * sr + e[:, sb:]], axis=1)
    y = _dot(ub, t_ref[0]) + _dot(s_start.astype(BF16), v_ref[0]) + d_ref[0] * u
    g_ref[0] = jax.nn.gelu(y).reshape(B, Rn, lk)


def _ssm_mix(u_view, ops, d_tiles, h0, *, Rn, seq):
    t_op, w_op, v_op, a_l = ops
    nj, B, ncb, lk = u_view.shape
    nseq = h0.shape[1]
    tiles = ncb // Rn
    M = B * Rn
    sb2 = 2 * STATE_BLOCK
    u_spec = pl.BlockSpec((1, B, Rn, lk), lambda j, i: (j, 0, i, 0))
    scratch = [pltpu.VMEM((M, sb2), F32), pltpu.VMEM((M, sb2), F32), pltpu.VMEM((B, sb2), F32)] if seq else []
    return pl.pallas_call(
        functools.partial(_ssm_kernel, B=B, Rn=Rn, seq=seq),
        grid=(nj, tiles),
        in_specs=[u_spec,
                  pl.BlockSpec((1, lk, lk), lambda j, i: (j, 0, 0)),
                  pl.BlockSpec((1, lk, sb2), lambda j, i: (j, 0, 0)),
                  pl.BlockSpec((1, sb2, lk), lambda j, i: (j, 0, 0)),
                  pl.BlockSpec((1, 1, sb2), lambda j, i: (j, 0, 0)),
                  pl.BlockSpec((1, 1, lk), lambda j, i: (j, 0, 0)),
                  pl.BlockSpec((1, nseq, sb2), lambda j, i: (j, 0, 0))],
        out_specs=[u_spec, pl.BlockSpec((1, nseq, sb2), lambda j, i: (j, 0, 0))],
        out_shape=[jax.ShapeDtypeStruct(u_view.shape, F32),
                   jax.ShapeDtypeStruct((nj, nseq, sb2), F32)],
        scratch_shapes=scratch,
        compiler_params=_params(("arbitrary", "arbitrary"), 48),
        name="ssm_mix_seq" if seq else "ssm_mix_rows",
    )(u_view, t_op, w_op, v_op, a_l, d_tiles, h0)


def _mix_out_kernel(g_ref, po_ref, x_ref, wglu_ref, bglu_ref, wout_ref, gc_ref, wq_ref, x1_ref, q_ref):
    g = jnp.concatenate([g_ref[j] for j in range(N_LANE_BLOCKS)], axis=1)
    gate = jax.nn.sigmoid(_dot(g.astype(BF16), wglu_ref[...]) + bglu_ref[...])
    ssm_out = (g * gate).astype(BF16)
    mix = _dot(po_ref[...].astype(BF16), wout_ref[:POOL_WIDTH, :]) + _dot(ssm_out, wout_ref[POOL_WIDTH:, :])
    x1 = x_ref[...] + mix
    x1_ref[...] = x1
    q_ref[...] = _dot(_rms(x1, gc_ref[...]).astype(BF16), wq_ref[...]).astype(BF16)


def _mix_out(g_rows, pool_out, x, w_glu, b_glu, w_out, g_cross, w_q, tm):
    rows = x.shape[0]
    return pl.pallas_call(
        _mix_out_kernel,
        grid=(rows // tm,),
        in_specs=[pl.BlockSpec((N_LANE_BLOCKS, tm, LANE), lambda i: (0, i, 0)),
                  pl.BlockSpec((tm, POOL_WIDTH), lambda i: (i, 0)),
                  pl.BlockSpec((tm, D_MODEL), lambda i: (i, 0)),
                  _const_spec((SSM_WIDTH, SSM_WIDTH)), _const_spec((1, SSM_WIDTH)),
                  _const_spec((D_MODEL, D_MODEL)), _const_spec((1, D_MODEL)),
                  _const_spec((D_MODEL, D_MODEL))],
        out_specs=[pl.BlockSpec((tm, D_MODEL), lambda i: (i, 0)),
                   pl.BlockSpec((tm, D_MODEL), lambda i: (i, 0))],
        out_shape=[jax.ShapeDtypeStruct((rows, D_MODEL), F32),
                   jax.ShapeDtypeStruct((rows, D_MODEL), BF16)],
        compiler_params=_params(("arbitrary",), 52),
        name="mix_out",
    )(g_rows, pool_out, x, w_glu, b_glu, w_out, g_cross, w_q)


def _mem_kv_kernel(m_ref, g_ref, w_ref, kv_ref, kvb_ref):
    kv = _dot(_rms(m_ref[...], g_ref[...]).astype(BF16), w_ref[0])
    kv_ref[0] = kv
    kvb_ref[0] = kv.astype(BF16)


def _mem_kv(mem, g_mem, w_kv, tm):
    rows = mem.shape[0]
    return pl.pallas_call(
        _mem_kv_kernel,
        grid=(2, rows // tm),
        in_specs=[pl.BlockSpec((tm, D_MODEL), lambda s, i: (i, 0)),
                  _const_spec((1, D_MODEL)),
                  pl.BlockSpec((1, D_MODEL, D_MODEL), lambda s, i: (s, 0, 0))],
        out_specs=[pl.BlockSpec((1, tm, D_MODEL), lambda s, i: (s, i, 0)),
                   pl.BlockSpec((1, tm, D_MODEL), lambda s, i: (s, i, 0))],
        out_shape=[jax.ShapeDtypeStruct((2, rows, D_MODEL), F32),
                   jax.ShapeDtypeStruct((2, rows, D_MODEL), BF16)],
        compiler_params=_params(("arbitrary", "arbitrary"), 48),
        name="mem_kv",
    )(mem, g_mem, w_kv)


_NT = (((1,), (1,)), ((), ()))


def _softmax_rows(s):
    e = jnp.exp(s - jnp.max(s, axis=-1, keepdims=True))
    return e / jnp.sum(e, axis=-1, keepdims=True)


def _attn_prompt_kernel(q_ref, kv_ref, o_ref):
    for h in range(N_XHEADS):
        sl = slice(h * XHEAD_DIM, (h + 1) * XHEAD_DIM)
        s = lax.dot_general(q_ref[:, sl], kv_ref[0, :, sl], _NT, preferred_element_type=F32)
        p = _softmax_rows(s * (XHEAD_DIM ** -0.5))
        o_ref[:, sl] = _dot(p.astype(BF16), kv_ref[1, :, sl]).astype(BF16)


def _attn_prompt(q, kv_bf, tq, rows_per_seq):
    rows = q.shape[0]
    tiles = rows_per_seq // tq
    return pl.pallas_call(
        _attn_prompt_kernel,
        grid=(rows // tq,),
        in_specs=[pl.BlockSpec((tq, D_MODEL), lambda i: (i, 0)),
                  pl.BlockSpec((2, N_MEM, D_MODEL), lambda i: (0, i // tiles, 0))],
        out_specs=pl.BlockSpec((tq, D_MODEL), lambda i: (i, 0)),
        out_shape=jax.ShapeDtypeStruct((rows, D_MODEL), BF16),
        compiler_params=_params(("arbitrary",), 40),
        name="attn_prompt",
    )(q, kv_bf)


def _attn_cache_kernel(q_ref, k_ref, v_ref, o_ref, *, bb):
    lane = lax.broadcasted_iota(jnp.int32, (8, D_MODEL), 1)
    for b in range(bb):
        q8 = q_ref[b]
        q_rows = jnp.concatenate(
            [jnp.where((lane >= h * XHEAD_DIM) & (lane < (h + 1) * XHEAD_DIM), q8, 0.0)
             for h in range(N_XHEADS)], axis=0).astype(BF16)
        s = lax.dot_general(q_rows, k_ref[b].astype(BF16), _NT, preferred_element_type=F32)
        p = _softmax_rows(s * (XHEAD_DIM ** -0.5))
        o_all = _dot(p.astype(BF16), v_ref[b].astype(BF16))
        o_ref[b] = jnp.concatenate(
            [o_all[8 * h:8 * (h + 1), h * XHEAD_DIM:(h + 1) * XHEAD_DIM] for h in range(N_XHEADS)], axis=1)


def _attn_cache(q8, k, v, bb):
    nseq = q8.shape[0]
    return pl.pallas_call(
        functools.partial(_attn_cache_kernel, bb=bb),
        grid=(nseq // bb,),
        in_specs=[pl.BlockSpec((bb, 8, D_MODEL), lambda i: (i, 0, 0)),
                  pl.BlockSpec((bb, N_MEM, D_MODEL), lambda i: (i, 0, 0)),
                  pl.BlockSpec((bb, N_MEM, D_MODEL), lambda i: (i, 0, 0))],
        out_specs=pl.BlockSpec((bb, 8, D_MODEL), lambda i: (i, 0, 0)),
        out_shape=jax.ShapeDtypeStruct((nseq, 8, D_MODEL), F32),
        compiler_params=_params(("arbitrary",), 48),
        name="attn_cache",
    )(q8, k, v)


def _oproj_kernel(o_ref, x1_ref, wo_ref, x2_ref):
    x2_ref[...] = x1_ref[...] + _dot(o_ref[...], wo_ref[...])


def _oproj(o, x1, w_o, tm):
    rows = x1.shape[0]
    return pl.pallas_call(
        _oproj_kernel,
        grid=(rows // tm,),
        in_specs=[pl.BlockSpec((tm, D_MODEL), lambda i: (i, 0)),
                  pl.BlockSpec((tm, D_MODEL), lambda i: (i, 0)),
                  _const_spec((D_MODEL, D_MODEL))],
        out_specs=pl.BlockSpec((tm, D_MODEL), lambda i: (i, 0)),
        out_shape=jax.ShapeDtypeStruct((rows, D_MODEL), F32),
        compiler_params=_params(("arbitrary",), 40),
        name="oproj",
    )(o, x1, w_o)


def _ffn_kernel(x_ref, gf_ref, wg_ref, wu_ref, wd_ref, gl_ref, y_ref, h_scr, acc_scr):
    f = pl.program_id(1)

    @pl.when(f == 0)
    def _():
        h_scr[...] = _rms(x_ref[...], gf_ref[...]).astype(BF16)
        acc_scr[...] = jnp.zeros_like(acc_scr)

    h = h_scr[...]
    z = (jax.nn.silu(_dot(h, wg_ref[...])) * _dot(h, wu_ref[...])).astype(BF16)
    acc_scr[...] += _dot(z, wd_ref[...])

    @pl.when(f == pl.num_programs(1) - 1)
    def _():
        y_ref[...] = _rms(x_ref[...] + acc_scr[...], gl_ref[...])


def _ffn(x2, g_ffn, w_gate, w_up, w_down, g_final, tm, tf):
    rows = x2.shape[0]
    return pl.pallas_call(
        _ffn_kernel,
        grid=(rows // tm, D_FF // tf),
        in_specs=[pl.BlockSpec((tm, D_MODEL), lambda i, f: (i, 0)),
                  _const_spec((1, D_MODEL)),
                  pl.BlockSpec((D_MODEL, tf), lambda i, f: (0, f)),
                  pl.BlockSpec((D_MODEL, tf), lambda i, f: (0, f)),
                  pl.BlockSpec((tf, D_MODEL), lambda i, f: (f, 0)),
                  _const_spec((1, D_MODEL))],
        out_specs=pl.BlockSpec((tm, D_MODEL), lambda i, f: (i, 0)),
        out_shape=jax.ShapeDtypeStruct((rows, D_MODEL), F32),
        scratch_shapes=[pltpu.VMEM((tm, D_MODEL), BF16), pltpu.VMEM((tm, D_MODEL), F32)],
        compiler_params=_params(("arbitrary", "arbitrary"), 48),
        name="ffn",
    )(x2, g_ffn, w_gate, w_up, w_down, g_final)


def _ssm_param_layouts(lam_re, lam_im, log_step, b_re, b_im, c_re, c_im):
    G, P = lam_re.shape
    ls = jnp.broadcast_to(log_step[:, None], (G, P))
    rep = lambda a: jnp.repeat(a, SSM_GROUP, axis=0)
    b_rows = lambda b: jnp.swapaxes(b, 1, 2).reshape(G * SSM_GROUP, P)
    row5 = jnp.stack([rep(lam_re), rep(lam_im), rep(ls), b_rows(b_re), b_rows(b_im)])
    col5 = jnp.stack([rep(lam_re).T, rep(lam_im).T, rep(ls).T,
                      c_re.reshape(G * SSM_GROUP, P).T, c_im.reshape(G * SSM_GROUP, P).T])
    flat = lambda a: a.reshape(N_LANE_BLOCKS, 1, STATE_BLOCK)
    flat3 = jnp.stack([flat(lam_re), flat(lam_im), flat(ls)])
    return row5, col5, flat3


def _states_to_blocks(h_re, h_im):
    S = h_re.shape[0]
    blk = lambda h: h.reshape(S, N_LANE_BLOCKS, STATE_BLOCK).transpose(1, 0, 2)
    return jnp.concatenate([blk(h_re), blk(h_im)], axis=-1)


def _blocks_to_states(st):
    S = st.shape[1]
    unblk = lambda a: a.transpose(1, 0, 2).reshape(1, S, N_SSM_GROUPS, SSM_STATE)
    return unblk(st[:, :, :STATE_BLOCK]), unblk(st[:, :, STATE_BLOCK:])


def kernel(x_prompt, x_sample, mem_prompt, state_pool_buf, state_ssm_re, state_ssm_im, cache_mem_k, cache_mem_v, g_mix, w_in, w_pool, pool_scale, ssm_lam_re, ssm_lam_im, ssm_log_step, ssm_b_re, ssm_b_im, ssm_c_re, ssm_c_im, ssm_d, w_glu, b_glu, w_out, g_cross, g_mem, w_q, w_k, w_v, w_o, g_ffn, w_gate, w_up, w_down, g_final):
    assert g_mix.shape[0] == 1, "single-layer step"
    B, T, _ = x_prompt.shape
    S, Ts, _ = x_sample.shape
    Lp, Ls = 8, Ts
    bf = lambda w: w.astype(BF16)
    vec = lambda v: v.reshape(1, -1)

    w_in_b, w_pool_b, w_glu_b, w_out_b = bf(w_in[0]), bf(w_pool[0]), bf(w_glu[0]), bf(w_out[0])
    w_q_b, w_o_b = bf(w_q[0]), bf(w_o[0])
    w_kv_b = bf(jnp.stack([w_k[0], w_v[0]]))
    w_gate_b, w_up_b, w_down_b = bf(w_gate[0]), bf(w_up[0]), bf(w_down[0])

    row5, col5, flat3 = _ssm_param_layouts(ssm_lam_re[0], ssm_lam_im[0], ssm_log_step[0],
                                           ssm_b_re[0], ssm_b_im[0], ssm_c_re[0], ssm_c_im[0])
    d_blocks = ssm_d[0].reshape(N_LANE_BLOCKS, 1, LANE)

    def layer(x, nseq, t_len, L, pool_buf, h0_blocks, attend, tm):
        rows = nseq * t_len
        nc = rows // L
        u_pool, u_ssm = _in_proj(x, vec(g_mix[0]), w_in_b, tm)

        u_chunks = u_pool.reshape(nc, L * POOL_WIDTH)
        if pool_buf is None:
            pool_out = _pool_mix(u_chunks, None, w_pool_b, vec(pool_scale[0]), L=L, R=128,
                                 rows_per_seq=t_len // L, pos0=0)
            new_buf = u_pool.reshape(nseq, t_len, POOL_WIDTH)[:, t_len - POOL_BUF:]
        else:
            pool_out, new_buf = _pool_mix(u_chunks, pool_buf.reshape(nseq, POOL_BUF * POOL_WIDTH), w_pool_b,
                                          vec(pool_scale[0]), L=L, R=nc, rows_per_seq=1, pos0=PAST_LEN)
            new_buf = new_buf.reshape(nseq, POOL_BUF, POOL_WIDTH)

        ops = _ssm_prep(row5, col5, flat3, L)
        d_tiles = jnp.tile(d_blocks, (1, 1, L))
        if t_len > L:
            u_view = u_ssm.reshape(N_LANE_BLOCKS, nseq, t_len // L, L * LANE)
            g_act, st = _ssm_mix(u_view, ops, d_tiles, h0_blocks, Rn=64, seq=True)
        else:
            u_view = u_ssm.reshape(N_LANE_BLOCKS, 1, nseq, L * LANE)
            g_act, st = _ssm_mix(u_view, ops, d_tiles, h0_blocks, Rn=nseq, seq=False)
        new_re, new_im = _blocks_to_states(st)

        x1, q = _mix_out(g_act.reshape(N_LANE_BLOCKS, rows, LANE), pool_out.reshape(rows, POOL_WIDTH), x,
                         w_glu_b, vec(b_glu[0]), w_out_b, vec(g_cross[0]), w_q_b, 256)
        o = attend(q)
        x2 = _oproj(o, x1, w_o_b, tm)
        y = _ffn(x2, vec(g_ffn[0]), w_gate_b, w_up_b, w_down_b, vec(g_final), tm, 512)
        return y, new_buf, new_re, new_im

    kv, kv_b = _mem_kv(mem_prompt.reshape(B * N_MEM, D_MODEL), vec(g_mem[0]), w_kv_b, 512)
    h0_p = jnp.zeros((N_LANE_BLOCKS, B, 2 * STATE_BLOCK), F32)
    yp, pb_p, re_p, im_p = layer(x_prompt.reshape(B * T, D_MODEL), B, T, Lp, None, h0_p,
                                 lambda q: _attn_prompt(q, kv_b, 512, T), 512)

    def attend_cache(q):
        q8 = jnp.pad(q.reshape(S, Ts, D_MODEL).astype(F32), ((0, 0), (0, 8 - Ts), (0, 0)))
        o8 = _attn_cache(q8, cache_mem_k[0].reshape(S, N_MEM, D_MODEL),
                         cache_mem_v[0].reshape(S, N_MEM, D_MODEL), 2)
        return o8[:, :Ts].reshape(S * Ts, D_MODEL).astype(BF16)

    h0_s = _states_to_blocks(state_ssm_re[0], state_ssm_im[0])
    ys, pb_s, re_s, im_s = layer(x_sample.reshape(S * Ts, D_MODEL), S, Ts, Ls, state_pool_buf[0], h0_s,
                                 attend_cache, 512)

    mk = kv[0].reshape(1, B, N_MEM, N_XHEADS, XHEAD_DIM)
    mv = kv[1].reshape(1, B, N_MEM, N_XHEADS, XHEAD_DIM)
    return (yp.reshape(B, T, D_MODEL), ys.reshape(S, Ts, D_MODEL), pb_p[None], re_p, im_p, mk, mv,
            pb_s[None], re_s, im_s)
```

```python
import functools

import jax
import jax.numpy as jnp
from jax import lax
from jax.experimental import pallas as pl
from jax.experimental.pallas import tpu as pltpu

F32 = jnp.float32
BF16 = jnp.bfloat16

D_MODEL = 2048
POOL_WIDTH = 1024
SSM_WIDTH = 1024
POOL_WINDOWS = (2, 4, 8, 16)
POOL_GROUP = POOL_WIDTH // len(POOL_WINDOWS)
POOL_BUF = max(POOL_WINDOWS) - 1
SSM_GROUP = 16
N_SSM_GROUPS = SSM_WIDTH // SSM_GROUP
SSM_STATE = 64
N_MEM = 256
N_XHEADS = 4
XHEAD_DIM = D_MODEL // N_XHEADS
D_FF = 5632
EPS = 1e-6
PAST_LEN = 16384

LANE = 128
N_LANE_BLOCKS = SSM_WIDTH // LANE
GROUPS_PER_BLOCK = LANE // SSM_GROUP
STATE_BLOCK = GROUPS_PER_BLOCK * SSM_STATE
MIB = 1024 * 1024
HI = lax.Precision.HIGHEST


def _params(semantics, vmem_mib):
    return pltpu.CompilerParams(dimension_semantics=semantics, vmem_limit_bytes=vmem_mib * MIB)


def _const_spec(shape):
    return pl.BlockSpec(shape, lambda *_: (0,) * len(shape), pipeline_mode=pl.Buffered(1))


def _rms(x, g):
    r = lax.rsqrt(jnp.mean(x * x, axis=-1, keepdims=True) + EPS)
    return x * r * g


def _dot(a, b):
    return jnp.dot(a, b, preferred_element_type=F32)


def _in_proj_kernel(x_ref, g_ref, w_ref, up_ref, us_ref):
    h = _rms(x_ref[...], g_ref[...]).astype(BF16)
    u = _dot(h, w_ref[...])
    up_ref[...] = u[:, :POOL_WIDTH]
    for j in range(N_LANE_BLOCKS):
        us_ref[j] = u[:, POOL_WIDTH + j * LANE:POOL_WIDTH + (j + 1) * LANE]


def _in_proj(x, g, w, tm):
    rows = x.shape[0]
    return pl.pallas_call(
        _in_proj_kernel,
        grid=(rows // tm,),
        in_specs=[pl.BlockSpec((tm, D_MODEL), lambda i: (i, 0)),
                  _const_spec((1, D_MODEL)),
                  _const_spec((D_MODEL, D_MODEL))],
        out_specs=[pl.BlockSpec((tm, POOL_WIDTH), lambda i: (i, 0)),
                   pl.BlockSpec((N_LANE_BLOCKS, tm, LANE), lambda i: (0, i, 0))],
        out_shape=[jax.ShapeDtypeStruct((rows, POOL_WIDTH), F32),
                   jax.ShapeDtypeStruct((N_LANE_BLOCKS, rows, LANE), F32)],
        compiler_params=_params(("arbitrary",), 40),
        name="in_proj",
    )(x, g, w)


POOL_HIST = POOL_BUF + 1


def _pool_seq_kernel(u_ref, wp_ref, sc_ref, o_ref, ext_ref, *, tc, tiles_per_seq):
    it = lax.rem(pl.program_id(0), tiles_per_seq)

    @pl.when(it == 0)
    def _():
        ext_ref[:POOL_HIST, :] = jnp.zeros((POOL_HIST, POOL_WIDTH), F32)

    ext_ref[POOL_HIST:, :] = u_ref[...]
    pos = it * tc + lax.broadcasted_iota(jnp.int32, (tc, 1), 0)
    for g, w in enumerate(POOL_WINDOWS):
        sl = slice(g * POOL_GROUP, (g + 1) * POOL_GROUP)
        z = ext_ref[:, sl]
        s, span = z, 1
        while span < w:
            s = s + pltpu.roll(s, span, 0)
            span *= 2
        cnt = jnp.minimum(pos + 1, w).astype(F32)
        pooled = (s[POOL_HIST:] / cnt - z[POOL_HIST:]).astype(BF16)
        o_ref[:, sl] = _dot(pooled, wp_ref[g]) * sc_ref[:, sl]
    ext_ref[:POOL_HIST, :] = ext_ref[tc:tc + POOL_HIST, :]


def _pool_mix_seq(u, w_pool, scale, *, tc, rows_per_seq):
    rows = u.shape[0]
    row_spec = pl.BlockSpec((tc, POOL_WIDTH), lambda i: (i, 0))
    return pl.pallas_call(
        functools.partial(_pool_seq_kernel, tc=tc, tiles_per_seq=rows_per_seq // tc),
        grid=(rows // tc,),
        in_specs=[row_spec, _const_spec((len(POOL_WINDOWS), POOL_GROUP, POOL_GROUP)),
                  _const_spec((1, POOL_WIDTH))],
        out_specs=row_spec,
        out_shape=jax.ShapeDtypeStruct(u.shape, F32),
        scratch_shapes=[pltpu.VMEM((tc + POOL_HIST, POOL_WIDTH), F32)],
        compiler_params=_params(("arbitrary",), 40),
        name="pool_mix_seq",
    )(u, w_pool, scale)


def _pool_buf_kernel(u_ref, buf_ref, wp_ref, sc_ref, o_ref, nb_ref, *, L):
    W = POOL_WIDTH
    R = u_ref.shape[0]

    def slab(idx, sl):
        ref, k = (buf_ref, idx) if idx < POOL_BUF else (u_ref, idx - POOL_BUF)
        return ref[:, k * W + sl.start:k * W + sl.stop]

    for g, w in enumerate(POOL_WINDOWS):
        sl = slice(g * POOL_GROUP, (g + 1) * POOL_GROUP)
        pooled = []
        for t in range(L):
            acc = slab(POOL_BUF + t, sl)
            for back in range(1, w):
                acc = acc + slab(POOL_BUF + t - back, sl)
            pooled.append((acc / float(w) - slab(POOL_BUF + t, sl)).astype(BF16))
        out = _dot(jnp.concatenate(pooled, axis=0), wp_ref[g]) * sc_ref[:, sl]
        for t in range(L):
            o_ref[:, t * W + sl.start:t * W + sl.stop] = out[t * R:(t + 1) * R]
    keep = POOL_BUF - L
    nb_ref[:, :keep * W] = buf_ref[:, L * W:]
    nb_ref[:, keep * W:] = u_ref[...]


def _pool_mix_buf(u_chunks, buf_chunks, w_pool, scale, *, L):
    nseq = u_chunks.shape[0]
    row_spec = pl.BlockSpec((nseq, L * POOL_WIDTH), lambda i: (0, 0))
    buf_spec = pl.BlockSpec((nseq, POOL_BUF * POOL_WIDTH), lambda i: (0, 0))
    return pl.pallas_call(
        functools.partial(_pool_buf_kernel, L=L),
        grid=(1,),
        in_specs=[row_spec, buf_spec, _const_spec((len(POOL_WINDOWS), POOL_GROUP, POOL_GROUP)),
                  _const_spec((1, POOL_WIDTH))],
        out_specs=[row_spec, buf_spec],
        out_shape=[jax.ShapeDtypeStruct(u_chunks.shape, F32),
                   jax.ShapeDtypeStruct(buf_chunks.shape, F32)],
        compiler_params=_params(("arbitrary",), 48),
        name="pool_mix_buf",
    )(u_chunks, buf_chunks, w_pool, scale)


def _cmul(ar, ai, br, bi):
    return ar * br - ai * bi, ar * bi + ai * br


def _cexp(lam_re, lam_im, log_step):
    delta = jnp.exp(log_step)
    mag = jnp.exp(lam_re * delta)
    ang = lam_im * delta
    return mag * jnp.cos(ang), mag * jnp.sin(ang)


def _ssm_prep_kernel(row_ref, col_ref, flat_ref, t_ref, w_ref, v_ref, al_ref, *, L):
    lr, li, ls, br, bi = (row_ref[i] for i in range(5))
    ar, ai = _cexp(lr, li, ls)
    den = lr * lr + li * li
    xr = ar - 1.0
    fr = (xr * lr + ai * li) / den
    fi = (ai * lr - xr * li) / den
    zs = [_cmul(fr, fi, br, bi)]
    for _ in range(1, L):
        zs.append(_cmul(ar, ai, *zs[-1]))

    lrc, lic, lsc, cr, ci = (col_ref[i] for i in range(5))
    acr, aci = _cexp(lrc, lic, lsc)
    xs = []
    cur = (cr, ci)
    for _ in range(L):
        cur = _cmul(acr, aci, *cur)
        xs.append(cur)

    def iota(shape, dim):
        return lax.broadcasted_iota(jnp.int32, shape, dim)

    same_tt = (iota((LANE, LANE), 0) >> 4) == (iota((LANE, LANE), 1) >> 4)
    c_stack = jnp.concatenate([cr, -ci], axis=0)
    zero_tile = jnp.zeros((LANE, LANE), BF16)
    lag = []
    for d in range(L):
        z_stack = jnp.concatenate([zs[d][0], zs[d][1]], axis=1)
        tile = jnp.dot(z_stack, c_stack, precision=HI, preferred_element_type=F32)
        lag.append(jnp.where(same_tt, tile, 0.0).astype(BF16))
    for k in range(L):
        for t in range(L):
            t_ref[0, k * LANE:(k + 1) * LANE, t * LANE:(t + 1) * LANE] = lag[t - k] if t >= k else zero_tile

    rep = ((iota((SSM_STATE, STATE_BLOCK), 1) & (SSM_STATE - 1)) == iota((SSM_STATE, STATE_BLOCK), 0)).astype(F32)
    same_w = (iota((LANE, STATE_BLOCK), 0) >> 4) == (iota((LANE, STATE_BLOCK), 1) >> 6)
    for k in range(L):
        zr, zi = zs[L - 1 - k]
        wr = jnp.where(same_w, jnp.dot(zr, rep, precision=HI, preferred_element_type=F32), 0.0)
        wi = jnp.where(same_w, jnp.dot(zi, rep, precision=HI, preferred_element_type=F32), 0.0)
        w_ref[0, k * LANE:(k + 1) * LANE, :] = jnp.concatenate([wr, wi], axis=1).astype(BF16)

    rep_t = ((iota((STATE_BLOCK, SSM_STATE), 0) & (SSM_STATE - 1)) == iota((STATE_BLOCK, SSM_STATE), 1)).astype(F32)
    same_v = (iota((STATE_BLOCK, LANE), 0) >> 6) == (iota((STATE_BLOCK, LANE), 1) >> 4)
    for t in range(L):
        xr_t, xi_t = xs[t]
        vr = jnp.where(same_v, jnp.dot(rep_t, xr_t, precision=HI, preferred_element_type=F32), 0.0)
        vi = jnp.where(same_v, jnp.dot(rep_t, xi_t, precision=HI, preferred_element_type=F32), 0.0)
        v_ref[0, :, t * LANE:(t + 1) * LANE] = jnp.concatenate([vr, -vi], axis=0).astype(BF16)

    flr, fli, fls = (flat_ref[i, 0] for i in range(3))
    far, fai = _cexp(flr, fli, fls)
    pr, pi = far, fai
    for _ in range(L - 1):
        pr, pi = _cmul(far, fai, pr, pi)
    al_ref[0] = jnp.concatenate([pr, pi], axis=1)


def _ssm_prep(row5, col5, flat3, L):
    lk = L * LANE
    return pl.pallas_call(
        functools.partial(_ssm_prep_kernel, L=L),
        grid=(N_LANE_BLOCKS,),
        in_specs=[pl.BlockSpec((5, LANE, SSM_STATE), lambda j: (0, j, 0)),
                  pl.BlockSpec((5, SSM_STATE, LANE), lambda j: (0, 0, j)),
                  pl.BlockSpec((3, 1, 1, STATE_BLOCK), lambda j: (0, j, 0, 0))],
        out_specs=[pl.BlockSpec((1, lk, lk), lambda j: (j, 0, 0)),
                   pl.BlockSpec((1, lk, 2 * STATE_BLOCK), lambda j: (j, 0, 0)),
                   pl.BlockSpec((1, 2 * STATE_BLOCK, lk), lambda j: (j, 0, 0)),
                   pl.BlockSpec((1, 1, 2 * STATE_BLOCK), lambda j: (j, 0, 0))],
        out_shape=[jax.ShapeDtypeStruct((N_LANE_BLOCKS, lk, lk), BF16),
                   jax.ShapeDtypeStruct((N_LANE_BLOCKS, lk, 2 * STATE_BLOCK), BF16),
                   jax.ShapeDtypeStruct((N_LANE_BLOCKS, 2 * STATE_BLOCK, lk), BF16),
                   jax.ShapeDtypeStruct((N_LANE_BLOCKS, 1, 2 * STATE_BLOCK), F32)],
        compiler_params=_params(("arbitrary",), 40),
        name=f"ssm_prep_L{L}",
    )(row5, col5, flat3)


def _ssm_kernel(u_ref, t_ref, w_ref, v_ref, al_ref, d_ref, h0_ref, g_ref, st_ref, *scratch, B, Rn, L, seq):
    M = B * Rn
    lk = L * LANE
    sb = STATE_BLOCK
    u = jnp.concatenate(
        [jnp.concatenate([u_ref[0, b, pl.ds(k, Rn, stride=L), :] for k in range(L)], axis=1)
         for b in range(B)], axis=0)
    ub = u.astype(BF16)
    e = _dot(ub, w_ref[0])
    ar = al_ref[0][:, :sb]
    ai = al_ref[0][:, sb:]
    if seq:
        e_scr, s_scr, c_scr = scratch

        @pl.when(pl.program_id(1) == 0)
        def _():
            c_scr[...] = h0_ref[0]

        e_scr[...] = e

        def body(r, carry):
            new = []
            for b in range(B):
                s = carry[b]
                s_scr[pl.ds(b * Rn + r, 1), :] = s
                eb = e_scr[pl.ds(b * Rn + r, 1), :]
                sr, si = s[:, :sb], s[:, sb:]
                nr = ar * sr - ai * si + eb[:, :sb]
                ni = ar * si + ai * sr + eb[:, sb:]
                new.append(jnp.concatenate([nr, ni], axis=1))
            return tuple(new)

        fin = lax.fori_loop(0, Rn, body, tuple(c_scr[b:b + 1, :] for b in range(B)))
        for b in range(B):
            c_scr[b:b + 1, :] = fin[b]
        st_ref[0] = c_scr[...]
        s_start = s_scr[...]
    else:
        s_start = h0_ref[0]
        sr, si = s_start[:, :sb], s_start[:, sb:]
        st_ref[0] = jnp.concatenate([ar * sr - ai * si + e[:, :sb], ar * si + ai * sr + e[:, sb:]], axis=1)
    y = _dot(ub, t_ref[0]) + _dot(s_start.astype(BF16), v_ref[0]) + d_ref[0] * u
    g = jax.nn.gelu(y)
    for b in range(B):
        for k in range(L):
            g_ref[0, b, pl.ds(k, Rn, stride=L), :] = g[b * Rn:(b + 1) * Rn, k * LANE:(k + 1) * LANE]


def _ssm_mix(u_view, ops, d_tiles, h0, *, L, Rn, seq):
    t_op, w_op, v_op, a_l = ops
    nj, B, trows, _ = u_view.shape
    lk = L * LANE
    nseq = h0.shape[1]
    tiles = trows // (Rn * L)
    M = B * Rn
    sb2 = 2 * STATE_BLOCK
    u_spec = pl.BlockSpec((1, B, Rn * L, LANE), lambda j, i: (j, 0, i, 0))
    scratch = [pltpu.VMEM((M, sb2), F32), pltpu.VMEM((M, sb2), F32), pltpu.VMEM((B, sb2), F32)] if seq else []
    return pl.pallas_call(
        functools.partial(_ssm_kernel, B=B, Rn=Rn, L=L, seq=seq),
        grid=(nj, tiles),
        in_specs=[u_spec,
                  pl.BlockSpec((1, lk, lk), lambda j, i: (j, 0, 0)),
                  pl.BlockSpec((1, lk, sb2), lambda j, i: (j, 0, 0)),
                  pl.BlockSpec((1, sb2, lk), lambda j, i: (j, 0, 0)),
                  pl.BlockSpec((1, 1, sb2), lambda j, i: (j, 0, 0)),
                  pl.BlockSpec((1, 1, lk), lambda j, i: (j, 0, 0)),
                  pl.BlockSpec((1, nseq, sb2), lambda j, i: (j, 0, 0))],
        out_specs=[u_spec, pl.BlockSpec((1, nseq, sb2), lambda j, i: (j, 0, 0))],
        out_shape=[jax.ShapeDtypeStruct(u_view.shape, F32),
                   jax.ShapeDtypeStruct((nj, nseq, sb2), F32)],
        scratch_shapes=scratch,
        compiler_params=_params(("arbitrary", "arbitrary"), 48),
        name="ssm_mix_seq" if seq else "ssm_mix_rows",
    )(u_view, t_op, w_op, v_op, a_l, d_tiles, h0)


def _mix_out_kernel(g_ref, po_ref, x_ref, wglu_ref, bglu_ref, wout_ref, gc_ref, wq_ref, x1_ref, q_ref):
    g = jnp.concatenate([g_ref[j] for j in range(N_LANE_BLOCKS)], axis=1)
    gate = jax.nn.sigmoid(_dot(g.astype(BF16), wglu_ref[...]) + bglu_ref[...])
    ssm_out = (g * gate).astype(BF16)
    mix = _dot(po_ref[...].astype(BF16), wout_ref[:POOL_WIDTH, :]) + _dot(ssm_out, wout_ref[POOL_WIDTH:, :])
    x1 = x_ref[...] + mix
    x1_ref[...] = x1
    q_ref[...] = _dot(_rms(x1, gc_ref[...]).astype(BF16), wq_ref[...]).astype(BF16)


def _mix_out(g_rows, pool_out, x, w_glu, b_glu, w_out, g_cross, w_q, tm):
    rows = x.shape[0]
    return pl.pallas_call(
        _mix_out_kernel,
        grid=(rows // tm,),
        in_specs=[pl.BlockSpec((N_LANE_BLOCKS, tm, LANE), lambda i: (0, i, 0)),
                  pl.BlockSpec((tm, POOL_WIDTH), lambda i: (i, 0)),
                  pl.BlockSpec((tm, D_MODEL), lambda i: (i, 0)),
                  _const_spec((SSM_WIDTH, SSM_WIDTH)), _const_spec((1, SSM_WIDTH)),
                  _const_spec((D_MODEL, D_MODEL)), _const_spec((1, D_MODEL)),
                  _const_spec((D_MODEL, D_MODEL))],
        out_specs=[pl.BlockSpec((tm, D_MODEL), lambda i: (i, 0)),
                   pl.BlockSpec((tm, D_MODEL), lambda i: (i, 0))],
        out_shape=[jax.ShapeDtypeStruct((rows, D_MODEL), F32),
                   jax.ShapeDtypeStruct((rows, D_MODEL), BF16)],
        compiler_params=_params(("arbitrary",), 52),
        name="mix_out",
    )(g_rows, pool_out, x, w_glu, b_glu, w_out, g_cross, w_q)


def _mem_kv_kernel(m_ref, g_ref, w_ref, kv_ref, kvb_ref):
    kv = _dot(_rms(m_ref[...], g_ref[...]).astype(BF16), w_ref[0])
    kv_ref[0] = kv
    kvb_ref[0] = kv.astype(BF16)


def _mem_kv(mem, g_mem, w_kv, tm):
    rows = mem.shape[0]
    return pl.pallas_call(
        _mem_kv_kernel,
        grid=(2, rows // tm),
        in_specs=[pl.BlockSpec((tm, D_MODEL), lambda s, i: (i, 0)),
                  _const_spec((1, D_MODEL)),
                  pl.BlockSpec((1, D_MODEL, D_MODEL), lambda s, i: (s, 0, 0))],
        out_specs=[pl.BlockSpec((1, tm, D_MODEL), lambda s, i: (s, i, 0)),
                   pl.BlockSpec((1, tm, D_MODEL), lambda s, i: (s, i, 0))],
        out_shape=[jax.ShapeDtypeStruct((2, rows, D_MODEL), F32),
                   jax.ShapeDtypeStruct((2, rows, D_MODEL), BF16)],
        compiler_params=_params(("arbitrary", "arbitrary"), 48),
        name="mem_kv",
    )(mem, g_mem, w_kv)


_NT = (((1,), (1,)), ((), ()))


def _softmax_rows(s):
    e = jnp.exp(s - jnp.max(s, axis=-1, keepdims=True))
    return e / jnp.sum(e, axis=-1, keepdims=True)


def _attn_prompt_kernel(q_ref, kv_ref, o_ref):
    for h in range(N_XHEADS):
        sl = slice(h * XHEAD_DIM, (h + 1) * XHEAD_DIM)
        s = lax.dot_general(q_ref[:, sl], kv_ref[0, :, sl], _NT, preferred_element_type=F32)
        p = _softmax_rows(s * (XHEAD_DIM ** -0.5))
        o_ref[:, sl] = _dot(p.astype(BF16), kv_ref[1, :, sl]).astype(BF16)


def _attn_prompt(q, kv_bf, tq, rows_per_seq):
    rows = q.shape[0]
    tiles = rows_per_seq // tq
    return pl.pallas_call(
        _attn_prompt_kernel,
        grid=(rows // tq,),
        in_specs=[pl.BlockSpec((tq, D_MODEL), lambda i: (i, 0)),
                  pl.BlockSpec((2, N_MEM, D_MODEL), lambda i: (0, i // tiles, 0))],
        out_specs=pl.BlockSpec((tq, D_MODEL), lambda i: (i, 0)),
        out_shape=jax.ShapeDtypeStruct((rows, D_MODEL), BF16),
        compiler_params=_params(("arbitrary",), 40),
        name="attn_prompt",
    )(q, kv_bf)


HEAD_LANE_BLOCKS = XHEAD_DIM // LANE
CACHE_ROW_PITCH = N_XHEADS * HEAD_LANE_BLOCKS


def _cache_rows_view(cache):
    S = cache.shape[0]
    c5 = cache.reshape(S, N_MEM, N_XHEADS, HEAD_LANE_BLOCKS, LANE)
    return c5.transpose(0, 1, 3, 2, 4).reshape(S, N_MEM * CACHE_ROW_PITCH, LANE)


def _attn_cache_kernel(q_ref, k_ref, v_ref, o_ref, *, bb):
    def head_matrix(ref, b, h):
        blocks = [ref[b, pl.ds(c * N_XHEADS + h, N_MEM, stride=CACHE_ROW_PITCH), :]
                  for c in range(HEAD_LANE_BLOCKS)]
        return jnp.concatenate(blocks, axis=1).astype(BF16)

    for b in range(bb):
        for h in range(N_XHEADS):
            sl = slice(h * XHEAD_DIM, (h + 1) * XHEAD_DIM)
            s = lax.dot_general(q_ref[b, :, sl].astype(BF16), head_matrix(k_ref, b, h), _NT,
                                preferred_element_type=F32)
            p = _softmax_rows(s * (XHEAD_DIM ** -0.5))
            o_ref[b, :, sl] = _dot(p.astype(BF16), head_matrix(v_ref, b, h))


def _attn_cache(q8, k_rows, v_rows, bb):
    nseq = q8.shape[0]
    rows = N_MEM * CACHE_ROW_PITCH
    return pl.pallas_call(
        functools.partial(_attn_cache_kernel, bb=bb),
        grid=(nseq // bb,),
        in_specs=[pl.BlockSpec((bb, 8, D_MODEL), lambda i: (i, 0, 0)),
                  pl.BlockSpec((bb, rows, LANE), lambda i: (i, 0, 0)),
                  pl.BlockSpec((bb, rows, LANE), lambda i: (i, 0, 0))],
        out_specs=pl.BlockSpec((bb, 8, D_MODEL), lambda i: (i, 0, 0)),
        out_shape=jax.ShapeDtypeStruct((nseq, 8, D_MODEL), F32),
        compiler_params=_params(("arbitrary",), 48),
        name="attn_cache",
    )(q8, k_rows, v_rows)


def _oproj_kernel(o_ref, x1_ref, wo_ref, x2_ref):
    x2_ref[...] = x1_ref[...] + _dot(o_ref[...], wo_ref[...])


def _oproj(o, x1, w_o, tm):
    rows = x1.shape[0]
    return pl.pallas_call(
        _oproj_kernel,
        grid=(rows // tm,),
        in_specs=[pl.BlockSpec((tm, D_MODEL), lambda i: (i, 0)),
                  pl.BlockSpec((tm, D_MODEL), lambda i: (i, 0)),
                  _const_spec((D_MODEL, D_MODEL))],
        out_specs=pl.BlockSpec((tm, D_MODEL), lambda i: (i, 0)),
        out_shape=jax.ShapeDtypeStruct((rows, D_MODEL), F32),
        compiler_params=_params(("arbitrary",), 40),
        name="oproj",
    )(o, x1, w_o)


def _ffn_kernel(x_ref, gf_ref, wg_ref, wu_ref, wd_ref, gl_ref, y_ref, h_scr, acc_scr):
    f = pl.program_id(1)

    @pl.when(f == 0)
    def _():
        h_scr[...] = _rms(x_ref[...], gf_ref[...]).astype(BF16)
        acc_scr[...] = jnp.zeros_like(acc_scr)

    h = h_scr[...]
    z = (jax.nn.silu(_dot(h, wg_ref[...])) * _dot(h, wu_ref[...])).astype(BF16)
    acc_scr[...] += _dot(z, wd_ref[...])

    @pl.when(f == pl.num_programs(1) - 1)
    def _():
        y_ref[...] = _rms(x_ref[...] + acc_scr[...], gl_ref[...])


def _ffn(x2, g_ffn, w_gate, w_up, w_down, g_final, tm, tf):
    rows = x2.shape[0]
    return pl.pallas_call(
        _ffn_kernel,
        grid=(rows // tm, D_FF // tf),
        in_specs=[pl.BlockSpec((tm, D_MODEL), lambda i, f: (i, 0)),
                  _const_spec((1, D_MODEL)),
                  pl.BlockSpec((D_MODEL, tf), lambda i, f: (0, f)),
                  pl.BlockSpec((D_MODEL, tf), lambda i, f: (0, f)),
                  pl.BlockSpec((tf, D_MODEL), lambda i, f: (f, 0)),
                  _const_spec((1, D_MODEL))],
        out_specs=pl.BlockSpec((tm, D_MODEL), lambda i, f: (i, 0)),
        out_shape=jax.ShapeDtypeStruct((rows, D_MODEL), F32),
        scratch_shapes=[pltpu.VMEM((tm, D_MODEL), BF16), pltpu.VMEM((tm, D_MODEL), F32)],
        compiler_params=_params(("arbitrary", "arbitrary"), 48),
        name="ffn",
    )(x2, g_ffn, w_gate, w_up, w_down, g_final)


def _ssm_param_layouts(lam_re, lam_im, log_step, b_re, b_im, c_re, c_im):
    G, P = lam_re.shape
    ls = jnp.broadcast_to(log_step[:, None], (G, P))
    rep = lambda a: jnp.repeat(a, SSM_GROUP, axis=0)
    b_rows = lambda b: jnp.swapaxes(b, 1, 2).reshape(G * SSM_GROUP, P)
    row5 = jnp.stack([rep(lam_re), rep(lam_im), rep(ls), b_rows(b_re), b_rows(b_im)])
    col5 = jnp.stack([rep(lam_re).T, rep(lam_im).T, rep(ls).T,
                      c_re.reshape(G * SSM_GROUP, P).T, c_im.reshape(G * SSM_GROUP, P).T])
    flat = lambda a: a.reshape(N_LANE_BLOCKS, 1, STATE_BLOCK)
    flat3 = jnp.stack([flat(lam_re), flat(lam_im), flat(ls)])
    return row5, col5, flat3


def _states_to_blocks(h_re, h_im):
    S = h_re.shape[0]
    blk = lambda h: h.reshape(S, N_LANE_BLOCKS, STATE_BLOCK).transpose(1, 0, 2)
    return jnp.concatenate([blk(h_re), blk(h_im)], axis=-1)


def _blocks_to_states(st):
    S = st.shape[1]
    unblk = lambda a: a.transpose(1, 0, 2).reshape(1, S, N_SSM_GROUPS, SSM_STATE)
    return unblk(st[:, :, :STATE_BLOCK]), unblk(st[:, :, STATE_BLOCK:])


def kernel(x_prompt, x_sample, mem_prompt, state_pool_buf, state_ssm_re, state_ssm_im, cache_mem_k, cache_mem_v, g_mix, w_in, w_pool, pool_scale, ssm_lam_re, ssm_lam_im, ssm_log_step, ssm_b_re, ssm_b_im, ssm_c_re, ssm_c_im, ssm_d, w_glu, b_glu, w_out, g_cross, g_mem, w_q, w_k, w_v, w_o, g_ffn, w_gate, w_up, w_down, g_final):
    assert g_mix.shape[0] == 1, "single-layer step"
    B, T, _ = x_prompt.shape
    S, Ts, _ = x_sample.shape
    Lp, Ls = 8, Ts
    bf = lambda w: w.astype(BF16)
    vec = lambda v: v.reshape(1, -1)

    w_in_b, w_pool_b, w_glu_b, w_out_b = bf(w_in[0]), bf(w_pool[0]), bf(w_glu[0]), bf(w_out[0])
    w_q_b, w_o_b = bf(w_q[0]), bf(w_o[0])
    w_kv_b = bf(jnp.stack([w_k[0], w_v[0]]))
    w_gate_b, w_up_b, w_down_b = bf(w_gate[0]), bf(w_up[0]), bf(w_down[0])

    row5, col5, flat3 = _ssm_param_layouts(ssm_lam_re[0], ssm_lam_im[0], ssm_log_step[0],
                                           ssm_b_re[0], ssm_b_im[0], ssm_c_re[0], ssm_c_im[0])
    d_blocks = ssm_d[0].reshape(N_LANE_BLOCKS, 1, LANE)

    def layer(x, nseq, t_len, L, pool_buf, h0_blocks, attend, tm):
        rows = nseq * t_len
        u_pool, u_ssm = _in_proj(x, vec(g_mix[0]), w_in_b, tm)

        if pool_buf is None:
            pool_out = _pool_mix_seq(u_pool, w_pool_b, vec(pool_scale[0]), tc=512, rows_per_seq=t_len)
            new_buf = u_pool.reshape(nseq, t_len, POOL_WIDTH)[:, t_len - POOL_BUF:]
        else:
            assert t_len == L and PAST_LEN >= POOL_BUF
            pool_out, new_buf = _pool_mix_buf(u_pool.reshape(nseq, L * POOL_WIDTH),
                                              pool_buf.reshape(nseq, POOL_BUF * POOL_WIDTH), w_pool_b,
                                              vec(pool_scale[0]), L=L)
            new_buf = new_buf.reshape(nseq, POOL_BUF, POOL_WIDTH)

        ops = _ssm_prep(row5, col5, flat3, L)
        d_tiles = jnp.tile(d_blocks, (1, 1, L))
        if t_len > L:
            u_view = u_ssm.reshape(N_LANE_BLOCKS, nseq, t_len, LANE)
            g_act, st = _ssm_mix(u_view, ops, d_tiles, h0_blocks, L=L, Rn=128, seq=True)
        else:
            u_view = u_ssm.reshape(N_LANE_BLOCKS, 1, rows, LANE)
            g_act, st = _ssm_mix(u_view, ops, d_tiles, h0_blocks, L=L, Rn=nseq, seq=False)
        new_re, new_im = _blocks_to_states(st)

        x1, q = _mix_out(g_act.reshape(N_LANE_BLOCKS, rows, LANE), pool_out.reshape(rows, POOL_WIDTH), x,
                         w_glu_b, vec(b_glu[0]), w_out_b, vec(g_cross[0]), w_q_b, 256)
        o = attend(q)
        x2 = _oproj(o, x1, w_o_b, tm)
        y = _ffn(x2, vec(g_ffn[0]), w_gate_b, w_up_b, w_down_b, vec(g_final), tm, 512)
        return y, new_buf, new_re, new_im

    kv, kv_b = _mem_kv(mem_prompt.reshape(B * N_MEM, D_MODEL), vec(g_mem[0]), w_kv_b, 512)
    h0_p = jnp.zeros((N_LANE_BLOCKS, B, 2 * STATE_BLOCK), F32)
    yp, pb_p, re_p, im_p = layer(x_prompt.reshape(B * T, D_MODEL), B, T, Lp, None, h0_p,
                                 lambda q: _attn_prompt(q, kv_b, 512, T), 512)

    def attend_cache(q):
        q8 = jnp.pad(q.reshape(S, Ts, D_MODEL).astype(F32), ((0, 0), (0, 8 - Ts), (0, 0)))
        o8 = _attn_cache(q8, _cache_rows_view(cache_mem_k[0]), _cache_rows_view(cache_mem_v[0]), 2)
        return o8[:, :Ts].reshape(S * Ts, D_MODEL).astype(BF16)

    h0_s = _states_to_blocks(state_ssm_re[0], state_ssm_im[0])
    ys, pb_s, re_s, im_s = layer(x_sample.reshape(S * Ts, D_MODEL), S, Ts, Ls, state_pool_buf[0], h0_s,
                                 attend_cache, 512)

    mk = kv[0].reshape(1, B, N_MEM, N_XHEADS, XHEAD_DIM)
    mv = kv[1].reshape(1, B, N_MEM, N_XHEADS, XHEAD_DIM)
    return (yp.reshape(B, T, D_MODEL), ys.reshape(S, Ts, D_MODEL), pb_p[None], re_p, im_p, mk, mv,
            pb_s[None], re_s, im_s)
```

```python
import functools

import jax
import jax.numpy as jnp
from jax import lax
from jax.experimental import pallas as pl
from jax.experimental.pallas import tpu as pltpu

F32 = jnp.float32
BF16 = jnp.bfloat16

D_MODEL = 2048
POOL_WIDTH = 1024
SSM_WIDTH = 1024
POOL_WINDOWS = (2, 4, 8, 16)
POOL_GROUP = POOL_WIDTH // len(POOL_WINDOWS)
POOL_BUF = max(POOL_WINDOWS) - 1
SSM_GROUP = 16
N_SSM_GROUPS = SSM_WIDTH // SSM_GROUP
SSM_STATE = 64
N_MEM = 256
N_XHEADS = 4
XHEAD_DIM = D_MODEL // N_XHEADS
D_FF = 5632
EPS = 1e-6
PAST_LEN = 16384

LANE = 128
N_LANE_BLOCKS = SSM_WIDTH // LANE
GROUPS_PER_BLOCK = LANE // SSM_GROUP
STATE_BLOCK = GROUPS_PER_BLOCK * SSM_STATE
MIB = 1024 * 1024
HI = lax.Precision.HIGHEST


def _params(semantics, vmem_mib):
    return pltpu.CompilerParams(dimension_semantics=semantics, vmem_limit_bytes=vmem_mib * MIB)


def _const_spec(shape):
    return pl.BlockSpec(shape, lambda *_: (0,) * len(shape), pipeline_mode=pl.Buffered(1))


def _rms(x, g):
    r = lax.rsqrt(jnp.mean(x * x, axis=-1, keepdims=True) + EPS)
    return x * r * g


def _dot(a, b):
    return jnp.dot(a, b, preferred_element_type=F32)


def _in_proj_kernel(x_ref, g_ref, w_ref, up_ref, us_ref):
    h = _rms(x_ref[...], g_ref[...]).astype(BF16)
    u = _dot(h, w_ref[...])
    up_ref[...] = u[:, :POOL_WIDTH]
    for j in range(N_LANE_BLOCKS):
        us_ref[j] = u[:, POOL_WIDTH + j * LANE:POOL_WIDTH + (j + 1) * LANE]


def _in_proj(x, g, w, tm):
    rows = x.shape[0]
    return pl.pallas_call(
        _in_proj_kernel,
        grid=(rows // tm,),
        in_specs=[pl.BlockSpec((tm, D_MODEL), lambda i: (i, 0)),
                  _const_spec((1, D_MODEL)),
                  _const_spec((D_MODEL, D_MODEL))],
        out_specs=[pl.BlockSpec((tm, POOL_WIDTH), lambda i: (i, 0)),
                   pl.BlockSpec((N_LANE_BLOCKS, tm, LANE), lambda i: (0, i, 0))],
        out_shape=[jax.ShapeDtypeStruct((rows, POOL_WIDTH), F32),
                   jax.ShapeDtypeStruct((N_LANE_BLOCKS, rows, LANE), F32)],
        compiler_params=_params(("arbitrary",), 40),
        name="in_proj",
    )(x, g, w)


POOL_HIST = POOL_BUF + 1


def _pool_seq_kernel(u_ref, wp_ref, sc_ref, o_ref, ext_ref, *, tc, tiles_per_seq):
    it = lax.rem(pl.program_id(0), tiles_per_seq)

    @pl.when(it == 0)
    def _():
        ext_ref[:POOL_HIST, :] = jnp.zeros((POOL_HIST, POOL_WIDTH), F32)

    ext_ref[POOL_HIST:, :] = u_ref[...]
    pos = it * tc + lax.broadcasted_iota(jnp.int32, (tc, 1), 0)
    for g, w in enumerate(POOL_WINDOWS):
        sl = slice(g * POOL_GROUP, (g + 1) * POOL_GROUP)
        z = ext_ref[:, sl]
        s, span = z, 1
        while span < w:
            s = s + pltpu.roll(s, span, 0)
            span *= 2
        cnt = jnp.minimum(pos + 1, w).astype(F32)
        pooled = (s[POOL_HIST:] / cnt - z[POOL_HIST:]).astype(BF16)
        o_ref[:, sl] = _dot(pooled, wp_ref[g]) * sc_ref[:, sl]
    ext_ref[:POOL_HIST, :] = ext_ref[tc:tc + POOL_HIST, :]


def _pool_mix_seq(u, w_pool, scale, *, tc, rows_per_seq):
    rows = u.shape[0]
    row_spec = pl.BlockSpec((tc, POOL_WIDTH), lambda i: (i, 0))
    return pl.pallas_call(
        functools.partial(_pool_seq_kernel, tc=tc, tiles_per_seq=rows_per_seq // tc),
        grid=(rows // tc,),
        in_specs=[row_spec, _const_spec((len(POOL_WINDOWS), POOL_GROUP, POOL_GROUP)),
                  _const_spec((1, POOL_WIDTH))],
        out_specs=row_spec,
        out_shape=jax.ShapeDtypeStruct(u.shape, F32),
        scratch_shapes=[pltpu.VMEM((tc + POOL_HIST, POOL_WIDTH), F32)],
        compiler_params=_params(("arbitrary",), 40),
        name="pool_mix_seq",
    )(u, w_pool, scale)


def _pool_buf_kernel(u_ref, buf_ref, wp_ref, sc_ref, o_ref, nb_ref, *, L):
    W = POOL_WIDTH
    R = u_ref.shape[0]

    def slab(idx, sl):
        ref, k = (buf_ref, idx) if idx < POOL_BUF else (u_ref, idx - POOL_BUF)
        return ref[:, k * W + sl.start:k * W + sl.stop]

    for g, w in enumerate(POOL_WINDOWS):
        sl = slice(g * POOL_GROUP, (g + 1) * POOL_GROUP)
        pooled = []
        for t in range(L):
            acc = slab(POOL_BUF + t, sl)
            for back in range(1, w):
                acc = acc + slab(POOL_BUF + t - back, sl)
            pooled.append((acc / float(w) - slab(POOL_BUF + t, sl)).astype(BF16))
        out = _dot(jnp.concatenate(pooled, axis=0), wp_ref[g]) * sc_ref[:, sl]
        for t in range(L):
            o_ref[:, t * W + sl.start:t * W + sl.stop] = out[t * R:(t + 1) * R]
    keep = POOL_BUF - L
    nb_ref[:, :keep * W] = buf_ref[:, L * W:]
    nb_ref[:, keep * W:] = u_ref[...]


def _pool_mix_buf(u_chunks, buf_chunks, w_pool, scale, *, L):
    nseq = u_chunks.shape[0]
    row_spec = pl.BlockSpec((nseq, L * POOL_WIDTH), lambda i: (0, 0))
    buf_spec = pl.BlockSpec((nseq, POOL_BUF * POOL_WIDTH), lambda i: (0, 0))
    return pl.pallas_call(
        functools.partial(_pool_buf_kernel, L=L),
        grid=(1,),
        in_specs=[row_spec, buf_spec, _const_spec((len(POOL_WINDOWS), POOL_GROUP, POOL_GROUP)),
                  _const_spec((1, POOL_WIDTH))],
        out_specs=[row_spec, buf_spec],
        out_shape=[jax.ShapeDtypeStruct(u_chunks.shape, F32),
                   jax.ShapeDtypeStruct(buf_chunks.shape, F32)],
        compiler_params=_params(("arbitrary",), 48),
        name="pool_mix_buf",
    )(u_chunks, buf_chunks, w_pool, scale)


def _cmul(ar, ai, br, bi):
    return ar * br - ai * bi, ar * bi + ai * br


def _cexp(lam_re, lam_im, log_step):
    delta = jnp.exp(log_step)
    mag = jnp.exp(lam_re * delta)
    ang = lam_im * delta
    return mag * jnp.cos(ang), mag * jnp.sin(ang)


def _ssm_prep_kernel(row_ref, col_ref, flat_ref, t_ref, w_ref, v_ref, al_ref, *, L):
    lr, li, ls, br, bi = (row_ref[i] for i in range(5))
    ar, ai = _cexp(lr, li, ls)
    den = lr * lr + li * li
    xr = ar - 1.0
    fr = (xr * lr + ai * li) / den
    fi = (ai * lr - xr * li) / den
    zs = [_cmul(fr, fi, br, bi)]
    for _ in range(1, L):
        zs.append(_cmul(ar, ai, *zs[-1]))

    lrc, lic, lsc, cr, ci = (col_ref[i] for i in range(5))
    acr, aci = _cexp(lrc, lic, lsc)
    xs = []
    cur = (cr, ci)
    for _ in range(L):
        cur = _cmul(acr, aci, *cur)
        xs.append(cur)

    def iota(shape, dim):
        return lax.broadcasted_iota(jnp.int32, shape, dim)

    same_tt = (iota((LANE, LANE), 0) >> 4) == (iota((LANE, LANE), 1) >> 4)
    c_stack = jnp.concatenate([cr, -ci], axis=0)
    zero_tile = jnp.zeros((LANE, LANE), BF16)
    lag = []
    for d in range(L):
        z_stack = jnp.concatenate([zs[d][0], zs[d][1]], axis=1)
        tile = jnp.dot(z_stack, c_stack, precision=HI, preferred_element_type=F32)
        lag.append(jnp.where(same_tt, tile, 0.0).astype(BF16))
    for k in range(L):
        for t in range(L):
            t_ref[0, k * LANE:(k + 1) * LANE, t * LANE:(t + 1) * LANE] = lag[t - k] if t >= k else zero_tile

    same_w = (iota((LANE, STATE_BLOCK), 0) >> 4) == (iota((LANE, STATE_BLOCK), 1) >> 6)
    for k in range(L):
        zr, zi = zs[L - 1 - k]
        wr = jnp.where(same_w, jnp.concatenate([zr] * GROUPS_PER_BLOCK, axis=1), 0.0)
        wi = jnp.where(same_w, jnp.concatenate([zi] * GROUPS_PER_BLOCK, axis=1), 0.0)
        w_ref[0, k * LANE:(k + 1) * LANE, :] = jnp.concatenate([wr, wi], axis=1).astype(BF16)

    same_v = (iota((STATE_BLOCK, LANE), 0) >> 6) == (iota((STATE_BLOCK, LANE), 1) >> 4)
    for t in range(L):
        xr_t, xi_t = xs[t]
        vr = jnp.where(same_v, jnp.concatenate([xr_t] * GROUPS_PER_BLOCK, axis=0), 0.0)
        vi = jnp.where(same_v, jnp.concatenate([xi_t] * GROUPS_PER_BLOCK, axis=0), 0.0)
        v_ref[0, :, t * LANE:(t + 1) * LANE] = jnp.concatenate([vr, -vi], axis=0).astype(BF16)

    flr, fli, fls = (flat_ref[i, 0] for i in range(3))
    far, fai = _cexp(flr, fli, fls)
    pr, pi = far, fai
    for _ in range(L - 1):
        pr, pi = _cmul(far, fai, pr, pi)
    al_ref[0] = jnp.concatenate([pr, pi], axis=1)


def _ssm_prep(row5, col5, flat3, L):
    lk = L * LANE
    return pl.pallas_call(
        functools.partial(_ssm_prep_kernel, L=L),
        grid=(N_LANE_BLOCKS,),
        in_specs=[pl.BlockSpec((5, LANE, SSM_STATE), lambda j: (0, j, 0)),
                  pl.BlockSpec((5, SSM_STATE, LANE), lambda j: (0, 0, j)),
                  pl.BlockSpec((3, 1, 1, STATE_BLOCK), lambda j: (0, j, 0, 0))],
        out_specs=[pl.BlockSpec((1, lk, lk), lambda j: (j, 0, 0)),
                   pl.BlockSpec((1, lk, 2 * STATE_BLOCK), lambda j: (j, 0, 0)),
                   pl.BlockSpec((1, 2 * STATE_BLOCK, lk), lambda j: (j, 0, 0)),
                   pl.BlockSpec((1, 1, 2 * STATE_BLOCK), lambda j: (j, 0, 0))],
        out_shape=[jax.ShapeDtypeStruct((N_LANE_BLOCKS, lk, lk), BF16),
                   jax.ShapeDtypeStruct((N_LANE_BLOCKS, lk, 2 * STATE_BLOCK), BF16),
                   jax.ShapeDtypeStruct((N_LANE_BLOCKS, 2 * STATE_BLOCK, lk), BF16),
                   jax.ShapeDtypeStruct((N_LANE_BLOCKS, 1, 2 * STATE_BLOCK), F32)],
        compiler_params=_params(("arbitrary",), 40),
        name=f"ssm_prep_L{L}",
    )(row5, col5, flat3)


def _ssm_kernel(u_ref, t_ref, w_ref, v_ref, al_ref, d_ref, h0_ref, g_ref, st_ref, *scratch, B, Rn, L, seq):
    M = B * Rn
    lk = L * LANE
    sb = STATE_BLOCK
    u = jnp.concatenate(
        [jnp.concatenate([u_ref[0, b, pl.ds(k, Rn, stride=L), :] for k in range(L)], axis=1)
         for b in range(B)], axis=0)
    ub = u.astype(BF16)
    e = _dot(ub, w_ref[0])
    ar = al_ref[0][:, :sb]
    ai = al_ref[0][:, sb:]
    if seq:
        e_scr, s_scr, c_scr = scratch

        @pl.when(pl.program_id(1) == 0)
        def _():
            c_scr[...] = h0_ref[0]

        e_scr[...] = e

        def body(r, carry):
            new = []
            for b in range(B):
                s = carry[b]
                s_scr[pl.ds(b * Rn + r, 1), :] = s
                eb = e_scr[pl.ds(b * Rn + r, 1), :]
                sr, si = s[:, :sb], s[:, sb:]
                nr = ar * sr - ai * si + eb[:, :sb]
                ni = ar * si + ai * sr + eb[:, sb:]
                new.append(jnp.concatenate([nr, ni], axis=1))
            return tuple(new)

        fin = lax.fori_loop(0, Rn, body, tuple(c_scr[b:b + 1, :] for b in range(B)))
        for b in range(B):
            c_scr[b:b + 1, :] = fin[b]
        st_ref[0] = c_scr[...]
        s_start = s_scr[...]
    else:
        s_start = h0_ref[0]
        sr, si = s_start[:, :sb], s_start[:, sb:]
        st_ref[0] = jnp.concatenate([ar * sr - ai * si + e[:, :sb], ar * si + ai * sr + e[:, sb:]], axis=1)
    y = _dot(ub, t_ref[0]) + _dot(s_start.astype(BF16), v_ref[0]) + d_ref[0] * u
    g = jax.nn.gelu(y)
    for b in range(B):
        for k in range(L):
            g_ref[0, b, pl.ds(k, Rn, stride=L), :] = g[b * Rn:(b + 1) * Rn, k * LANE:(k + 1) * LANE]


def _ssm_mix(u_view, ops, d_tiles, h0, *, L, Rn, seq):
    t_op, w_op, v_op, a_l = ops
    nj, B, trows, _ = u_view.shape
    lk = L * LANE
    nseq = h0.shape[1]
    tiles = trows // (Rn * L)
    M = B * Rn
    sb2 = 2 * STATE_BLOCK
    u_spec = pl.BlockSpec((1, B, Rn * L, LANE), lambda j, i: (j, 0, i, 0))
    scratch = [pltpu.VMEM((M, sb2), F32), pltpu.VMEM((M, sb2), F32), pltpu.VMEM((B, sb2), F32)] if seq else []
    return pl.pallas_call(
        functools.partial(_ssm_kernel, B=B, Rn=Rn, L=L, seq=seq),
        grid=(nj, tiles),
        in_specs=[u_spec,
                  pl.BlockSpec((1, lk, lk), lambda j, i: (j, 0, 0)),
                  pl.BlockSpec((1, lk, sb2), lambda j, i: (j, 0, 0)),
                  pl.BlockSpec((1, sb2, lk), lambda j, i: (j, 0, 0)),
                  pl.BlockSpec((1, 1, sb2), lambda j, i: (j, 0, 0)),
                  pl.BlockSpec((1, 1, lk), lambda j, i: (j, 0, 0)),
                  pl.BlockSpec((1, nseq, sb2), lambda j, i: (j, 0, 0))],
        out_specs=[u_spec, pl.BlockSpec((1, nseq, sb2), lambda j, i: (j, 0, 0))],
        out_shape=[jax.ShapeDtypeStruct(u_view.shape, F32),
                   jax.ShapeDtypeStruct((nj, nseq, sb2), F32)],
        scratch_shapes=scratch,
        compiler_params=_params(("arbitrary", "arbitrary"), 48),
        name="ssm_mix_seq" if seq else "ssm_mix_rows",
    )(u_view, t_op, w_op, v_op, a_l, d_tiles, h0)


def _mix_out_kernel(g_ref, po_ref, x_ref, wglu_ref, bglu_ref, wout_ref, gc_ref, wq_ref, x1_ref, q_ref):
    g = jnp.concatenate([g_ref[j] for j in range(N_LANE_BLOCKS)], axis=1)
    gate = jax.nn.sigmoid(_dot(g.astype(BF16), wglu_ref[...]) + bglu_ref[...])
    ssm_out = (g * gate).astype(BF16)
    mix = _dot(po_ref[...].astype(BF16), wout_ref[:POOL_WIDTH, :]) + _dot(ssm_out, wout_ref[POOL_WIDTH:, :])
    x1 = x_ref[...] + mix
    x1_ref[...] = x1
    q_ref[...] = _dot(_rms(x1, gc_ref[...]).astype(BF16), wq_ref[...]).astype(BF16)


def _mix_out(g_rows, pool_out, x, w_glu, b_glu, w_out, g_cross, w_q, tm):
    rows = x.shape[0]
    return pl.pallas_call(
        _mix_out_kernel,
        grid=(rows // tm,),
        in_specs=[pl.BlockSpec((N_LANE_BLOCKS, tm, LANE), lambda i: (0, i, 0)),
                  pl.BlockSpec((tm, POOL_WIDTH), lambda i: (i, 0)),
                  pl.BlockSpec((tm, D_MODEL), lambda i: (i, 0)),
                  _const_spec((SSM_WIDTH, SSM_WIDTH)), _const_spec((1, SSM_WIDTH)),
                  _const_spec((D_MODEL, D_MODEL)), _const_spec((1, D_MODEL)),
                  _const_spec((D_MODEL, D_MODEL))],
        out_specs=[pl.BlockSpec((tm, D_MODEL), lambda i: (i, 0)),
                   pl.BlockSpec((tm, D_MODEL), lambda i: (i, 0))],
        out_shape=[jax.ShapeDtypeStruct((rows, D_MODEL), F32),
                   jax.ShapeDtypeStruct((rows, D_MODEL), BF16)],
        compiler_params=_params(("arbitrary",), 52),
        name="mix_out",
    )(g_rows, pool_out, x, w_glu, b_glu, w_out, g_cross, w_q)


HEAD_LANE_BLOCKS = XHEAD_DIM // LANE
CACHE_ROW_PITCH = N_XHEADS * HEAD_LANE_BLOCKS


def _cache_rows_view(cache):
    S = cache.shape[0]
    c5 = cache.reshape(S, N_MEM, N_XHEADS, HEAD_LANE_BLOCKS, LANE)
    return c5.transpose(0, 1, 3, 2, 4).reshape(S, N_MEM * CACHE_ROW_PITCH, LANE)


def _cache_rows_unview(rows, nseq):
    r5 = rows.reshape(nseq, N_MEM, HEAD_LANE_BLOCKS, N_XHEADS, LANE)
    return r5.transpose(0, 1, 3, 2, 4).reshape(nseq, N_MEM, N_XHEADS, XHEAD_DIM)


def _mem_kv_kernel(m_ref, g_ref, w_ref, rows_ref, kvb_ref):
    tm = m_ref.shape[0]
    kv = _dot(_rms(m_ref[...], g_ref[...]).astype(BF16), w_ref[...].astype(BF16))
    kvb_ref[...] = kv.astype(BF16)
    for c in range(HEAD_LANE_BLOCKS):
        for h in range(N_XHEADS):
            lo = h * XHEAD_DIM + c * LANE
            rows_ref[pl.ds(c * N_XHEADS + h, tm, stride=CACHE_ROW_PITCH), :] = kv[:, lo:lo + LANE]


def _mem_kv(mem, g_mem, w, tm):
    rows = mem.shape[0]
    return pl.pallas_call(
        _mem_kv_kernel,
        grid=(rows // tm,),
        in_specs=[pl.BlockSpec((tm, D_MODEL), lambda i: (i, 0)),
                  _const_spec((1, D_MODEL)),
                  _const_spec((D_MODEL, D_MODEL))],
        out_specs=[pl.BlockSpec((tm * CACHE_ROW_PITCH, LANE), lambda i: (i, 0)),
                   pl.BlockSpec((tm, D_MODEL), lambda i: (i, 0))],
        out_shape=[jax.ShapeDtypeStruct((rows * CACHE_ROW_PITCH, LANE), F32),
                   jax.ShapeDtypeStruct((rows, D_MODEL), BF16)],
        compiler_params=_params(("arbitrary",), 48),
        name="mem_kv",
    )(mem, g_mem, w)


_NT = (((1,), (1,)), ((), ()))


def _softmax_rows(s):
    e = jnp.exp(s - jnp.max(s, axis=-1, keepdims=True))
    return e / jnp.sum(e, axis=-1, keepdims=True)


def _attn_prompt_kernel(q_ref, k_ref, v_ref, o_ref):
    for h in range(N_XHEADS):
        sl = slice(h * XHEAD_DIM, (h + 1) * XHEAD_DIM)
        s = lax.dot_general(q_ref[:, sl], k_ref[:, sl], _NT, preferred_element_type=F32)
        p = _softmax_rows(s * (XHEAD_DIM ** -0.5))
        o_ref[:, sl] = _dot(p.astype(BF16), v_ref[:, sl]).astype(BF16)


def _attn_prompt(q, k_bf, v_bf, tq, rows_per_seq):
    rows = q.shape[0]
    tiles = rows_per_seq // tq
    kv_spec = pl.BlockSpec((N_MEM, D_MODEL), lambda i: (i // tiles, 0))
    return pl.pallas_call(
        _attn_prompt_kernel,
        grid=(rows // tq,),
        in_specs=[pl.BlockSpec((tq, D_MODEL), lambda i: (i, 0)), kv_spec, kv_spec],
        out_specs=pl.BlockSpec((tq, D_MODEL), lambda i: (i, 0)),
        out_shape=jax.ShapeDtypeStruct((rows, D_MODEL), BF16),
        compiler_params=_params(("arbitrary",), 40),
        name="attn_prompt",
    )(q, k_bf, v_bf)


def _attn_cache_kernel(q_ref, k_ref, v_ref, o_ref, *, bb):
    def head_matrix(ref, b, h):
        blocks = [ref[b, pl.ds(c * N_XHEADS + h, N_MEM, stride=CACHE_ROW_PITCH), :]
                  for c in range(HEAD_LANE_BLOCKS)]
        return jnp.concatenate(blocks, axis=1).astype(BF16)

    for b in range(bb):
        for h in range(N_XHEADS):
            sl = slice(h * XHEAD_DIM, (h + 1) * XHEAD_DIM)
            s = lax.dot_general(q_ref[b, :, sl].astype(BF16), head_matrix(k_ref, b, h), _NT,
                                preferred_element_type=F32)
            p = _softmax_rows(s * (XHEAD_DIM ** -0.5))
            o_ref[b, :, sl] = _dot(p.astype(BF16), head_matrix(v_ref, b, h))


def _attn_cache(q8, k_rows, v_rows, bb):
    nseq = q8.shape[0]
    rows = N_MEM * CACHE_ROW_PITCH
    return pl.pallas_call(
        functools.partial(_attn_cache_kernel, bb=bb),
        grid=(nseq // bb,),
        in_specs=[pl.BlockSpec((bb, 8, D_MODEL), lambda i: (i, 0, 0)),
                  pl.BlockSpec((bb, rows, LANE), lambda i: (i, 0, 0)),
                  pl.BlockSpec((bb, rows, LANE), lambda i: (i, 0, 0))],
        out_specs=pl.BlockSpec((bb, 8, D_MODEL), lambda i: (i, 0, 0)),
        out_shape=jax.ShapeDtypeStruct((nseq, 8, D_MODEL), F32),
        compiler_params=_params(("arbitrary",), 48),
        name="attn_cache",
    )(q8, k_rows, v_rows)


def _oproj_kernel(o_ref, x1_ref, wo_ref, x2_ref):
    x2_ref[...] = x1_ref[...] + _dot(o_ref[...], wo_ref[...])


def _oproj(o, x1, w_o, tm):
    rows = x1.shape[0]
    return pl.pallas_call(
        _oproj_kernel,
        grid=(rows // tm,),
        in_specs=[pl.BlockSpec((tm, D_MODEL), lambda i: (i, 0)),
                  pl.BlockSpec((tm, D_MODEL), lambda i: (i, 0)),
                  _const_spec((D_MODEL, D_MODEL))],
        out_specs=pl.BlockSpec((tm, D_MODEL), lambda i: (i, 0)),
        out_shape=jax.ShapeDtypeStruct((rows, D_MODEL), F32),
        compiler_params=_params(("arbitrary",), 40),
        name="oproj",
    )(o, x1, w_o)


def _ffn_kernel(x_ref, gf_ref, wg_ref, wu_ref, wd_ref, gl_ref, y_ref, h_scr):
    f = pl.program_id(1)

    @pl.when(f == 0)
    def _():
        h_scr[...] = _rms(x_ref[...], gf_ref[...]).astype(BF16)
        y_ref[...] = jnp.zeros_like(y_ref)

    h = h_scr[...]
    z = (jax.nn.silu(_dot(h, wg_ref[...].astype(BF16))) * _dot(h, wu_ref[...].astype(BF16))).astype(BF16)
    y_ref[...] += _dot(z, wd_ref[...].astype(BF16))

    @pl.when(f == pl.num_programs(1) - 1)
    def _():
        y_ref[...] = _rms(x_ref[...] + y_ref[...], gl_ref[...])


def _ffn(x2, g_ffn, w_gate, w_up, w_down, g_final, tm, tf):
    rows = x2.shape[0]
    return pl.pallas_call(
        _ffn_kernel,
        grid=(rows // tm, D_FF // tf),
        in_specs=[pl.BlockSpec((tm, D_MODEL), lambda i, f: (i, 0)),
                  _const_spec((1, D_MODEL)),
                  pl.BlockSpec((D_MODEL, tf), lambda i, f: (0, f)),
                  pl.BlockSpec((D_MODEL, tf), lambda i, f: (0, f)),
                  pl.BlockSpec((tf, D_MODEL), lambda i, f: (f, 0)),
                  _const_spec((1, D_MODEL))],
        out_specs=pl.BlockSpec((tm, D_MODEL), lambda i, f: (i, 0), pipeline_mode=pl.Buffered(1)),
        out_shape=jax.ShapeDtypeStruct((rows, D_MODEL), F32),
        scratch_shapes=[pltpu.VMEM((tm, D_MODEL), BF16)],
        compiler_params=_params(("arbitrary", "arbitrary"), 56),
        name="ffn",
    )(x2, g_ffn, w_gate, w_up, w_down, g_final)


def _ssm_param_layouts(lam_re, lam_im, log_step, b_re, b_im, c_re, c_im):
    G, P = lam_re.shape
    ls = jnp.broadcast_to(log_step[:, None], (G, P))
    rep = lambda a: jnp.repeat(a, SSM_GROUP, axis=0)
    b_rows = lambda b: jnp.swapaxes(b, 1, 2).reshape(G * SSM_GROUP, P)
    row5 = jnp.stack([rep(lam_re), rep(lam_im), rep(ls), b_rows(b_re), b_rows(b_im)])
    col5 = jnp.stack([rep(lam_re).T, rep(lam_im).T, rep(ls).T,
                      c_re.reshape(G * SSM_GROUP, P).T, c_im.reshape(G * SSM_GROUP, P).T])
    flat = lambda a: a.reshape(N_LANE_BLOCKS, 1, STATE_BLOCK)
    flat3 = jnp.stack([flat(lam_re), flat(lam_im), flat(ls)])
    return row5, col5, flat3


def _states_to_blocks(h_re, h_im):
    S = h_re.shape[0]
    blk = lambda h: h.reshape(S, N_LANE_BLOCKS, STATE_BLOCK).transpose(1, 0, 2)
    return jnp.concatenate([blk(h_re), blk(h_im)], axis=-1)


def _blocks_to_states(st):
    S = st.shape[1]
    unblk = lambda a: a.transpose(1, 0, 2).reshape(1, S, N_SSM_GROUPS, SSM_STATE)
    return unblk(st[:, :, :STATE_BLOCK]), unblk(st[:, :, STATE_BLOCK:])


def kernel(x_prompt, x_sample, mem_prompt, state_pool_buf, state_ssm_re, state_ssm_im, cache_mem_k, cache_mem_v, g_mix, w_in, w_pool, pool_scale, ssm_lam_re, ssm_lam_im, ssm_log_step, ssm_b_re, ssm_b_im, ssm_c_re, ssm_c_im, ssm_d, w_glu, b_glu, w_out, g_cross, g_mem, w_q, w_k, w_v, w_o, g_ffn, w_gate, w_up, w_down, g_final):
    assert g_mix.shape[0] == 1, "single-layer step"
    B, T, _ = x_prompt.shape
    S, Ts, _ = x_sample.shape
    Lp, Ls = 8, Ts
    bf = lambda w: w.astype(BF16)
    vec = lambda v: v.reshape(1, -1)

    w_in_b, w_pool_b, w_glu_b, w_out_b = bf(w_in[0]), bf(w_pool[0]), bf(w_glu[0]), bf(w_out[0])
    w_q_b, w_o_b = bf(w_q[0]), bf(w_o[0])

    row5, col5, flat3 = _ssm_param_layouts(ssm_lam_re[0], ssm_lam_im[0], ssm_log_step[0],
                                           ssm_b_re[0], ssm_b_im[0], ssm_c_re[0], ssm_c_im[0])
    d_blocks = ssm_d[0].reshape(N_LANE_BLOCKS, 1, LANE)

    def layer(x, nseq, t_len, L, pool_buf, h0_blocks, attend, tm, tm_ffn):
        rows = nseq * t_len
        u_pool, u_ssm = _in_proj(x, vec(g_mix[0]), w_in_b, tm)

        if pool_buf is None:
            pool_out = _pool_mix_seq(u_pool, w_pool_b, vec(pool_scale[0]), tc=512, rows_per_seq=t_len)
            new_buf = u_pool.reshape(nseq, t_len, POOL_WIDTH)[:, t_len - POOL_BUF:]
        else:
            assert t_len == L and PAST_LEN >= POOL_BUF
            pool_out, new_buf = _pool_mix_buf(u_pool.reshape(nseq, L * POOL_WIDTH),
                                              pool_buf.reshape(nseq, POOL_BUF * POOL_WIDTH), w_pool_b,
                                              vec(pool_scale[0]), L=L)
            new_buf = new_buf.reshape(nseq, POOL_BUF, POOL_WIDTH)

        ops = _ssm_prep(row5, col5, flat3, L)
        d_tiles = jnp.tile(d_blocks, (1, 1, L))
        if t_len > L:
            u_view = u_ssm.reshape(N_LANE_BLOCKS, nseq, t_len, LANE)
            g_act, st = _ssm_mix(u_view, ops, d_tiles, h0_blocks, L=L, Rn=128, seq=True)
        else:
            u_view = u_ssm.reshape(N_LANE_BLOCKS, 1, rows, LANE)
            g_act, st = _ssm_mix(u_view, ops, d_tiles, h0_blocks, L=L, Rn=nseq, seq=False)
        new_re, new_im = _blocks_to_states(st)

        x1, q = _mix_out(g_act.reshape(N_LANE_BLOCKS, rows, LANE), pool_out.reshape(rows, POOL_WIDTH), x,
                         w_glu_b, vec(b_glu[0]), w_out_b, vec(g_cross[0]), w_q_b, 256)
        o = attend(q)
        x2 = _oproj(o, x1, w_o_b, tm)
        y = _ffn(x2, vec(g_ffn[0]), w_gate[0], w_up[0], w_down[0], vec(g_final), tm_ffn, 256)
        return y, new_buf, new_re, new_im

    mem = mem_prompt.reshape(B * N_MEM, D_MODEL)
    k_rows, k_b = _mem_kv(mem, vec(g_mem[0]), w_k[0], 512)
    v_rows, v_b = _mem_kv(mem, vec(g_mem[0]), w_v[0], 512)
    h0_p = jnp.zeros((N_LANE_BLOCKS, B, 2 * STATE_BLOCK), F32)
    yp, pb_p, re_p, im_p = layer(x_prompt.reshape(B * T, D_MODEL), B, T, Lp, None, h0_p,
                                 lambda q: _attn_prompt(q, k_b, v_b, 512, T), 512, 1024)

    def attend_cache(q):
        q8 = jnp.pad(q.reshape(S, Ts, D_MODEL).astype(F32), ((0, 0), (0, 8 - Ts), (0, 0)))
        o8 = _attn_cache(q8, _cache_rows_view(cache_mem_k[0]), _cache_rows_view(cache_mem_v[0]), 2)
        return o8[:, :Ts].reshape(S * Ts, D_MODEL).astype(BF16)

    h0_s = _states_to_blocks(state_ssm_re[0], state_ssm_im[0])
    ys, pb_s, re_s, im_s = layer(x_sample.reshape(S * Ts, D_MODEL), S, Ts, Ls, state_pool_buf[0], h0_s,
                                 attend_cache, 512, 512)

    mk = _cache_rows_unview(k_rows, B)[None]
    mv = _cache_rows_unview(v_rows, B)[None]
    return (yp.reshape(B, T, D_MODEL), ys.reshape(S, Ts, D_MODEL), pb_p[None], re_p, im_p, mk, mv,
            pb_s[None], re_s, im_s)
```

```python
import functools

import jax
import jax.numpy as jnp
from jax import lax
from jax.experimental import pallas as pl
from jax.experimental.pallas import tpu as pltpu

F32 = jnp.float32
BF16 = jnp.bfloat16

D_MODEL = 2048
POOL_WIDTH = 1024
SSM_WIDTH = 1024
POOL_WINDOWS = (2, 4, 8, 16)
POOL_GROUP = POOL_WIDTH // len(POOL_WINDOWS)
POOL_BUF = max(POOL_WINDOWS) - 1
SSM_GROUP = 16
N_SSM_GROUPS = SSM_WIDTH // SSM_GROUP
SSM_STATE = 64
N_MEM = 256
N_XHEADS = 4
XHEAD_DIM = D_MODEL // N_XHEADS
D_FF = 5632
EPS = 1e-6
PAST_LEN = 16384

LANE = 128
N_LANE_BLOCKS = SSM_WIDTH // LANE
GROUPS_PER_BLOCK = LANE // SSM_GROUP
STATE_BLOCK = GROUPS_PER_BLOCK * SSM_STATE
MIB = 1024 * 1024
HI = lax.Precision.HIGHEST


def _params(semantics, vmem_mib):
    return pltpu.CompilerParams(dimension_semantics=semantics, vmem_limit_bytes=vmem_mib * MIB)


def _const_spec(shape):
    return pl.BlockSpec(shape, lambda *_: (0,) * len(shape), pipeline_mode=pl.Buffered(1))


def _rms(x, g):
    r = lax.rsqrt(jnp.mean(x * x, axis=-1, keepdims=True) + EPS)
    return x * r * g


def _dot(a, b):
    return jnp.dot(a, b, preferred_element_type=F32)


def _in_proj_kernel(x_ref, g_ref, w_ref, up_ref, us_ref):
    h = _rms(x_ref[...], g_ref[...]).astype(BF16)
    u = _dot(h, w_ref[...])
    up_ref[...] = u[:, :POOL_WIDTH]
    for j in range(N_LANE_BLOCKS):
        us_ref[j] = u[:, POOL_WIDTH + j * LANE:POOL_WIDTH + (j + 1) * LANE]


def _in_proj(x, g, w, tm):
    rows = x.shape[0]
    return pl.pallas_call(
        _in_proj_kernel,
        grid=(rows // tm,),
        in_specs=[pl.BlockSpec((tm, D_MODEL), lambda i: (i, 0)),
                  _const_spec((1, D_MODEL)),
                  _const_spec((D_MODEL, D_MODEL))],
        out_specs=[pl.BlockSpec((tm, POOL_WIDTH), lambda i: (i, 0)),
                   pl.BlockSpec((N_LANE_BLOCKS, tm, LANE), lambda i: (0, i, 0))],
        out_shape=[jax.ShapeDtypeStruct((rows, POOL_WIDTH), F32),
                   jax.ShapeDtypeStruct((N_LANE_BLOCKS, rows, LANE), F32)],
        compiler_params=_params(("arbitrary",), 40),
        name="in_proj",
    )(x, g, w)


POOL_HIST = POOL_BUF + 1


def _pool_seq_kernel(u_ref, wp_ref, sc_ref, o_ref, ext_ref, *, tc, tiles_per_seq):
    it = lax.rem(pl.program_id(0), tiles_per_seq)

    @pl.when(it == 0)
    def _():
        ext_ref[:POOL_HIST, :] = jnp.zeros((POOL_HIST, POOL_WIDTH), F32)

    ext_ref[POOL_HIST:, :] = u_ref[...]
    pos = it * tc + lax.broadcasted_iota(jnp.int32, (tc, 1), 0)
    for g, w in enumerate(POOL_WINDOWS):
        sl = slice(g * POOL_GROUP, (g + 1) * POOL_GROUP)
        z = ext_ref[:, sl]
        s, span = z, 1
        while span < w:
            s = s + pltpu.roll(s, span, 0)
            span *= 2
        cnt = jnp.minimum(pos + 1, w).astype(F32)
        pooled = (s[POOL_HIST:] / cnt - z[POOL_HIST:]).astype(BF16)
        o_ref[:, sl] = _dot(pooled, wp_ref[g]) * sc_ref[:, sl]
    ext_ref[:POOL_HIST, :] = ext_ref[tc:tc + POOL_HIST, :]


def _pool_mix_seq(u, w_pool, scale, *, tc, rows_per_seq):
    rows = u.shape[0]
    row_spec = pl.BlockSpec((tc, POOL_WIDTH), lambda i: (i, 0))
    return pl.pallas_call(
        functools.partial(_pool_seq_kernel, tc=tc, tiles_per_seq=rows_per_seq // tc),
        grid=(rows // tc,),
        in_specs=[row_spec, _const_spec((len(POOL_WINDOWS), POOL_GROUP, POOL_GROUP)),
                  _const_spec((1, POOL_WIDTH))],
        out_specs=row_spec,
        out_shape=jax.ShapeDtypeStruct(u.shape, F32),
        scratch_shapes=[pltpu.VMEM((tc + POOL_HIST, POOL_WIDTH), F32)],
        compiler_params=_params(("arbitrary",), 40),
        name="pool_mix_seq",
    )(u, w_pool, scale)


def _pool_buf_kernel(u_ref, buf_ref, wp_ref, sc_ref, o_ref, nb_ref, *, L):
    W = POOL_WIDTH
    R = u_ref.shape[0]

    def slab(idx, sl):
        ref, k = (buf_ref, idx) if idx < POOL_BUF else (u_ref, idx - POOL_BUF)
        return ref[:, k * W + sl.start:k * W + sl.stop]

    for g, w in enumerate(POOL_WINDOWS):
        sl = slice(g * POOL_GROUP, (g + 1) * POOL_GROUP)
        pooled = []
        for t in range(L):
            acc = slab(POOL_BUF + t, sl)
            for back in range(1, w):
                acc = acc + slab(POOL_BUF + t - back, sl)
            pooled.append((acc / float(w) - slab(POOL_BUF + t, sl)).astype(BF16))
        out = _dot(jnp.concatenate(pooled, axis=0), wp_ref[g]) * sc_ref[:, sl]
        for t in range(L):
            o_ref[:, t * W + sl.start:t * W + sl.stop] = out[t * R:(t + 1) * R]
    keep = POOL_BUF - L
    nb_ref[:, :keep * W] = buf_ref[:, L * W:]
    nb_ref[:, keep * W:] = u_ref[...]


def _pool_mix_buf(u_chunks, buf_chunks, w_pool, scale, *, L):
    nseq = u_chunks.shape[0]
    row_spec = pl.BlockSpec((nseq, L * POOL_WIDTH), lambda i: (0, 0))
    buf_spec = pl.BlockSpec((nseq, POOL_BUF * POOL_WIDTH), lambda i: (0, 0))
    return pl.pallas_call(
        functools.partial(_pool_buf_kernel, L=L),
        grid=(1,),
        in_specs=[row_spec, buf_spec, _const_spec((len(POOL_WINDOWS), POOL_GROUP, POOL_GROUP)),
                  _const_spec((1, POOL_WIDTH))],
        out_specs=[row_spec, buf_spec],
        out_shape=[jax.ShapeDtypeStruct(u_chunks.shape, F32),
                   jax.ShapeDtypeStruct(buf_chunks.shape, F32)],
        compiler_params=_params(("arbitrary",), 48),
        name="pool_mix_buf",
    )(u_chunks, buf_chunks, w_pool, scale)


def _cmul(ar, ai, br, bi):
    return ar * br - ai * bi, ar * bi + ai * br


def _cexp(lam_re, lam_im, log_step):
    delta = jnp.exp(log_step)
    mag = jnp.exp(lam_re * delta)
    ang = lam_im * delta
    return mag * jnp.cos(ang), mag * jnp.sin(ang)


def _ssm_prep_kernel(row_ref, col_ref, flat_ref, t_ref, w_ref, v_ref, al_ref, *, L):
    lr, li, ls, br, bi = (row_ref[i] for i in range(5))
    ar, ai = _cexp(lr, li, ls)
    den = lr * lr + li * li
    xr = ar - 1.0
    fr = (xr * lr + ai * li) / den
    fi = (ai * lr - xr * li) / den
    zs = [_cmul(fr, fi, br, bi)]
    for _ in range(1, L):
        zs.append(_cmul(ar, ai, *zs[-1]))

    lrc, lic, lsc, cr, ci = (col_ref[i] for i in range(5))
    acr, aci = _cexp(lrc, lic, lsc)
    xs = []
    cur = (cr, ci)
    for _ in range(L):
        cur = _cmul(acr, aci, *cur)
        xs.append(cur)

    def iota(shape, dim):
        return lax.broadcasted_iota(jnp.int32, shape, dim)

    same_tt = (iota((LANE, LANE), 0) >> 4) == (iota((LANE, LANE), 1) >> 4)
    c_stack = jnp.concatenate([cr, -ci], axis=0)
    zero_tile = jnp.zeros((LANE, LANE), BF16)
    lag = []
    for d in range(L):
        z_stack = jnp.concatenate([zs[d][0], zs[d][1]], axis=1)
        tile = jnp.dot(z_stack, c_stack, precision=HI, preferred_element_type=F32)
        lag.append(jnp.where(same_tt, tile, 0.0).astype(BF16))
    for k in range(L):
        for t in range(L):
            t_ref[0, k * LANE:(k + 1) * LANE, t * LANE:(t + 1) * LANE] = lag[t - k] if t >= k else zero_tile

    same_w = (iota((LANE, STATE_BLOCK), 0) >> 4) == (iota((LANE, STATE_BLOCK), 1) >> 6)
    for k in range(L):
        zr, zi = zs[L - 1 - k]
        wr = jnp.where(same_w, jnp.concatenate([zr] * GROUPS_PER_BLOCK, axis=1), 0.0)
        wi = jnp.where(same_w, jnp.concatenate([zi] * GROUPS_PER_BLOCK, axis=1), 0.0)
        w_ref[0, k * LANE:(k + 1) * LANE, :] = jnp.concatenate([wr, wi], axis=1).astype(BF16)

    same_v = (iota((STATE_BLOCK, LANE), 0) >> 6) == (iota((STATE_BLOCK, LANE), 1) >> 4)
    for t in range(L):
        xr_t, xi_t = xs[t]
        vr = jnp.where(same_v, jnp.concatenate([xr_t] * GROUPS_PER_BLOCK, axis=0), 0.0)
        vi = jnp.where(same_v, jnp.concatenate([xi_t] * GROUPS_PER_BLOCK, axis=0), 0.0)
        v_ref[0, :, t * LANE:(t + 1) * LANE] = jnp.concatenate([vr, -vi], axis=0).astype(BF16)

    flr, fli, fls = (flat_ref[i, 0] for i in range(3))
    far, fai = _cexp(flr, fli, fls)
    pr, pi = far, fai
    for _ in range(L - 1):
        pr, pi = _cmul(far, fai, pr, pi)
    al_ref[0] = jnp.concatenate([pr, pi], axis=1)


def _ssm_prep(row5, col5, flat3, L):
    lk = L * LANE
    return pl.pallas_call(
        functools.partial(_ssm_prep_kernel, L=L),
        grid=(N_LANE_BLOCKS,),
        in_specs=[pl.BlockSpec((5, LANE, SSM_STATE), lambda j: (0, j, 0)),
                  pl.BlockSpec((5, SSM_STATE, LANE), lambda j: (0, 0, j)),
                  pl.BlockSpec((3, 1, 1, STATE_BLOCK), lambda j: (0, j, 0, 0))],
        out_specs=[pl.BlockSpec((1, lk, lk), lambda j: (j, 0, 0)),
                   pl.BlockSpec((1, lk, 2 * STATE_BLOCK), lambda j: (j, 0, 0)),
                   pl.BlockSpec((1, 2 * STATE_BLOCK, lk), lambda j: (j, 0, 0)),
                   pl.BlockSpec((1, 1, 2 * STATE_BLOCK), lambda j: (j, 0, 0))],
        out_shape=[jax.ShapeDtypeStruct((N_LANE_BLOCKS, lk, lk), BF16),
                   jax.ShapeDtypeStruct((N_LANE_BLOCKS, lk, 2 * STATE_BLOCK), BF16),
                   jax.ShapeDtypeStruct((N_LANE_BLOCKS, 2 * STATE_BLOCK, lk), BF16),
                   jax.ShapeDtypeStruct((N_LANE_BLOCKS, 1, 2 * STATE_BLOCK), F32)],
        compiler_params=_params(("arbitrary",), 40),
        name=f"ssm_prep_L{L}",
    )(row5, col5, flat3)


def _ssm_kernel(u_ref, t_ref, w_ref, v_ref, al_ref, d_ref, h0_ref, g_ref, st_ref, *scratch, B, Rn, L, seq):
    M = B * Rn
    sb = STATE_BLOCK
    io_scr = scratch[0]
    for b in range(B):
        for k in range(L):
            io_scr[k, pl.ds(b, Rn, stride=B), :] = u_ref[0, b, pl.ds(k, Rn, stride=L), :]
    u = jnp.concatenate([io_scr[k] for k in range(L)], axis=1)
    ub = u.astype(BF16)
    e = _dot(ub, w_ref[0])
    ar = al_ref[0][:, :sb]
    ai = al_ref[0][:, sb:]

    def advance(s, eb):
        sr, si = s[:, :sb], s[:, sb:]
        return jnp.concatenate([ar * sr - ai * si + eb[:, :sb], ar * si + ai * sr + eb[:, sb:]], axis=1)

    if seq:
        e_scr, s_scr, c_scr = scratch[1:]

        assert 2 * B == 8 and Rn % 2 == 0, "two chunk rows of B sequences fill one sublane tile"

        @pl.when(pl.program_id(1) == 0)
        def _():
            c_scr[:B, :] = h0_ref[0]
            c_scr[B:, :] = jnp.zeros((B, 2 * sb), F32)

        e_scr[...] = e
        top = lax.broadcasted_iota(jnp.int32, (2 * B, 1), 0) < B

        def body(i, s):
            rows = pl.ds(pl.multiple_of(i * 2 * B, 2 * B), 2 * B)
            e2 = e_scr[rows, :]
            mid = pltpu.roll(advance(s, e2), B, 0)
            s_scr[rows, :] = jnp.where(top, s, mid)
            return pltpu.roll(advance(mid, e2), B, 0)

        c_scr[...] = lax.fori_loop(0, Rn // 2, body, c_scr[...])
        st_ref[0] = c_scr[:B, :]
        s_start = s_scr[...]
    else:
        s_start = h0_ref[0]
        st_ref[0] = advance(s_start, e)
    y = _dot(ub, t_ref[0]) + _dot(s_start.astype(BF16), v_ref[0]) + d_ref[0] * u
    g = jax.nn.gelu(y)
    for k in range(L):
        io_scr[k] = g[:, k * LANE:(k + 1) * LANE]
    for b in range(B):
        for k in range(L):
            g_ref[0, b, pl.ds(k, Rn, stride=L), :] = io_scr[k, pl.ds(b, Rn, stride=B), :]


def _ssm_mix(u_view, ops, d_tiles, h0, *, L, Rn, seq):
    t_op, w_op, v_op, a_l = ops
    nj, B, trows, _ = u_view.shape
    lk = L * LANE
    nseq = h0.shape[1]
    tiles = trows // (Rn * L)
    M = B * Rn
    sb2 = 2 * STATE_BLOCK
    u_spec = pl.BlockSpec((1, B, Rn * L, LANE), lambda j, i: (j, 0, i, 0))
    scratch = [pltpu.VMEM((L, M, LANE), F32)]
    if seq:
        scratch += [pltpu.VMEM((M, sb2), F32), pltpu.VMEM((M, sb2), F32), pltpu.VMEM((2 * B, sb2), F32)]
    return pl.pallas_call(
        functools.partial(_ssm_kernel, B=B, Rn=Rn, L=L, seq=seq),
        grid=(nj, tiles),
        in_specs=[u_spec,
                  pl.BlockSpec((1, lk, lk), lambda j, i: (j, 0, 0)),
                  pl.BlockSpec((1, lk, sb2), lambda j, i: (j, 0, 0)),
                  pl.BlockSpec((1, sb2, lk), lambda j, i: (j, 0, 0)),
                  pl.BlockSpec((1, 1, sb2), lambda j, i: (j, 0, 0)),
                  pl.BlockSpec((1, 1, lk), lambda j, i: (j, 0, 0)),
                  pl.BlockSpec((1, nseq, sb2), lambda j, i: (j, 0, 0))],
        out_specs=[u_spec, pl.BlockSpec((1, nseq, sb2), lambda j, i: (j, 0, 0))],
        out_shape=[jax.ShapeDtypeStruct(u_view.shape, F32),
                   jax.ShapeDtypeStruct((nj, nseq, sb2), F32)],
        scratch_shapes=scratch,
        compiler_params=_params(("arbitrary", "arbitrary"), 48),
        name="ssm_mix_seq" if seq else "ssm_mix_rows",
    )(u_view, t_op, w_op, v_op, a_l, d_tiles, h0)


def _mix_out_kernel(g_ref, po_ref, x_ref, wglu_ref, bglu_ref, wout_ref, gc_ref, wq_ref, x1_ref, q_ref):
    g = jnp.concatenate([g_ref[j] for j in range(N_LANE_BLOCKS)], axis=1)
    gate = jax.nn.sigmoid(_dot(g.astype(BF16), wglu_ref[...]) + bglu_ref[...])
    ssm_out = (g * gate).astype(BF16)
    mix = _dot(po_ref[...].astype(BF16), wout_ref[:POOL_WIDTH, :]) + _dot(ssm_out, wout_ref[POOL_WIDTH:, :])
    x1 = x_ref[...] + mix
    x1_ref[...] = x1
    q_ref[...] = _dot(_rms(x1, gc_ref[...]).astype(BF16), wq_ref[...]).astype(BF16)


def _mix_out(g_rows, pool_out, x, w_glu, b_glu, w_out, g_cross, w_q, tm):
    rows = x.shape[0]
    return pl.pallas_call(
        _mix_out_kernel,
        grid=(rows // tm,),
        in_specs=[pl.BlockSpec((N_LANE_BLOCKS, tm, LANE), lambda i: (0, i, 0)),
                  pl.BlockSpec((tm, POOL_WIDTH), lambda i: (i, 0)),
                  pl.BlockSpec((tm, D_MODEL), lambda i: (i, 0)),
                  _const_spec((SSM_WIDTH, SSM_WIDTH)), _const_spec((1, SSM_WIDTH)),
                  _const_spec((D_MODEL, D_MODEL)), _const_spec((1, D_MODEL)),
                  _const_spec((D_MODEL, D_MODEL))],
        out_specs=[pl.BlockSpec((tm, D_MODEL), lambda i: (i, 0)),
                   pl.BlockSpec((tm, D_MODEL), lambda i: (i, 0))],
        out_shape=[jax.ShapeDtypeStruct((rows, D_MODEL), F32),
                   jax.ShapeDtypeStruct((rows, D_MODEL), BF16)],
        compiler_params=_params(("arbitrary",), 52),
        name="mix_out",
    )(g_rows, pool_out, x, w_glu, b_glu, w_out, g_cross, w_q)


HEAD_LANE_BLOCKS = XHEAD_DIM // LANE
CACHE_ROW_PITCH = N_XHEADS * HEAD_LANE_BLOCKS


def _cache_rows_view(cache):
    S = cache.shape[0]
    c5 = cache.reshape(S, N_MEM, N_XHEADS, HEAD_LANE_BLOCKS, LANE)
    return c5.transpose(0, 1, 3, 2, 4).reshape(S, N_MEM * CACHE_ROW_PITCH, LANE)


def _cache_rows_unview(rows, nseq):
    r5 = rows.reshape(nseq, N_MEM, HEAD_LANE_BLOCKS, N_XHEADS, LANE)
    return r5.transpose(0, 1, 3, 2, 4).reshape(nseq, N_MEM, N_XHEADS, XHEAD_DIM)


def _mem_kv_kernel(m_ref, g_ref, w_ref, rows_ref, kvb_ref):
    tm = m_ref.shape[0]
    kv = _dot(_rms(m_ref[...], g_ref[...]).astype(BF16), w_ref[...].astype(BF16))
    kvb_ref[...] = kv.astype(BF16)
    for c in range(HEAD_LANE_BLOCKS):
        for h in range(N_XHEADS):
            lo = h * XHEAD_DIM + c * LANE
            rows_ref[pl.ds(c * N_XHEADS + h, tm, stride=CACHE_ROW_PITCH), :] = kv[:, lo:lo + LANE]


def _mem_kv(mem, g_mem, w, tm):
    rows = mem.shape[0]
    return pl.pallas_call(
        _mem_kv_kernel,
        grid=(rows // tm,),
        in_specs=[pl.BlockSpec((tm, D_MODEL), lambda i: (i, 0)),
                  _const_spec((1, D_MODEL)),
                  _const_spec((D_MODEL, D_MODEL))],
        out_specs=[pl.BlockSpec((tm * CACHE_ROW_PITCH, LANE), lambda i: (i, 0)),
                   pl.BlockSpec((tm, D_MODEL), lambda i: (i, 0))],
        out_shape=[jax.ShapeDtypeStruct((rows * CACHE_ROW_PITCH, LANE), F32),
                   jax.ShapeDtypeStruct((rows, D_MODEL), BF16)],
        compiler_params=_params(("arbitrary",), 48),
        name="mem_kv",
    )(mem, g_mem, w)


_NT = (((1,), (1,)), ((), ()))


def _softmax_rows(s):
    e = jnp.exp(s - jnp.max(s, axis=-1, keepdims=True))
    return e / jnp.sum(e, axis=-1, keepdims=True)


def _attn_prompt_kernel(q_ref, k_ref, v_ref, o_ref):
    for h in range(N_XHEADS):
        sl = slice(h * XHEAD_DIM, (h + 1) * XHEAD_DIM)
        s = lax.dot_general(q_ref[:, sl], k_ref[:, sl], _NT, preferred_element_type=F32)
        p = _softmax_rows(s * (XHEAD_DIM ** -0.5))
        o_ref[:, sl] = _dot(p.astype(BF16), v_ref[:, sl]).astype(BF16)


def _attn_prompt(q, k_bf, v_bf, tq, rows_per_seq):
    rows = q.shape[0]
    tiles = rows_per_seq // tq
    kv_spec = pl.BlockSpec((N_MEM, D_MODEL), lambda i: (i // tiles, 0))
    return pl.pallas_call(
        _attn_prompt_kernel,
        grid=(rows // tq,),
        in_specs=[pl.BlockSpec((tq, D_MODEL), lambda i: (i, 0)), kv_spec, kv_spec],
        out_specs=pl.BlockSpec((tq, D_MODEL), lambda i: (i, 0)),
        out_shape=jax.ShapeDtypeStruct((rows, D_MODEL), BF16),
        compiler_params=_params(("arbitrary",), 40),
        name="attn_prompt",
    )(q, k_bf, v_bf)


def _attn_cache_kernel(q_ref, k_ref, v_ref, o_ref, *, bb):
    nc = HEAD_LANE_BLOCKS
    mc = N_MEM * nc
    lane_c = lax.broadcasted_iota(jnp.int32, (8, mc), 1) & (nc - 1)
    for b in range(bb):
        q8 = q_ref[b]
        xv = jnp.concatenate([v_ref[b, pl.ds(h, mc, stride=N_XHEADS), :] for h in range(N_XHEADS)],
                             axis=1).astype(BF16)
        pm_rows = []
        for h in range(N_XHEADS):
            xk = k_ref[b, pl.ds(h, mc, stride=N_XHEADS), :].astype(BF16)
            qc = jnp.concatenate([q8[:, h * XHEAD_DIM + c * LANE:h * XHEAD_DIM + (c + 1) * LANE]
                                  for c in range(nc)], axis=0).astype(BF16)
            part = lax.dot_general(qc, xk, _NT, preferred_element_type=F32)
            own = part[8 * (nc - 1):]
            for c in range(nc - 2, -1, -1):
                own = jnp.where(lane_c == c, part[8 * c:8 * (c + 1)], own)
            pair = own + pltpu.roll(own, 1, 1)
            full = pair + pltpu.roll(pair, 2, 1)
            s = jnp.where(lane_c == nc - 1, full * (XHEAD_DIM ** -0.5), -1e30)
            e = jnp.exp(s - jnp.max(s, axis=1, keepdims=True))
            p = e / jnp.sum(e, axis=1, keepdims=True)
            p2 = p + pltpu.roll(p, mc - 1, 1)
            p4 = p2 + pltpu.roll(p2, mc - 2, 1)
            pm_rows += [jnp.where(lane_c == c, p4, 0.0) for c in range(nc)]
        o_all = _dot(jnp.concatenate(pm_rows, axis=0).astype(BF16), xv)
        for h in range(N_XHEADS):
            for c in range(nc):
                g = h * nc + c
                o_ref[b, :, h * XHEAD_DIM + c * LANE:h * XHEAD_DIM + (c + 1) * LANE] = (
                    o_all[8 * g:8 * (g + 1), h * LANE:(h + 1) * LANE])


def _attn_cache(q8, k_rows, v_rows, bb):
    nseq = q8.shape[0]
    rows = N_MEM * CACHE_ROW_PITCH
    return pl.pallas_call(
        functools.partial(_attn_cache_kernel, bb=bb),
        grid=(nseq // bb,),
        in_specs=[pl.BlockSpec((bb, 8, D_MODEL), lambda i: (i, 0, 0)),
                  pl.BlockSpec((bb, rows, LANE), lambda i: (i, 0, 0)),
                  pl.BlockSpec((bb, rows, LANE), lambda i: (i, 0, 0))],
        out_specs=pl.BlockSpec((bb, 8, D_MODEL), lambda i: (i, 0, 0)),
        out_shape=jax.ShapeDtypeStruct((nseq, 8, D_MODEL), F32),
        compiler_params=_params(("arbitrary",), 48),
        name="attn_cache",
    )(q8, k_rows, v_rows)


def _oproj_kernel(o_ref, x1_ref, wo_ref, x2_ref):
    x2_ref[...] = x1_ref[...] + _dot(o_ref[...], wo_ref[...])


def _oproj(o, x1, w_o, tm):
    rows = x1.shape[0]
    return pl.pallas_call(
        _oproj_kernel,
        grid=(rows // tm,),
        in_specs=[pl.BlockSpec((tm, D_MODEL), lambda i: (i, 0)),
                  pl.BlockSpec((tm, D_MODEL), lambda i: (i, 0)),
                  _const_spec((D_MODEL, D_MODEL))],
        out_specs=pl.BlockSpec((tm, D_MODEL), lambda i: (i, 0)),
        out_shape=jax.ShapeDtypeStruct((rows, D_MODEL), F32),
        compiler_params=_params(("arbitrary",), 40),
        name="oproj",
    )(o, x1, w_o)


def _ffn_kernel(x_ref, gf_ref, wg_ref, wu_ref, wd_ref, gl_ref, y_ref, h_scr):
    f = pl.program_id(1)

    @pl.when(f == 0)
    def _():
        h_scr[...] = _rms(x_ref[...], gf_ref[...]).astype(BF16)
        y_ref[...] = jnp.zeros_like(y_ref)

    h = h_scr[...]
    z = (jax.nn.silu(_dot(h, wg_ref[...].astype(BF16))) * _dot(h, wu_ref[...].astype(BF16))).astype(BF16)
    y_ref[...] += _dot(z, wd_ref[...].astype(BF16))

    @pl.when(f == pl.num_programs(1) - 1)
    def _():
        y_ref[...] = _rms(x_ref[...] + y_ref[...], gl_ref[...])


def _ffn(x2, g_ffn, w_gate, w_up, w_down, g_final, tm, tf):
    rows = x2.shape[0]
    return pl.pallas_call(
        _ffn_kernel,
        grid=(rows // tm, D_FF // tf),
        in_specs=[pl.BlockSpec((tm, D_MODEL), lambda i, f: (i, 0)),
                  _const_spec((1, D_MODEL)),
                  pl.BlockSpec((D_MODEL, tf), lambda i, f: (0, f)),
                  pl.BlockSpec((D_MODEL, tf), lambda i, f: (0, f)),
                  pl.BlockSpec((tf, D_MODEL), lambda i, f: (f, 0)),
                  _const_spec((1, D_MODEL))],
        out_specs=pl.BlockSpec((tm, D_MODEL), lambda i, f: (i, 0), pipeline_mode=pl.Buffered(1)),
        out_shape=jax.ShapeDtypeStruct((rows, D_MODEL), F32),
        scratch_shapes=[pltpu.VMEM((tm, D_MODEL), BF16)],
        compiler_params=_params(("arbitrary", "arbitrary"), 56),
        name="ffn",
    )(x2, g_ffn, w_gate, w_up, w_down, g_final)


def _ssm_param_layouts(lam_re, lam_im, log_step, b_re, b_im, c_re, c_im):
    G, P = lam_re.shape
    ls = jnp.broadcast_to(log_step[:, None], (G, P))
    rep = lambda a: jnp.repeat(a, SSM_GROUP, axis=0)
    b_rows = lambda b: jnp.swapaxes(b, 1, 2).reshape(G * SSM_GROUP, P)
    row5 = jnp.stack([rep(lam_re), rep(lam_im), rep(ls), b_rows(b_re), b_rows(b_im)])
    col5 = jnp.stack([rep(lam_re).T, rep(lam_im).T, rep(ls).T,
                      c_re.reshape(G * SSM_GROUP, P).T, c_im.reshape(G * SSM_GROUP, P).T])
    flat = lambda a: a.reshape(N_LANE_BLOCKS, 1, STATE_BLOCK)
    flat3 = jnp.stack([flat(lam_re), flat(lam_im), flat(ls)])
    return row5, col5, flat3


def _states_to_blocks(h_re, h_im):
    S = h_re.shape[0]
    blk = lambda h: h.reshape(S, N_LANE_BLOCKS, STATE_BLOCK).transpose(1, 0, 2)
    return jnp.concatenate([blk(h_re), blk(h_im)], axis=-1)


def _blocks_to_states(st):
    S = st.shape[1]
    unblk = lambda a: a.transpose(1, 0, 2).reshape(1, S, N_SSM_GROUPS, SSM_STATE)
    return unblk(st[:, :, :STATE_BLOCK]), unblk(st[:, :, STATE_BLOCK:])


def kernel(x_prompt, x_sample, mem_prompt, state_pool_buf, state_ssm_re, state_ssm_im, cache_mem_k, cache_mem_v, g_mix, w_in, w_pool, pool_scale, ssm_lam_re, ssm_lam_im, ssm_log_step, ssm_b_re, ssm_b_im, ssm_c_re, ssm_c_im, ssm_d, w_glu, b_glu, w_out, g_cross, g_mem, w_q, w_k, w_v, w_o, g_ffn, w_gate, w_up, w_down, g_final):
    assert g_mix.shape[0] == 1, "single-layer step"
    B, T, _ = x_prompt.shape
    S, Ts, _ = x_sample.shape
    Lp, Ls = 8, Ts
    bf = lambda w: w.astype(BF16)
    vec = lambda v: v.reshape(1, -1)

    w_in_b, w_pool_b, w_glu_b, w_out_b = bf(w_in[0]), bf(w_pool[0]), bf(w_glu[0]), bf(w_out[0])
    w_q_b, w_o_b = bf(w_q[0]), bf(w_o[0])

    row5, col5, flat3 = _ssm_param_layouts(ssm_lam_re[0], ssm_lam_im[0], ssm_log_step[0],
                                           ssm_b_re[0], ssm_b_im[0], ssm_c_re[0], ssm_c_im[0])
    d_blocks = ssm_d[0].reshape(N_LANE_BLOCKS, 1, LANE)

    def layer(x, nseq, t_len, L, pool_buf, h0_blocks, attend, tm, ffn_tiles):
        rows = nseq * t_len
        u_pool, u_ssm = _in_proj(x, vec(g_mix[0]), w_in_b, tm)

        if pool_buf is None:
            pool_out = _pool_mix_seq(u_pool, w_pool_b, vec(pool_scale[0]), tc=512, rows_per_seq=t_len)
            new_buf = u_pool.reshape(nseq, t_len, POOL_WIDTH)[:, t_len - POOL_BUF:]
        else:
            assert t_len == L and PAST_LEN >= POOL_BUF
            pool_out, new_buf = _pool_mix_buf(u_pool.reshape(nseq, L * POOL_WIDTH),
                                              pool_buf.reshape(nseq, POOL_BUF * POOL_WIDTH), w_pool_b,
                                              vec(pool_scale[0]), L=L)
            new_buf = new_buf.reshape(nseq, POOL_BUF, POOL_WIDTH)

        ops = _ssm_prep(row5, col5, flat3, L)
        d_tiles = jnp.tile(d_blocks, (1, 1, L))
        if t_len > L:
            u_view = u_ssm.reshape(N_LANE_BLOCKS, nseq, t_len, LANE)
            g_act, st = _ssm_mix(u_view, ops, d_tiles, h0_blocks, L=L, Rn=128, seq=True)
        else:
            u_view = u_ssm.reshape(N_LANE_BLOCKS, 1, rows, LANE)
            g_act, st = _ssm_mix(u_view, ops, d_tiles, h0_blocks, L=L, Rn=nseq, seq=False)
        new_re, new_im = _blocks_to_states(st)

        x1, q = _mix_out(g_act.reshape(N_LANE_BLOCKS, rows, LANE), pool_out.reshape(rows, POOL_WIDTH), x,
                         w_glu_b, vec(b_glu[0]), w_out_b, vec(g_cross[0]), w_q_b, 512)
        o = attend(q)
        x2 = _oproj(o, x1, w_o_b, tm)
        y = _ffn(x2, vec(g_ffn[0]), w_gate[0], w_up[0], w_down[0], vec(g_final), *ffn_tiles)
        return y, new_buf, new_re, new_im

    mem = mem_prompt.reshape(B * N_MEM, D_MODEL)
    k_rows, k_b = _mem_kv(mem, vec(g_mem[0]), w_k[0], 512)
    v_rows, v_b = _mem_kv(mem, vec(g_mem[0]), w_v[0], 512)
    h0_p = jnp.zeros((N_LANE_BLOCKS, B, 2 * STATE_BLOCK), F32)
    yp, pb_p, re_p, im_p = layer(x_prompt.reshape(B * T, D_MODEL), B, T, Lp, None, h0_p,
                                 lambda q: _attn_prompt(q, k_b, v_b, 512, T), 512, (1024, 256))

    def attend_cache(q):
        q8 = jnp.pad(q.reshape(S, Ts, D_MODEL).astype(F32), ((0, 0), (0, 8 - Ts), (0, 0)))
        o8 = _attn_cache(q8, _cache_rows_view(cache_mem_k[0]), _cache_rows_view(cache_mem_v[0]), 2)
        return o8[:, :Ts].reshape(S * Ts, D_MODEL).astype(BF16)

    h0_s = _states_to_blocks(state_ssm_re[0], state_ssm_im[0])
    ys, pb_s, re_s, im_s = layer(x_sample.reshape(S * Ts, D_MODEL), S, Ts, Ls, state_pool_buf[0], h0_s,
                                 attend_cache, 512, (512, 512))

    mk = _cache_rows_unview(k_rows, B)[None]
    mv = _cache_rows_unview(v_rows, B)[None]
    return (yp.reshape(B, T, D_MODEL), ys.reshape(S, Ts, D_MODEL), pb_p[None], re_p, im_p, mk, mv,
            pb_s[None], re_s, im_s)
```

```python
import functools

import jax
import jax.numpy as jnp
from jax import lax
from jax.experimental import pallas as pl
from jax.experimental.pallas import tpu as pltpu

F32 = jnp.float32
BF16 = jnp.bfloat16

D_MODEL = 2048
POOL_WIDTH = 1024
SSM_WIDTH = 1024
POOL_WINDOWS = (2, 4, 8, 16)
POOL_GROUP = POOL_WIDTH // len(POOL_WINDOWS)
POOL_BUF = max(POOL_WINDOWS) - 1
SSM_GROUP = 16
N_SSM_GROUPS = SSM_WIDTH // SSM_GROUP
SSM_STATE = 64
N_MEM = 256
N_XHEADS = 4
XHEAD_DIM = D_MODEL // N_XHEADS
D_FF = 5632
EPS = 1e-6
PAST_LEN = 16384

LANE = 128
N_LANE_BLOCKS = SSM_WIDTH // LANE
GROUPS_PER_BLOCK = LANE // SSM_GROUP
STATE_BLOCK = GROUPS_PER_BLOCK * SSM_STATE
MIB = 1024 * 1024
HI = lax.Precision.HIGHEST


def _params(semantics, vmem_mib):
    return pltpu.CompilerParams(dimension_semantics=semantics, vmem_limit_bytes=vmem_mib * MIB)


def _const_spec(shape):
    return pl.BlockSpec(shape, lambda *_: (0,) * len(shape), pipeline_mode=pl.Buffered(1))


def _rms(x, g):
    r = lax.rsqrt(jnp.mean(x * x, axis=-1, keepdims=True) + EPS)
    return x * r * g


def _dot(a, b):
    return jnp.dot(a, b, preferred_element_type=F32)


MIX_LANE_BLOCKS = D_MODEL // LANE


def _in_proj_kernel(x_ref, g_ref, w_ref, u_ref):
    h = _rms(x_ref[...], g_ref[...]).astype(BF16)
    u = _dot(h, w_ref[...])
    for j in range(MIX_LANE_BLOCKS):
        u_ref[j] = u[:, j * LANE:(j + 1) * LANE]


def _in_proj(x, g, w, tm):
    rows = x.shape[0]
    return pl.pallas_call(
        _in_proj_kernel,
        grid=(rows // tm,),
        in_specs=[pl.BlockSpec((tm, D_MODEL), lambda i: (i, 0)),
                  _const_spec((1, D_MODEL)),
                  _const_spec((D_MODEL, D_MODEL))],
        out_specs=pl.BlockSpec((MIX_LANE_BLOCKS, tm, LANE), lambda i: (0, i, 0)),
        out_shape=jax.ShapeDtypeStruct((MIX_LANE_BLOCKS, rows, LANE), F32),
        compiler_params=_params(("arbitrary",), 40),
        name="in_proj",
    )(x, g, w)


def _lane_blocks(ref, rows=slice(None)):
    return jnp.concatenate([ref[j, rows, :] for j in range(ref.shape[0])], axis=1)


POOL_HIST = POOL_BUF + 1


def _pool_seq_kernel(u_ref, wp_ref, sc_ref, o_ref, ext_ref, *, tc, tiles_per_seq):
    it = lax.rem(pl.program_id(0), tiles_per_seq)

    @pl.when(it == 0)
    def _():
        ext_ref[:POOL_HIST, :] = jnp.zeros((POOL_HIST, POOL_WIDTH), F32)

    ext_ref[POOL_HIST:, :] = _lane_blocks(u_ref)
    pos = it * tc + lax.broadcasted_iota(jnp.int32, (tc, 1), 0)
    for g, w in enumerate(POOL_WINDOWS):
        sl = slice(g * POOL_GROUP, (g + 1) * POOL_GROUP)
        z = ext_ref[:, sl]
        s, span = z, 1
        while span < w:
            s = s + pltpu.roll(s, span, 0)
            span *= 2
        cnt = jnp.minimum(pos + 1, w).astype(F32)
        pooled = (s[POOL_HIST:] / cnt - z[POOL_HIST:]).astype(BF16)
        out = _dot(pooled, wp_ref[g]) * sc_ref[:, sl]
        for jj in range(POOL_GROUP // LANE):
            o_ref[g * (POOL_GROUP // LANE) + jj] = out[:, jj * LANE:(jj + 1) * LANE]
    ext_ref[:POOL_HIST, :] = ext_ref[tc:tc + POOL_HIST, :]


POOL_LANE_BLOCKS = POOL_WIDTH // LANE


def _pool_mix_seq(u_blocks, w_pool, scale, *, tc, rows_per_seq):
    rows = u_blocks.shape[1]
    slab_spec = pl.BlockSpec((POOL_LANE_BLOCKS, tc, LANE), lambda i: (0, i, 0))
    return pl.pallas_call(
        functools.partial(_pool_seq_kernel, tc=tc, tiles_per_seq=rows_per_seq // tc),
        grid=(rows // tc,),
        in_specs=[slab_spec, _const_spec((len(POOL_WINDOWS), POOL_GROUP, POOL_GROUP)),
                  _const_spec((1, POOL_WIDTH))],
        out_specs=slab_spec,
        out_shape=jax.ShapeDtypeStruct((POOL_LANE_BLOCKS, rows, LANE), F32),
        scratch_shapes=[pltpu.VMEM((tc + POOL_HIST, POOL_WIDTH), F32)],
        compiler_params=_params(("arbitrary",), 40),
        name="pool_mix_seq",
    )(u_blocks, w_pool, scale)


def _pool_buf_kernel(u_ref, buf_ref, wp_ref, sc_ref, o_ref, nb_ref, *, L):
    nseq = buf_ref.shape[1]
    per_group = POOL_GROUP // LANE

    def token_rows(t):
        return pl.ds(t, nseq, stride=L)

    def slab(idx, g):
        if idx < POOL_BUF:
            return buf_ref[idx, :, g * POOL_GROUP:(g + 1) * POOL_GROUP]
        return jnp.concatenate([u_ref[g * per_group + jj, token_rows(idx - POOL_BUF), :]
                                for jj in range(per_group)], axis=1)

    for g, w in enumerate(POOL_WINDOWS):
        sl = slice(g * POOL_GROUP, (g + 1) * POOL_GROUP)
        pooled = []
        for t in range(L):
            acc = slab(POOL_BUF + t, g)
            for back in range(1, w):
                acc = acc + slab(POOL_BUF + t - back, g)
            pooled.append((acc / float(w) - slab(POOL_BUF + t, g)).astype(BF16))
        out = _dot(jnp.concatenate(pooled, axis=0), wp_ref[g]) * sc_ref[:, sl]
        for t in range(L):
            for jj in range(per_group):
                o_ref[g * per_group + jj, token_rows(t), :] = out[t * nseq:(t + 1) * nseq,
                                                                  jj * LANE:(jj + 1) * LANE]
    keep = POOL_BUF - L
    for k in range(keep):
        nb_ref[k] = buf_ref[k + L]
    for t in range(L):
        nb_ref[keep + t] = _lane_blocks(u_ref, token_rows(t))


def _pool_mix_buf(u_blocks, buf_tm, w_pool, scale, *, L):
    rows = u_blocks.shape[1]
    slab_spec = pl.BlockSpec((POOL_LANE_BLOCKS, rows, LANE), lambda i: (0, 0, 0))
    buf_spec = pl.BlockSpec(buf_tm.shape, lambda i: (0, 0, 0))
    return pl.pallas_call(
        functools.partial(_pool_buf_kernel, L=L),
        grid=(1,),
        in_specs=[slab_spec, buf_spec, _const_spec((len(POOL_WINDOWS), POOL_GROUP, POOL_GROUP)),
                  _const_spec((1, POOL_WIDTH))],
        out_specs=[slab_spec, buf_spec],
        out_shape=[jax.ShapeDtypeStruct((POOL_LANE_BLOCKS, rows, LANE), F32),
                   jax.ShapeDtypeStruct(buf_tm.shape, F32)],
        compiler_params=_params(("arbitrary",), 48),
        name="pool_mix_buf",
    )(u_blocks, buf_tm, w_pool, scale)


def _cmul(ar, ai, br, bi):
    return ar * br - ai * bi, ar * bi + ai * br


def _cexp(lam_re, lam_im, log_step):
    delta = jnp.exp(log_step)
    mag = jnp.exp(lam_re * delta)
    ang = lam_im * delta
    return mag * jnp.cos(ang), mag * jnp.sin(ang)


def _ssm_prep_kernel(row_ref, col_ref, flat_ref, t_ref, w_ref, v_ref, al_ref, *, L):
    lr, li, ls, br, bi = (row_ref[i] for i in range(5))
    ar, ai = _cexp(lr, li, ls)
    den = lr * lr + li * li
    xr = ar - 1.0
    fr = (xr * lr + ai * li) / den
    fi = (ai * lr - xr * li) / den
    zs = [_cmul(fr, fi, br, bi)]
    for _ in range(1, L):
        zs.append(_cmul(ar, ai, *zs[-1]))

    lrc, lic, lsc, cr, ci = (col_ref[i] for i in range(5))
    acr, aci = _cexp(lrc, lic, lsc)
    xs = []
    cur = (cr, ci)
    for _ in range(L):
        cur = _cmul(acr, aci, *cur)
        xs.append(cur)

    def iota(shape, dim):
        return lax.broadcasted_iota(jnp.int32, shape, dim)

    same_tt = (iota((LANE, LANE), 0) >> 4) == (iota((LANE, LANE), 1) >> 4)
    c_stack = jnp.concatenate([cr, -ci], axis=0)
    zero_tile = jnp.zeros((LANE, LANE), BF16)
    lag = []
    for d in range(L):
        z_stack = jnp.concatenate([zs[d][0], zs[d][1]], axis=1)
        tile = jnp.dot(z_stack, c_stack, precision=HI, preferred_element_type=F32)
        lag.append(jnp.where(same_tt, tile, 0.0).astype(BF16))
    for k in range(L):
        for t in range(L):
            t_ref[0, k * LANE:(k + 1) * LANE, t * LANE:(t + 1) * LANE] = lag[t - k] if t >= k else zero_tile

    same_w = (iota((LANE, STATE_BLOCK), 0) >> 4) == (iota((LANE, STATE_BLOCK), 1) >> 6)
    for k in range(L):
        zr, zi = zs[L - 1 - k]
        wr = jnp.where(same_w, jnp.concatenate([zr] * GROUPS_PER_BLOCK, axis=1), 0.0)
        wi = jnp.where(same_w, jnp.concatenate([zi] * GROUPS_PER_BLOCK, axis=1), 0.0)
        w_ref[0, k * LANE:(k + 1) * LANE, :] = jnp.concatenate([wr, wi], axis=1).astype(BF16)

    same_v = (iota((STATE_BLOCK, LANE), 0) >> 6) == (iota((STATE_BLOCK, LANE), 1) >> 4)
    for t in range(L):
        xr_t, xi_t = xs[t]
        vr = jnp.where(same_v, jnp.concatenate([xr_t] * GROUPS_PER_BLOCK, axis=0), 0.0)
        vi = jnp.where(same_v, jnp.concatenate([xi_t] * GROUPS_PER_BLOCK, axis=0), 0.0)
        v_ref[0, :, t * LANE:(t + 1) * LANE] = jnp.concatenate([vr, -vi], axis=0).astype(BF16)

    flr, fli, fls = (flat_ref[i, 0] for i in range(3))
    far, fai = _cexp(flr, fli, fls)
    pr, pi = far, fai
    for _ in range(L - 1):
        pr, pi = _cmul(far, fai, pr, pi)
    al_ref[0] = jnp.concatenate([pr, pi], axis=1)


def _ssm_prep(row5, col5, flat3, L):
    lk = L * LANE
    return pl.pallas_call(
        functools.partial(_ssm_prep_kernel, L=L),
        grid=(N_LANE_BLOCKS,),
        in_specs=[pl.BlockSpec((5, LANE, SSM_STATE), lambda j: (0, j, 0)),
                  pl.BlockSpec((5, SSM_STATE, LANE), lambda j: (0, 0, j)),
                  pl.BlockSpec((3, 1, 1, STATE_BLOCK), lambda j: (0, j, 0, 0))],
        out_specs=[pl.BlockSpec((1, lk, lk), lambda j: (j, 0, 0)),
                   pl.BlockSpec((1, lk, 2 * STATE_BLOCK), lambda j: (j, 0, 0)),
                   pl.BlockSpec((1, 2 * STATE_BLOCK, lk), lambda j: (j, 0, 0)),
                   pl.BlockSpec((1, 1, 2 * STATE_BLOCK), lambda j: (j, 0, 0))],
        out_shape=[jax.ShapeDtypeStruct((N_LANE_BLOCKS, lk, lk), BF16),
                   jax.ShapeDtypeStruct((N_LANE_BLOCKS, lk, 2 * STATE_BLOCK), BF16),
                   jax.ShapeDtypeStruct((N_LANE_BLOCKS, 2 * STATE_BLOCK, lk), BF16),
                   jax.ShapeDtypeStruct((N_LANE_BLOCKS, 1, 2 * STATE_BLOCK), F32)],
        compiler_params=_params(("arbitrary",), 40),
        name=f"ssm_prep_L{L}",
    )(row5, col5, flat3)


def _ssm_kernel(u_ref, t_ref, w_ref, v_ref, al_ref, d_ref, h0_ref, g_ref, st_ref, *scratch, B, Rn, L, seq):
    M = B * Rn
    sb = STATE_BLOCK
    io_scr = scratch[0]
    for b in range(B):
        for k in range(L):
            io_scr[k, pl.ds(b, Rn, stride=B), :] = u_ref[0, b, pl.ds(k, Rn, stride=L), :]
    u = jnp.concatenate([io_scr[k] for k in range(L)], axis=1)
    ub = u.astype(BF16)
    e = _dot(ub, w_ref[0])
    ar = al_ref[0][:, :sb]
    ai = al_ref[0][:, sb:]

    def advance(s, eb):
        sr, si = s[:, :sb], s[:, sb:]
        return jnp.concatenate([ar * sr - ai * si + eb[:, :sb], ar * si + ai * sr + eb[:, sb:]], axis=1)

    if seq:
        e_scr, s_scr, c_scr = scratch[1:]

        assert 2 * B == 8 and Rn % 2 == 0, "two chunk rows of B sequences fill one sublane tile"

        @pl.when(pl.program_id(1) == 0)
        def _():
            c_scr[:B, :] = h0_ref[0]
            c_scr[B:, :] = jnp.zeros((B, 2 * sb), F32)

        e_scr[...] = e
        top = lax.broadcasted_iota(jnp.int32, (2 * B, 1), 0) < B

        def body(i, s):
            rows = pl.ds(pl.multiple_of(i * 2 * B, 2 * B), 2 * B)
            e2 = e_scr[rows, :]
            mid = pltpu.roll(advance(s, e2), B, 0)
            s_scr[rows, :] = jnp.where(top, s, mid)
            return pltpu.roll(advance(mid, e2), B, 0)

        c_scr[...] = lax.fori_loop(0, Rn // 2, body, c_scr[...])
        st_ref[0] = c_scr[:B, :]
        s_start = s_scr[...]
    else:
        s_start = h0_ref[0]
        st_ref[0] = advance(s_start, e)
    y = _dot(ub, t_ref[0]) + _dot(s_start.astype(BF16), v_ref[0]) + d_ref[0] * u
    g = jax.nn.gelu(y)
    for k in range(L):
        io_scr[k] = g[:, k * LANE:(k + 1) * LANE]
    for b in range(B):
        for k in range(L):
            g_ref[0, b, pl.ds(k, Rn, stride=L), :] = io_scr[k, pl.ds(b, Rn, stride=B), :]


def _ssm_mix(u_view, ops, d_tiles, h0, *, L, Rn, seq):
    t_op, w_op, v_op, a_l = ops
    nj = N_LANE_BLOCKS
    _, B, trows, _ = u_view.shape
    lk = L * LANE
    nseq = h0.shape[1]
    tiles = trows // (Rn * L)
    M = B * Rn
    sb2 = 2 * STATE_BLOCK
    first = u_view.shape[0] - nj
    u_spec = pl.BlockSpec((1, B, Rn * L, LANE), lambda j, i: (j + first, 0, i, 0))
    g_spec = pl.BlockSpec((1, B, Rn * L, LANE), lambda j, i: (j, 0, i, 0))
    scratch = [pltpu.VMEM((L, M, LANE), F32)]
    if seq:
        scratch += [pltpu.VMEM((M, sb2), F32), pltpu.VMEM((M, sb2), F32), pltpu.VMEM((2 * B, sb2), F32)]
    return pl.pallas_call(
        functools.partial(_ssm_kernel, B=B, Rn=Rn, L=L, seq=seq),
        grid=(nj, tiles),
        in_specs=[u_spec,
                  pl.BlockSpec((1, lk, lk), lambda j, i: (j, 0, 0)),
                  pl.BlockSpec((1, lk, sb2), lambda j, i: (j, 0, 0)),
                  pl.BlockSpec((1, sb2, lk), lambda j, i: (j, 0, 0)),
                  pl.BlockSpec((1, 1, sb2), lambda j, i: (j, 0, 0)),
                  pl.BlockSpec((1, 1, lk), lambda j, i: (j, 0, 0)),
                  pl.BlockSpec((1, nseq, sb2), lambda j, i: (j, 0, 0))],
        out_specs=[g_spec, pl.BlockSpec((1, nseq, sb2), lambda j, i: (j, 0, 0))],
        out_shape=[jax.ShapeDtypeStruct((nj,) + u_view.shape[1:], F32),
                   jax.ShapeDtypeStruct((nj, nseq, sb2), F32)],
        scratch_shapes=scratch,
        compiler_params=_params(("arbitrary", "arbitrary"), 48),
        name="ssm_mix_seq" if seq else "ssm_mix_rows",
    )(u_view, t_op, w_op, v_op, a_l, d_tiles, h0)


def _mix_out_kernel(g_ref, po_ref, x_ref, wglu_ref, bglu_ref, wout_ref, gc_ref, wq_ref, x1_ref, q_ref):
    g = _lane_blocks(g_ref)
    gate = jax.nn.sigmoid(_dot(g.astype(BF16), wglu_ref[...]) + bglu_ref[...])
    ssm_out = (g * gate).astype(BF16)
    mix = (_dot(_lane_blocks(po_ref).astype(BF16), wout_ref[:POOL_WIDTH, :])
           + _dot(ssm_out, wout_ref[POOL_WIDTH:, :]))
    x1 = x_ref[...] + mix
    x1_ref[...] = x1
    q_ref[...] = _dot(_rms(x1, gc_ref[...]).astype(BF16), wq_ref[...]).astype(BF16)


def _mix_out(g_rows, pool_out, x, w_glu, b_glu, w_out, g_cross, w_q, tm):
    rows = x.shape[0]
    return pl.pallas_call(
        _mix_out_kernel,
        grid=(rows // tm,),
        in_specs=[pl.BlockSpec((N_LANE_BLOCKS, tm, LANE), lambda i: (0, i, 0)),
                  pl.BlockSpec((POOL_LANE_BLOCKS, tm, LANE), lambda i: (0, i, 0)),
                  pl.BlockSpec((tm, D_MODEL), lambda i: (i, 0)),
                  _const_spec((SSM_WIDTH, SSM_WIDTH)), _const_spec((1, SSM_WIDTH)),
                  _const_spec((D_MODEL, D_MODEL)), _const_spec((1, D_MODEL)),
                  _const_spec((D_MODEL, D_MODEL))],
        out_specs=[pl.BlockSpec((tm, D_MODEL), lambda i: (i, 0)),
                   pl.BlockSpec((tm, D_MODEL), lambda i: (i, 0))],
        out_shape=[jax.ShapeDtypeStruct((rows, D_MODEL), F32),
                   jax.ShapeDtypeStruct((rows, D_MODEL), BF16)],
        compiler_params=_params(("arbitrary",), 52),
        name="mix_out",
    )(g_rows, pool_out, x, w_glu, b_glu, w_out, g_cross, w_q)


HEAD_LANE_BLOCKS = XHEAD_DIM // LANE
CACHE_ROW_PITCH = N_XHEADS * HEAD_LANE_BLOCKS


def _cache_rows_view(cache):
    S = cache.shape[0]
    c5 = cache.reshape(S, N_MEM, N_XHEADS, HEAD_LANE_BLOCKS, LANE)
    return c5.transpose(0, 1, 3, 2, 4).reshape(S, N_MEM * CACHE_ROW_PITCH, LANE)


def _cache_rows_unview(rows, nseq):
    r5 = rows.reshape(nseq, N_MEM, HEAD_LANE_BLOCKS, N_XHEADS, LANE)
    return r5.transpose(0, 1, 3, 2, 4).reshape(nseq, N_MEM, N_XHEADS, XHEAD_DIM)


def _mem_kv_kernel(m_ref, g_ref, w_ref, rows_ref, kvb_ref):
    tm = m_ref.shape[0]
    kv = _dot(_rms(m_ref[...], g_ref[...]).astype(BF16), w_ref[...].astype(BF16))
    kvb_ref[...] = kv.astype(BF16)
    for c in range(HEAD_LANE_BLOCKS):
        for h in range(N_XHEADS):
            lo = h * XHEAD_DIM + c * LANE
            rows_ref[pl.ds(c * N_XHEADS + h, tm, stride=CACHE_ROW_PITCH), :] = kv[:, lo:lo + LANE]


def _mem_kv(mem, g_mem, w, tm):
    rows = mem.shape[0]
    return pl.pallas_call(
        _mem_kv_kernel,
        grid=(rows // tm,),
        in_specs=[pl.BlockSpec((tm, D_MODEL), lambda i: (i, 0)),
                  _const_spec((1, D_MODEL)),
                  _const_spec((D_MODEL, D_MODEL))],
        out_specs=[pl.BlockSpec((tm * CACHE_ROW_PITCH, LANE), lambda i: (i, 0)),
                   pl.BlockSpec((tm, D_MODEL), lambda i: (i, 0))],
        out_shape=[jax.ShapeDtypeStruct((rows * CACHE_ROW_PITCH, LANE), F32),
                   jax.ShapeDtypeStruct((rows, D_MODEL), BF16)],
        compiler_params=_params(("arbitrary",), 48),
        name="mem_kv",
    )(mem, g_mem, w)


_NT = (((1,), (1,)), ((), ()))


def _softmax_rows(s):
    e = jnp.exp(s - jnp.max(s, axis=-1, keepdims=True))
    return e / jnp.sum(e, axis=-1, keepdims=True)


def _attn_prompt_kernel(q_ref, k_ref, v_ref, o_ref):
    for h in range(N_XHEADS):
        sl = slice(h * XHEAD_DIM, (h + 1) * XHEAD_DIM)
        s = lax.dot_general(q_ref[:, sl], k_ref[:, sl], _NT, preferred_element_type=F32)
        p = _softmax_rows(s * (XHEAD_DIM ** -0.5))
        o_ref[:, sl] = _dot(p.astype(BF16), v_ref[:, sl]).astype(BF16)


def _attn_prompt(q, k_bf, v_bf, tq, rows_per_seq):
    rows = q.shape[0]
    tiles = rows_per_seq // tq
    kv_spec = pl.BlockSpec((N_MEM, D_MODEL), lambda i: (i // tiles, 0))
    return pl.pallas_call(
        _attn_prompt_kernel,
        grid=(rows // tq,),
        in_specs=[pl.BlockSpec((tq, D_MODEL), lambda i: (i, 0)), kv_spec, kv_spec],
        out_specs=pl.BlockSpec((tq, D_MODEL), lambda i: (i, 0)),
        out_shape=jax.ShapeDtypeStruct((rows, D_MODEL), BF16),
        compiler_params=_params(("arbitrary",), 40),
        name="attn_prompt",
    )(q, k_bf, v_bf)


def _attn_cache_kernel(q_ref, k_ref, v_ref, o_ref, *, bb):
    nc = HEAD_LANE_BLOCKS
    mc = N_MEM * nc
    lane_c = lax.broadcasted_iota(jnp.int32, (8, mc), 1) & (nc - 1)
    owns = []
    for b in range(bb):
        q8 = q_ref[b]
        for h in range(N_XHEADS):
            xk = k_ref[b, pl.ds(h, mc, stride=N_XHEADS), :].astype(BF16)
            qc = jnp.concatenate([q8[:, h * XHEAD_DIM + c * LANE:h * XHEAD_DIM + (c + 1) * LANE]
                                  for c in range(nc)], axis=0).astype(BF16)
            part = lax.dot_general(qc, xk, _NT, preferred_element_type=F32)
            own = part[8 * (nc - 1):]
            for c in range(nc - 2, -1, -1):
                own = jnp.where(lane_c == c, part[8 * c:8 * (c + 1)], own)
            owns.append(own)
    own_all = jnp.concatenate(owns, axis=0)
    lane_all = lax.broadcasted_iota(jnp.int32, own_all.shape, 1) & (nc - 1)
    pair = own_all + pltpu.roll(own_all, 1, 1)
    full = pair + pltpu.roll(pair, 2, 1)
    s = jnp.where(lane_all == nc - 1, full * (XHEAD_DIM ** -0.5), -1e30)
    e = jnp.exp(s - jnp.max(s, axis=1, keepdims=True))
    p = e / jnp.sum(e, axis=1, keepdims=True)
    p2 = p + pltpu.roll(p, mc - 1, 1)
    p4 = p2 + pltpu.roll(p2, mc - 2, 1)
    for b in range(bb):
        xv = jnp.concatenate([v_ref[b, pl.ds(h, mc, stride=N_XHEADS), :] for h in range(N_XHEADS)],
                             axis=1).astype(BF16)
        pm_rows = []
        for h in range(N_XHEADS):
            r0 = 8 * (b * N_XHEADS + h)
            pm_rows += [jnp.where(lane_c == c, p4[r0:r0 + 8], 0.0) for c in range(nc)]
        o_all = _dot(jnp.concatenate(pm_rows, axis=0).astype(BF16), xv)
        for h in range(N_XHEADS):
            for c in range(nc):
                g = h * nc + c
                o_ref[b, :, h * XHEAD_DIM + c * LANE:h * XHEAD_DIM + (c + 1) * LANE] = (
                    o_all[8 * g:8 * (g + 1), h * LANE:(h + 1) * LANE])


def _attn_cache(q8, k_rows, v_rows, bb):
    nseq = q8.shape[0]
    rows = N_MEM * CACHE_ROW_PITCH
    return pl.pallas_call(
        functools.partial(_attn_cache_kernel, bb=bb),
        grid=(nseq // bb,),
        in_specs=[pl.BlockSpec((bb, 8, D_MODEL), lambda i: (i, 0, 0)),
                  pl.BlockSpec((bb, rows, LANE), lambda i: (i, 0, 0)),
                  pl.BlockSpec((bb, rows, LANE), lambda i: (i, 0, 0))],
        out_specs=pl.BlockSpec((bb, 8, D_MODEL), lambda i: (i, 0, 0)),
        out_shape=jax.ShapeDtypeStruct((nseq, 8, D_MODEL), F32),
        compiler_params=_params(("arbitrary",), 48),
        name="attn_cache",
    )(q8, k_rows, v_rows)


def _oproj_kernel(o_ref, x1_ref, wo_ref, gf_ref, x2_ref, h_ref):
    x2 = x1_ref[...] + _dot(o_ref[...], wo_ref[...])
    x2_ref[...] = x2
    h_ref[...] = _rms(x2, gf_ref[...]).astype(BF16)


def _oproj(o, x1, w_o, g_ffn, tm):
    rows = x1.shape[0]
    row_spec = pl.BlockSpec((tm, D_MODEL), lambda i: (i, 0))
    return pl.pallas_call(
        _oproj_kernel,
        grid=(rows // tm,),
        in_specs=[row_spec, row_spec, _const_spec((D_MODEL, D_MODEL)), _const_spec((1, D_MODEL))],
        out_specs=[row_spec, row_spec],
        out_shape=[jax.ShapeDtypeStruct((rows, D_MODEL), F32),
                   jax.ShapeDtypeStruct((rows, D_MODEL), BF16)],
        compiler_params=_params(("arbitrary",), 40),
        name="oproj",
    )(o, x1, w_o, g_ffn)


def _ffn_kernel(x_ref, h_ref, wg_ref, wu_ref, wd_ref, gl_ref, y_ref):
    f = pl.program_id(1)

    @pl.when(f == 0)
    def _():
        y_ref[...] = jnp.zeros_like(y_ref)

    h = h_ref[...]
    z = (jax.nn.silu(_dot(h, wg_ref[...].astype(BF16))) * _dot(h, wu_ref[...].astype(BF16))).astype(BF16)
    y_ref[...] += _dot(z, wd_ref[...].astype(BF16))

    @pl.when(f == pl.num_programs(1) - 1)
    def _():
        y_ref[...] = _rms(x_ref[...] + y_ref[...], gl_ref[...])


def _ffn(x2, h, w_gate, w_up, w_down, g_final, tm, tf):
    rows = x2.shape[0]
    row_spec = pl.BlockSpec((tm, D_MODEL), lambda i, f: (i, 0))
    return pl.pallas_call(
        _ffn_kernel,
        grid=(rows // tm, D_FF // tf),
        in_specs=[row_spec, row_spec,
                  pl.BlockSpec((D_MODEL, tf), lambda i, f: (0, f)),
                  pl.BlockSpec((D_MODEL, tf), lambda i, f: (0, f)),
                  pl.BlockSpec((tf, D_MODEL), lambda i, f: (f, 0)),
                  _const_spec((1, D_MODEL))],
        out_specs=pl.BlockSpec((tm, D_MODEL), lambda i, f: (i, 0), pipeline_mode=pl.Buffered(1)),
        out_shape=jax.ShapeDtypeStruct((rows, D_MODEL), F32),
        compiler_params=_params(("arbitrary", "arbitrary"), 56),
        name="ffn",
    )(x2, h, w_gate, w_up, w_down, g_final)


def _ssm_param_layouts(lam_re, lam_im, log_step, b_re, b_im, c_re, c_im):
    G, P = lam_re.shape
    ls = jnp.broadcast_to(log_step[:, None], (G, P))
    rep = lambda a: jnp.repeat(a, SSM_GROUP, axis=0)
    b_rows = lambda b: jnp.swapaxes(b, 1, 2).reshape(G * SSM_GROUP, P)
    row5 = jnp.stack([rep(lam_re), rep(lam_im), rep(ls), b_rows(b_re), b_rows(b_im)])
    col5 = jnp.stack([rep(lam_re).T, rep(lam_im).T, rep(ls).T,
                      c_re.reshape(G * SSM_GROUP, P).T, c_im.reshape(G * SSM_GROUP, P).T])
    flat = lambda a: a.reshape(N_LANE_BLOCKS, 1, STATE_BLOCK)
    flat3 = jnp.stack([flat(lam_re), flat(lam_im), flat(ls)])
    return row5, col5, flat3


def _states_to_blocks(h_re, h_im):
    S = h_re.shape[0]
    blk = lambda h: h.reshape(S, N_LANE_BLOCKS, STATE_BLOCK).transpose(1, 0, 2)
    return jnp.concatenate([blk(h_re), blk(h_im)], axis=-1)


def _blocks_to_states(st):
    S = st.shape[1]
    unblk = lambda a: a.transpose(1, 0, 2).reshape(1, S, N_SSM_GROUPS, SSM_STATE)
    return unblk(st[:, :, :STATE_BLOCK]), unblk(st[:, :, STATE_BLOCK:])


def kernel(x_prompt, x_sample, mem_prompt, state_pool_buf, state_ssm_re, state_ssm_im, cache_mem_k, cache_mem_v, g_mix, w_in, w_pool, pool_scale, ssm_lam_re, ssm_lam_im, ssm_log_step, ssm_b_re, ssm_b_im, ssm_c_re, ssm_c_im, ssm_d, w_glu, b_glu, w_out, g_cross, g_mem, w_q, w_k, w_v, w_o, g_ffn, w_gate, w_up, w_down, g_final):
    assert g_mix.shape[0] == 1, "single-layer step"
    B, T, _ = x_prompt.shape
    S, Ts, _ = x_sample.shape
    Lp, Ls = 8, Ts
    bf = lambda w: w.astype(BF16)
    vec = lambda v: v.reshape(1, -1)

    w_in_b, w_pool_b, w_glu_b, w_out_b = bf(w_in[0]), bf(w_pool[0]), bf(w_glu[0]), bf(w_out[0])
    w_q_b, w_o_b = bf(w_q[0]), bf(w_o[0])

    row5, col5, flat3 = _ssm_param_layouts(ssm_lam_re[0], ssm_lam_im[0], ssm_log_step[0],
                                           ssm_b_re[0], ssm_b_im[0], ssm_c_re[0], ssm_c_im[0])
    d_blocks = ssm_d[0].reshape(N_LANE_BLOCKS, 1, LANE)

    def layer(x, nseq, t_len, L, pool_buf, h0_blocks, attend, tm, ffn_tiles):
        rows = nseq * t_len
        u = _in_proj(x, vec(g_mix[0]), w_in_b, tm)

        if pool_buf is None:
            pool_out = _pool_mix_seq(u, w_pool_b, vec(pool_scale[0]), tc=512, rows_per_seq=t_len)
            tail = u[:POOL_LANE_BLOCKS].reshape(POOL_LANE_BLOCKS, nseq, t_len, LANE)[:, :, t_len - POOL_BUF:]
            new_buf = tail.transpose(1, 2, 0, 3).reshape(nseq, POOL_BUF, POOL_WIDTH)
        else:
            assert t_len == L and PAST_LEN >= POOL_BUF
            pool_out, buf_tm = _pool_mix_buf(u, pool_buf.transpose(1, 0, 2), w_pool_b, vec(pool_scale[0]), L=L)
            new_buf = buf_tm.transpose(1, 0, 2)

        ops = _ssm_prep(row5, col5, flat3, L)
        d_tiles = jnp.tile(d_blocks, (1, 1, L))
        if t_len > L:
            u_view = u.reshape(MIX_LANE_BLOCKS, nseq, t_len, LANE)
            g_act, st = _ssm_mix(u_view, ops, d_tiles, h0_blocks, L=L, Rn=128, seq=True)
        else:
            u_view = u.reshape(MIX_LANE_BLOCKS, 1, rows, LANE)
            g_act, st = _ssm_mix(u_view, ops, d_tiles, h0_blocks, L=L, Rn=nseq, seq=False)
        new_re, new_im = _blocks_to_states(st)

        x1, q = _mix_out(g_act.reshape(N_LANE_BLOCKS, rows, LANE), pool_out, x,
                         w_glu_b, vec(b_glu[0]), w_out_b, vec(g_cross[0]), w_q_b, 512)
        o = attend(q)
        x2, hf = _oproj(o, x1, w_o_b, vec(g_ffn[0]), tm)
        y = _ffn(x2, hf, w_gate[0], w_up[0], w_down[0], vec(g_final), *ffn_tiles)
        return y, new_buf, new_re, new_im

    mem = mem_prompt.reshape(B * N_MEM, D_MODEL)
    k_rows, k_b = _mem_kv(mem, vec(g_mem[0]), w_k[0], 512)
    v_rows, v_b = _mem_kv(mem, vec(g_mem[0]), w_v[0], 512)
    h0_p = jnp.zeros((N_LANE_BLOCKS, B, 2 * STATE_BLOCK), F32)
    yp, pb_p, re_p, im_p = layer(x_prompt.reshape(B * T, D_MODEL), B, T, Lp, None, h0_p,
                                 lambda q: _attn_prompt(q, k_b, v_b, 512, T), 512, (1024, 256))

    def attend_cache(q):
        q8 = jnp.pad(q.reshape(S, Ts, D_MODEL).astype(F32), ((0, 0), (0, 8 - Ts), (0, 0)))
        o8 = _attn_cache(q8, _cache_rows_view(cache_mem_k[0]), _cache_rows_view(cache_mem_v[0]), 4)
        return o8[:, :Ts].reshape(S * Ts, D_MODEL).astype(BF16)

    h0_s = _states_to_blocks(state_ssm_re[0], state_ssm_im[0])
    ys, pb_s, re_s, im_s = layer(x_sample.reshape(S * Ts, D_MODEL), S, Ts, Ls, state_pool_buf[0], h0_s,
                                 attend_cache, 512, (512, 512))

    mk = _cache_rows_unview(k_rows, B)[None]
    mv = _cache_rows_unview(v_rows, B)[None]
    return (yp.reshape(B, T, D_MODEL), ys.reshape(S, Ts, D_MODEL), pb_p[None], re_p, im_p, mk, mv,
            pb_s[None], re_s, im_s)
```

```python
import functools

import jax
import jax.numpy as jnp
from jax import lax
from jax.experimental import pallas as pl
from jax.experimental.pallas import tpu as pltpu

F32 = jnp.float32
BF16 = jnp.bfloat16

D_MODEL = 2048
POOL_WIDTH = 1024
SSM_WIDTH = 1024
POOL_WINDOWS = (2, 4, 8, 16)
POOL_GROUP = POOL_WIDTH // len(POOL_WINDOWS)
POOL_BUF = max(POOL_WINDOWS) - 1
SSM_GROUP = 16
N_SSM_GROUPS = SSM_WIDTH // SSM_GROUP
SSM_STATE = 64
N_MEM = 256
N_XHEADS = 4
XHEAD_DIM = D_MODEL // N_XHEADS
D_FF = 5632
EPS = 1e-6
PAST_LEN = 16384

LANE = 128
N_LANE_BLOCKS = SSM_WIDTH // LANE
GROUPS_PER_BLOCK = LANE // SSM_GROUP
STATE_BLOCK = GROUPS_PER_BLOCK * SSM_STATE
MIB = 1024 * 1024
HI = lax.Precision.HIGHEST

ROW_TILE = 512
PROMPT_CHUNK = 8
PROMPT_CHUNK_ROWS = 128
PROMPT_FFN_TILES = (1024, 256)
SAMPLE_FFN_TILES = (512, 512)
CACHE_SEQS_PER_STEP = 4


def _params(semantics, vmem_mib):
    return pltpu.CompilerParams(dimension_semantics=semantics, vmem_limit_bytes=vmem_mib * MIB)


def _const_spec(shape):
    return pl.BlockSpec(shape, lambda *_: (0,) * len(shape), pipeline_mode=pl.Buffered(1))


def _rms(x, g):
    r = lax.rsqrt(jnp.mean(x * x, axis=-1, keepdims=True) + EPS)
    return x * r * g


def _dot(a, b):
    return jnp.dot(a, b, preferred_element_type=F32)


MIX_LANE_BLOCKS = D_MODEL // LANE


def _in_proj_kernel(x_ref, g_ref, w_ref, u_ref):
    h = _rms(x_ref[...], g_ref[...]).astype(BF16)
    u = _dot(h, w_ref[...])
    for j in range(MIX_LANE_BLOCKS):
        u_ref[j] = u[:, j * LANE:(j + 1) * LANE]


def _in_proj(x, g, w, tm):
    rows = x.shape[0]
    return pl.pallas_call(
        _in_proj_kernel,
        grid=(rows // tm,),
        in_specs=[pl.BlockSpec((tm, D_MODEL), lambda i: (i, 0)),
                  _const_spec((1, D_MODEL)),
                  _const_spec((D_MODEL, D_MODEL))],
        out_specs=pl.BlockSpec((MIX_LANE_BLOCKS, tm, LANE), lambda i: (0, i, 0)),
        out_shape=jax.ShapeDtypeStruct((MIX_LANE_BLOCKS, rows, LANE), F32),
        compiler_params=_params(("arbitrary",), 40),
        name="in_proj",
    )(x, g, w)


def _lane_blocks(ref, rows=slice(None)):
    return jnp.concatenate([ref[j, rows, :] for j in range(ref.shape[0])], axis=1)


POOL_HIST = POOL_BUF + 1


def _in_proj_pool_kernel(x_ref, g_ref, w_ref, wp_ref, sc_ref, us_ref, o_ref, tail_ref, ext_ref, *, tc,
                         tiles_per_seq):
    u = _dot(_rms(x_ref[...], g_ref[...]).astype(BF16), w_ref[...])
    for j in range(N_LANE_BLOCKS):
        us_ref[j] = u[:, POOL_WIDTH + j * LANE:POOL_WIDTH + (j + 1) * LANE]

    it = lax.rem(pl.program_id(0), tiles_per_seq)

    @pl.when(it == 0)
    def _():
        ext_ref[:POOL_HIST, :] = jnp.zeros((POOL_HIST, POOL_WIDTH), F32)

    ext_ref[POOL_HIST:, :] = u[:, :POOL_WIDTH]
    pos = it * tc + lax.broadcasted_iota(jnp.int32, (tc, 1), 0)
    for g, w in enumerate(POOL_WINDOWS):
        sl = slice(g * POOL_GROUP, (g + 1) * POOL_GROUP)
        z = ext_ref[:, sl]
        s, span = z, 1
        while span < w:
            s = s + pltpu.roll(s, span, 0)
            span *= 2
        cnt = jnp.minimum(pos + 1, w).astype(F32)
        pooled = (s[POOL_HIST:] / cnt - z[POOL_HIST:]).astype(BF16)
        out = _dot(pooled, wp_ref[g]) * sc_ref[:, sl]
        for jj in range(POOL_GROUP // LANE):
            o_ref[g * (POOL_GROUP // LANE) + jj] = out[:, jj * LANE:(jj + 1) * LANE]
    last = ext_ref[tc:tc + POOL_HIST, :]
    tail_ref[0] = last
    ext_ref[:POOL_HIST, :] = last


POOL_LANE_BLOCKS = POOL_WIDTH // LANE


def _in_proj_pool(x, g, w, w_pool, scale, *, tc, rows_per_seq):
    rows = x.shape[0]
    tiles_per_seq = rows_per_seq // tc
    slab_spec = pl.BlockSpec((N_LANE_BLOCKS, tc, LANE), lambda i: (0, i, 0))
    slabs = jax.ShapeDtypeStruct((N_LANE_BLOCKS, rows, LANE), F32)
    return pl.pallas_call(
        functools.partial(_in_proj_pool_kernel, tc=tc, tiles_per_seq=tiles_per_seq),
        grid=(rows // tc,),
        in_specs=[pl.BlockSpec((tc, D_MODEL), lambda i: (i, 0)),
                  _const_spec((1, D_MODEL)),
                  _const_spec((D_MODEL, D_MODEL)),
                  _const_spec((len(POOL_WINDOWS), POOL_GROUP, POOL_GROUP)),
                  _const_spec((1, POOL_WIDTH))],
        out_specs=[slab_spec, slab_spec,
                   pl.BlockSpec((1, POOL_HIST, POOL_WIDTH), lambda i: (i // tiles_per_seq, 0, 0))],
        out_shape=[slabs, slabs,
                   jax.ShapeDtypeStruct((rows // rows_per_seq, POOL_HIST, POOL_WIDTH), F32)],
        scratch_shapes=[pltpu.VMEM((tc + POOL_HIST, POOL_WIDTH), F32)],
        compiler_params=_params(("arbitrary",), 48),
        name="in_proj_pool",
    )(x, g, w, w_pool, scale)


def _pool_buf_kernel(u_ref, buf_ref, wp_ref, sc_ref, o_ref, nb_ref, *, L):
    nseq = buf_ref.shape[1]
    per_group = POOL_GROUP // LANE

    def token_rows(t):
        return pl.ds(t, nseq, stride=L)

    def slab(idx, g):
        if idx < POOL_BUF:
            return buf_ref[idx, :, g * POOL_GROUP:(g + 1) * POOL_GROUP]
        return jnp.concatenate([u_ref[g * per_group + jj, token_rows(idx - POOL_BUF), :]
                                for jj in range(per_group)], axis=1)

    for g, w in enumerate(POOL_WINDOWS):
        sl = slice(g * POOL_GROUP, (g + 1) * POOL_GROUP)
        pooled = []
        for t in range(L):
            acc = slab(POOL_BUF + t, g)
            for back in range(1, w):
                acc = acc + slab(POOL_BUF + t - back, g)
            pooled.append((acc / float(w) - slab(POOL_BUF + t, g)).astype(BF16))
        out = _dot(jnp.concatenate(pooled, axis=0), wp_ref[g]) * sc_ref[:, sl]
        for t in range(L):
            for jj in range(per_group):
                o_ref[g * per_group + jj, token_rows(t), :] = out[t * nseq:(t + 1) * nseq,
                                                                  jj * LANE:(jj + 1) * LANE]
    keep = POOL_BUF - L
    for k in range(keep):
        nb_ref[k] = buf_ref[k + L]
    for t in range(L):
        nb_ref[keep + t] = _lane_blocks(u_ref, token_rows(t))


def _pool_mix_buf(u_blocks, buf_tm, w_pool, scale, *, L):
    rows = u_blocks.shape[1]
    slab_spec = pl.BlockSpec((POOL_LANE_BLOCKS, rows, LANE), lambda i: (0, 0, 0))
    buf_spec = pl.BlockSpec(buf_tm.shape, lambda i: (0, 0, 0))
    return pl.pallas_call(
        functools.partial(_pool_buf_kernel, L=L),
        grid=(1,),
        in_specs=[slab_spec, buf_spec, _const_spec((len(POOL_WINDOWS), POOL_GROUP, POOL_GROUP)),
                  _const_spec((1, POOL_WIDTH))],
        out_specs=[slab_spec, buf_spec],
        out_shape=[jax.ShapeDtypeStruct((POOL_LANE_BLOCKS, rows, LANE), F32),
                   jax.ShapeDtypeStruct(buf_tm.shape, F32)],
        compiler_params=_params(("arbitrary",), 48),
        name="pool_mix_buf",
    )(u_blocks, buf_tm, w_pool, scale)


def _cmul(ar, ai, br, bi):
    return ar * br - ai * bi, ar * bi + ai * br


def _cexp(lam_re, lam_im, log_step):
    delta = jnp.exp(log_step)
    mag = jnp.exp(lam_re * delta)
    ang = lam_im * delta
    return mag * jnp.cos(ang), mag * jnp.sin(ang)


def _ssm_prep_kernel(row_ref, col_ref, flat_ref, t_ref, w_ref, v_ref, al_ref, *, L):
    lr, li, ls, br, bi = (row_ref[i] for i in range(5))
    ar, ai = _cexp(lr, li, ls)
    den = lr * lr + li * li
    xr = ar - 1.0
    fr = (xr * lr + ai * li) / den
    fi = (ai * lr - xr * li) / den
    zs = [_cmul(fr, fi, br, bi)]
    for _ in range(1, L):
        zs.append(_cmul(ar, ai, *zs[-1]))

    lrc, lic, lsc, cr, ci = (col_ref[i] for i in range(5))
    acr, aci = _cexp(lrc, lic, lsc)
    xs = []
    cur = (cr, ci)
    for _ in range(L):
        cur = _cmul(acr, aci, *cur)
        xs.append(cur)

    def iota(shape, dim):
        return lax.broadcasted_iota(jnp.int32, shape, dim)

    same_tt = (iota((LANE, LANE), 0) >> 4) == (iota((LANE, LANE), 1) >> 4)
    c_stack = jnp.concatenate([cr, -ci], axis=0)
    zero_tile = jnp.zeros((LANE, LANE), BF16)
    lag = []
    for d in range(L):
        z_stack = jnp.concatenate([zs[d][0], zs[d][1]], axis=1)
        tile = jnp.dot(z_stack, c_stack, precision=HI, preferred_element_type=F32)
        lag.append(jnp.where(same_tt, tile, 0.0).astype(BF16))
    for k in range(L):
        for t in range(L):
            t_ref[0, k * LANE:(k + 1) * LANE, t * LANE:(t + 1) * LANE] = lag[t - k] if t >= k else zero_tile

    same_w = (iota((LANE, STATE_BLOCK), 0) >> 4) == (iota((LANE, STATE_BLOCK), 1) >> 6)
    for k in range(L):
        zr, zi = zs[L - 1 - k]
        wr = jnp.where(same_w, jnp.concatenate([zr] * GROUPS_PER_BLOCK, axis=1), 0.0)
        wi = jnp.where(same_w, jnp.concatenate([zi] * GROUPS_PER_BLOCK, axis=1), 0.0)
        w_ref[0, k * LANE:(k + 1) * LANE, :] = jnp.concatenate([wr, wi], axis=1).astype(BF16)

    same_v = (iota((STATE_BLOCK, LANE), 0) >> 6) == (iota((STATE_BLOCK, LANE), 1) >> 4)
    for t in range(L):
        xr_t, xi_t = xs[t]
        vr = jnp.where(same_v, jnp.concatenate([xr_t] * GROUPS_PER_BLOCK, axis=0), 0.0)
        vi = jnp.where(same_v, jnp.concatenate([xi_t] * GROUPS_PER_BLOCK, axis=0), 0.0)
        v_ref[0, :, t * LANE:(t + 1) * LANE] = jnp.concatenate([vr, -vi], axis=0).astype(BF16)

    flr, fli, fls = (flat_ref[i, 0] for i in range(3))
    far, fai = _cexp(flr, fli, fls)
    pr, pi = far, fai
    for _ in range(L - 1):
        pr, pi = _cmul(far, fai, pr, pi)
    al_ref[0] = jnp.concatenate([pr, pi], axis=1)


def _ssm_prep(row5, col5, flat3, L):
    lk = L * LANE
    return pl.pallas_call(
        functools.partial(_ssm_prep_kernel, L=L),
        grid=(N_LANE_BLOCKS,),
        in_specs=[pl.BlockSpec((5, LANE, SSM_STATE), lambda j: (0, j, 0)),
                  pl.BlockSpec((5, SSM_STATE, LANE), lambda j: (0, 0, j)),
                  pl.BlockSpec((3, 1, 1, STATE_BLOCK), lambda j: (0, j, 0, 0))],
        out_specs=[pl.BlockSpec((1, lk, lk), lambda j: (j, 0, 0)),
                   pl.BlockSpec((1, lk, 2 * STATE_BLOCK), lambda j: (j, 0, 0)),
                   pl.BlockSpec((1, 2 * STATE_BLOCK, lk), lambda j: (j, 0, 0)),
                   pl.BlockSpec((1, 1, 2 * STATE_BLOCK), lambda j: (j, 0, 0))],
        out_shape=[jax.ShapeDtypeStruct((N_LANE_BLOCKS, lk, lk), BF16),
                   jax.ShapeDtypeStruct((N_LANE_BLOCKS, lk, 2 * STATE_BLOCK), BF16),
                   jax.ShapeDtypeStruct((N_LANE_BLOCKS, 2 * STATE_BLOCK, lk), BF16),
                   jax.ShapeDtypeStruct((N_LANE_BLOCKS, 1, 2 * STATE_BLOCK), F32)],
        compiler_params=_params(("arbitrary",), 40),
        name=f"ssm_prep_L{L}",
    )(row5, col5, flat3)


def _ssm_kernel(u_ref, t_ref, w_ref, v_ref, al_ref, d_ref, h0_ref, g_ref, st_ref, *scratch, B, Rn, L, seq):
    M = B * Rn
    sb = STATE_BLOCK
    io_scr = scratch[0]
    for b in range(B):
        for k in range(L):
            io_scr[k, pl.ds(b, Rn, stride=B), :] = u_ref[0, b, pl.ds(k, Rn, stride=L), :]
    u = jnp.concatenate([io_scr[k] for k in range(L)], axis=1)
    ub = u.astype(BF16)
    e = _dot(ub, w_ref[0])
    ar = al_ref[0][:, :sb]
    ai = al_ref[0][:, sb:]

    def advance(s, eb):
        sr, si = s[:, :sb], s[:, sb:]
        return jnp.concatenate([ar * sr - ai * si + eb[:, :sb], ar * si + ai * sr + eb[:, sb:]], axis=1)

    if seq:
        e_scr, s_scr, c_scr = scratch[1:]

        assert 2 * B == 8 and Rn % 2 == 0, "two chunk rows of B sequences fill one sublane tile"

        @pl.when(pl.program_id(1) == 0)
        def _():
            c_scr[:B, :] = h0_ref[0]
            c_scr[B:, :] = jnp.zeros((B, 2 * sb), F32)

        e_scr[...] = e
        top = lax.broadcasted_iota(jnp.int32, (2 * B, 1), 0) < B

        def body(i, s):
            rows = pl.ds(pl.multiple_of(i * 2 * B, 2 * B), 2 * B)
            e2 = e_scr[rows, :]
            mid = pltpu.roll(advance(s, e2), B, 0)
            s_scr[rows, :] = jnp.where(top, s, mid)
            return pltpu.roll(advance(mid, e2), B, 0)

        c_scr[...] = lax.fori_loop(0, Rn // 2, body, c_scr[...])
        st_ref[0] = c_scr[:B, :]
        s_start = s_scr[...]
    else:
        s_start = h0_ref[0]
        st_ref[0] = advance(s_start, e)
    y = _dot(ub, t_ref[0]) + _dot(s_start.astype(BF16), v_ref[0]) + d_ref[0] * u
    g = jax.nn.gelu(y)
    for k in range(L):
        io_scr[k] = g[:, k * LANE:(k + 1) * LANE]
    for b in range(B):
        for k in range(L):
            g_ref[0, b, pl.ds(k, Rn, stride=L), :] = io_scr[k, pl.ds(b, Rn, stride=B), :]


def _ssm_mix(u_view, ops, d_tiles, h0, *, L, Rn, seq):
    t_op, w_op, v_op, a_l = ops
    nj = N_LANE_BLOCKS
    _, B, trows, _ = u_view.shape
    lk = L * LANE
    nseq = h0.shape[1]
    tiles = trows // (Rn * L)
    M = B * Rn
    sb2 = 2 * STATE_BLOCK
    first = u_view.shape[0] - nj
    u_spec = pl.BlockSpec((1, B, Rn * L, LANE), lambda j, i: (j + first, 0, i, 0))
    g_spec = pl.BlockSpec((1, B, Rn * L, LANE), lambda j, i: (j, 0, i, 0))
    scratch = [pltpu.VMEM((L, M, LANE), F32)]
    if seq:
        scratch += [pltpu.VMEM((M, sb2), F32), pltpu.VMEM((M, sb2), F32), pltpu.VMEM((2 * B, sb2), F32)]
    return pl.pallas_call(
        functools.partial(_ssm_kernel, B=B, Rn=Rn, L=L, seq=seq),
        grid=(nj, tiles),
        in_specs=[u_spec,
                  pl.BlockSpec((1, lk, lk), lambda j, i: (j, 0, 0)),
                  pl.BlockSpec((1, lk, sb2), lambda j, i: (j, 0, 0)),
                  pl.BlockSpec((1, sb2, lk), lambda j, i: (j, 0, 0)),
                  pl.BlockSpec((1, 1, sb2), lambda j, i: (j, 0, 0)),
                  pl.BlockSpec((1, 1, lk), lambda j, i: (j, 0, 0)),
                  pl.BlockSpec((1, nseq, sb2), lambda j, i: (j, 0, 0))],
        out_specs=[g_spec, pl.BlockSpec((1, nseq, sb2), lambda j, i: (j, 0, 0))],
        out_shape=[jax.ShapeDtypeStruct((nj,) + u_view.shape[1:], F32),
                   jax.ShapeDtypeStruct((nj, nseq, sb2), F32)],
        scratch_shapes=scratch,
        compiler_params=_params(("arbitrary", "arbitrary"), 48),
        name="ssm_mix_seq" if seq else "ssm_mix_rows",
    )(u_view, t_op, w_op, v_op, a_l, d_tiles, h0)


def _mix_out_kernel(g_ref, po_ref, x_ref, wglu_ref, bglu_ref, wout_ref, gc_ref, wq_ref, *rest, attend):
    if attend:
        k_ref, v_ref, x1_ref, q_ref = rest
    else:
        x1_ref, q_ref = rest
    g = _lane_blocks(g_ref)
    gate = jax.nn.sigmoid(_dot(g.astype(BF16), wglu_ref[...]) + bglu_ref[...])
    ssm_out = (g * gate).astype(BF16)
    mix = (_dot(_lane_blocks(po_ref).astype(BF16), wout_ref[:POOL_WIDTH, :])
           + _dot(ssm_out, wout_ref[POOL_WIDTH:, :]))
    x1 = x_ref[...] + mix
    x1_ref[...] = x1
    q = _dot(_rms(x1, gc_ref[...]).astype(BF16), wq_ref[...]).astype(BF16)
    if attend:
        for h in range(N_XHEADS):
            sl = slice(h * XHEAD_DIM, (h + 1) * XHEAD_DIM)
            s = lax.dot_general(q[:, sl], k_ref[:, sl], _NT, preferred_element_type=F32)
            p = _softmax_rows(s * (XHEAD_DIM ** -0.5))
            q_ref[:, sl] = _dot(p.astype(BF16), v_ref[:, sl]).astype(BF16)
    else:
        q_ref[...] = q


def _mix_out(g_rows, pool_out, x, w_glu, b_glu, w_out, g_cross, w_q, tm, mem_kv=None, rows_per_seq=None):
    rows = x.shape[0]
    attend = mem_kv is not None
    extra_specs, extra = [], []
    if attend:
        tiles = rows_per_seq // tm
        kv_spec = pl.BlockSpec((N_MEM, D_MODEL), lambda i: (i // tiles, 0))
        extra_specs, extra = [kv_spec, kv_spec], list(mem_kv)
    return pl.pallas_call(
        functools.partial(_mix_out_kernel, attend=attend),
        grid=(rows // tm,),
        in_specs=[pl.BlockSpec((N_LANE_BLOCKS, tm, LANE), lambda i: (0, i, 0)),
                  pl.BlockSpec((POOL_LANE_BLOCKS, tm, LANE), lambda i: (0, i, 0)),
                  pl.BlockSpec((tm, D_MODEL), lambda i: (i, 0)),
                  _const_spec((SSM_WIDTH, SSM_WIDTH)), _const_spec((1, SSM_WIDTH)),
                  _const_spec((D_MODEL, D_MODEL)), _const_spec((1, D_MODEL)),
                  _const_spec((D_MODEL, D_MODEL))] + extra_specs,
        out_specs=[pl.BlockSpec((tm, D_MODEL), lambda i: (i, 0)),
                   pl.BlockSpec((tm, D_MODEL), lambda i: (i, 0))],
        out_shape=[jax.ShapeDtypeStruct((rows, D_MODEL), F32),
                   jax.ShapeDtypeStruct((rows, D_MODEL), BF16)],
        compiler_params=_params(("arbitrary",), 56),
        name="mix_out_attn" if attend else "mix_out",
    )(g_rows, pool_out, x, w_glu, b_glu, w_out, g_cross, w_q, *extra)


HEAD_LANE_BLOCKS = XHEAD_DIM // LANE
CACHE_ROW_PITCH = N_XHEADS * HEAD_LANE_BLOCKS


def _cache_rows_view(cache):
    S = cache.shape[0]
    c5 = cache.reshape(S, N_MEM, N_XHEADS, HEAD_LANE_BLOCKS, LANE)
    return c5.transpose(0, 1, 3, 2, 4).reshape(S, N_MEM * CACHE_ROW_PITCH, LANE)


def _cache_rows_unview(rows, nseq):
    r5 = rows.reshape(nseq, N_MEM, HEAD_LANE_BLOCKS, N_XHEADS, LANE)
    return r5.transpose(0, 1, 3, 2, 4).reshape(nseq, N_MEM, N_XHEADS, XHEAD_DIM)


def _mem_kv_kernel(m_ref, g_ref, w_ref, rows_ref, kvb_ref):
    tm = m_ref.shape[0]
    kv = _dot(_rms(m_ref[...], g_ref[...]).astype(BF16), w_ref[...].astype(BF16))
    kvb_ref[...] = kv.astype(BF16)
    for c in range(HEAD_LANE_BLOCKS):
        for h in range(N_XHEADS):
            lo = h * XHEAD_DIM + c * LANE
            rows_ref[pl.ds(c * N_XHEADS + h, tm, stride=CACHE_ROW_PITCH), :] = kv[:, lo:lo + LANE]


def _mem_kv(mem, g_mem, w, tm):
    rows = mem.shape[0]
    return pl.pallas_call(
        _mem_kv_kernel,
        grid=(rows // tm,),
        in_specs=[pl.BlockSpec((tm, D_MODEL), lambda i: (i, 0)),
                  _const_spec((1, D_MODEL)),
                  _const_spec((D_MODEL, D_MODEL))],
        out_specs=[pl.BlockSpec((tm * CACHE_ROW_PITCH, LANE), lambda i: (i, 0)),
                   pl.BlockSpec((tm, D_MODEL), lambda i: (i, 0))],
        out_shape=[jax.ShapeDtypeStruct((rows * CACHE_ROW_PITCH, LANE), F32),
                   jax.ShapeDtypeStruct((rows, D_MODEL), BF16)],
        compiler_params=_params(("arbitrary",), 48),
        name="mem_kv",
    )(mem, g_mem, w)


_NT = (((1,), (1,)), ((), ()))


def _softmax_rows(s):
    e = jnp.exp(s - jnp.max(s, axis=-1, keepdims=True))
    return e / jnp.sum(e, axis=-1, keepdims=True)


def _attn_cache_kernel(q_ref, k_ref, v_ref, o_ref, *, bb):
    nc = HEAD_LANE_BLOCKS
    mc = N_MEM * nc
    lane_c = lax.broadcasted_iota(jnp.int32, (8, mc), 1) & (nc - 1)
    owns = []
    for b in range(bb):
        q8 = q_ref[b]
        for h in range(N_XHEADS):
            xk = k_ref[b, pl.ds(h, mc, stride=N_XHEADS), :].astype(BF16)
            qc = jnp.concatenate([q8[:, h * XHEAD_DIM + c * LANE:h * XHEAD_DIM + (c + 1) * LANE]
                                  for c in range(nc)], axis=0).astype(BF16)
            part = lax.dot_general(qc, xk, _NT, preferred_element_type=F32)
            own = part[8 * (nc - 1):]
            for c in range(nc - 2, -1, -1):
                own = jnp.where(lane_c == c, part[8 * c:8 * (c + 1)], own)
            owns.append(own)
    own_all = jnp.concatenate(owns, axis=0)
    lane_all = lax.broadcasted_iota(jnp.int32, own_all.shape, 1) & (nc - 1)
    pair = own_all + pltpu.roll(own_all, 1, 1)
    full = pair + pltpu.roll(pair, 2, 1)
    s = jnp.where(lane_all == nc - 1, full * (XHEAD_DIM ** -0.5), -1e30)
    e = jnp.exp(s - jnp.max(s, axis=1, keepdims=True))
    p = e / jnp.sum(e, axis=1, keepdims=True)
    p2 = p + pltpu.roll(p, mc - 1, 1)
    p4 = p2 + pltpu.roll(p2, mc - 2, 1)
    for b in range(bb):
        xv = jnp.concatenate([v_ref[b, pl.ds(h, mc, stride=N_XHEADS), :] for h in range(N_XHEADS)],
                             axis=1).astype(BF16)
        pm_rows = []
        for h in range(N_XHEADS):
            r0 = 8 * (b * N_XHEADS + h)
            pm_rows += [jnp.where(lane_c == c, p4[r0:r0 + 8], 0.0) for c in range(nc)]
        o_all = _dot(jnp.concatenate(pm_rows, axis=0).astype(BF16), xv)
        for h in range(N_XHEADS):
            for c in range(nc):
                g = h * nc + c
                o_ref[b, :, h * XHEAD_DIM + c * LANE:h * XHEAD_DIM + (c + 1) * LANE] = (
                    o_all[8 * g:8 * (g + 1), h * LANE:(h + 1) * LANE])


def _attn_cache(q8, k_rows, v_rows, bb):
    nseq = q8.shape[0]
    rows = N_MEM * CACHE_ROW_PITCH
    return pl.pallas_call(
        functools.partial(_attn_cache_kernel, bb=bb),
        grid=(nseq // bb,),
        in_specs=[pl.BlockSpec((bb, 8, D_MODEL), lambda i: (i, 0, 0)),
                  pl.BlockSpec((bb, rows, LANE), lambda i: (i, 0, 0)),
                  pl.BlockSpec((bb, rows, LANE), lambda i: (i, 0, 0))],
        out_specs=pl.BlockSpec((bb, 8, D_MODEL), lambda i: (i, 0, 0)),
        out_shape=jax.ShapeDtypeStruct((nseq, 8, D_MODEL), F32),
        compiler_params=_params(("arbitrary",), 48),
        name="attn_cache",
    )(q8, k_rows, v_rows)


def _oproj_kernel(o_ref, x1_ref, wo_ref, gf_ref, x2_ref, h_ref):
    x2 = x1_ref[...] + _dot(o_ref[...], wo_ref[...])
    x2_ref[...] = x2
    h_ref[...] = _rms(x2, gf_ref[...]).astype(BF16)


def _oproj(o, x1, w_o, g_ffn, tm):
    rows = x1.shape[0]
    row_spec = pl.BlockSpec((tm, D_MODEL), lambda i: (i, 0))
    return pl.pallas_call(
        _oproj_kernel,
        grid=(rows // tm,),
        in_specs=[row_spec, row_spec, _const_spec((D_MODEL, D_MODEL)), _const_spec((1, D_MODEL))],
        out_specs=[row_spec, row_spec],
        out_shape=[jax.ShapeDtypeStruct((rows, D_MODEL), F32),
                   jax.ShapeDtypeStruct((rows, D_MODEL), BF16)],
        compiler_params=_params(("arbitrary",), 40),
        name="oproj",
    )(o, x1, w_o, g_ffn)


def _ffn_kernel(x_ref, h_ref, wg_ref, wu_ref, wd_ref, gl_ref, y_ref):
    f = pl.program_id(1)

    @pl.when(f == 0)
    def _():
        y_ref[...] = jnp.zeros_like(y_ref)

    h = h_ref[...]
    z = (jax.nn.silu(_dot(h, wg_ref[...].astype(BF16))) * _dot(h, wu_ref[...].astype(BF16))).astype(BF16)
    y_ref[...] += _dot(z, wd_ref[...].astype(BF16))

    @pl.when(f == pl.num_programs(1) - 1)
    def _():
        y_ref[...] = _rms(x_ref[...] + y_ref[...], gl_ref[...])


def _ffn(x2, h, w_gate, w_up, w_down, g_final, tm, tf):
    rows = x2.shape[0]
    row_spec = pl.BlockSpec((tm, D_MODEL), lambda i, f: (i, 0))
    return pl.pallas_call(
        _ffn_kernel,
        grid=(rows // tm, D_FF // tf),
        in_specs=[row_spec, row_spec,
                  pl.BlockSpec((D_MODEL, tf), lambda i, f: (0, f)),
                  pl.BlockSpec((D_MODEL, tf), lambda i, f: (0, f)),
                  pl.BlockSpec((tf, D_MODEL), lambda i, f: (f, 0)),
                  _const_spec((1, D_MODEL))],
        out_specs=pl.BlockSpec((tm, D_MODEL), lambda i, f: (i, 0), pipeline_mode=pl.Buffered(1)),
        out_shape=jax.ShapeDtypeStruct((rows, D_MODEL), F32),
        compiler_params=_params(("arbitrary", "arbitrary"), 56),
        name="ffn",
    )(x2, h, w_gate, w_up, w_down, g_final)


def _ssm_param_layouts(lam_re, lam_im, log_step, b_re, b_im, c_re, c_im):
    G, P = lam_re.shape
    ls = jnp.broadcast_to(log_step[:, None], (G, P))
    rep = lambda a: jnp.repeat(a, SSM_GROUP, axis=0)
    b_rows = lambda b: jnp.swapaxes(b, 1, 2).reshape(G * SSM_GROUP, P)
    row5 = jnp.stack([rep(lam_re), rep(lam_im), rep(ls), b_rows(b_re), b_rows(b_im)])
    col5 = jnp.stack([rep(lam_re).T, rep(lam_im).T, rep(ls).T,
                      c_re.reshape(G * SSM_GROUP, P).T, c_im.reshape(G * SSM_GROUP, P).T])
    flat = lambda a: a.reshape(N_LANE_BLOCKS, 1, STATE_BLOCK)
    flat3 = jnp.stack([flat(lam_re), flat(lam_im), flat(ls)])
    return row5, col5, flat3


def _states_to_blocks(h_re, h_im):
    S = h_re.shape[0]
    blk = lambda h: h.reshape(S, N_LANE_BLOCKS, STATE_BLOCK).transpose(1, 0, 2)
    return jnp.concatenate([blk(h_re), blk(h_im)], axis=-1)


def _blocks_to_states(st):
    S = st.shape[1]
    unblk = lambda a: a.transpose(1, 0, 2).reshape(1, S, N_SSM_GROUPS, SSM_STATE)
    return unblk(st[:, :, :STATE_BLOCK]), unblk(st[:, :, STATE_BLOCK:])


def kernel(x_prompt, x_sample, mem_prompt, state_pool_buf, state_ssm_re, state_ssm_im, cache_mem_k, cache_mem_v, g_mix, w_in, w_pool, pool_scale, ssm_lam_re, ssm_lam_im, ssm_log_step, ssm_b_re, ssm_b_im, ssm_c_re, ssm_c_im, ssm_d, w_glu, b_glu, w_out, g_cross, g_mem, w_q, w_k, w_v, w_o, g_ffn, w_gate, w_up, w_down, g_final):
    assert g_mix.shape[0] == 1, "single-layer step"
    B, T, _ = x_prompt.shape
    S, Ts, _ = x_sample.shape
    bf = lambda w: w.astype(BF16)
    vec = lambda v: v.reshape(1, -1)

    w_in_b, w_pool_b, w_glu_b, w_out_b = bf(w_in[0]), bf(w_pool[0]), bf(w_glu[0]), bf(w_out[0])
    w_q_b, w_o_b = bf(w_q[0]), bf(w_o[0])

    row5, col5, flat3 = _ssm_param_layouts(ssm_lam_re[0], ssm_lam_im[0], ssm_log_step[0],
                                           ssm_b_re[0], ssm_b_im[0], ssm_c_re[0], ssm_c_im[0])
    d_blocks = ssm_d[0].reshape(N_LANE_BLOCKS, 1, LANE)

    gains = dict(g_mix=vec(g_mix[0]), scale=vec(pool_scale[0]), b_glu=vec(b_glu[0]), g_cross=vec(g_cross[0]),
                 g_ffn=vec(g_ffn[0]), g_final=vec(g_final))

    def ssm(u_view, L, h0_blocks, **tiles):
        ops = _ssm_prep(row5, col5, flat3, L)
        g_act, st = _ssm_mix(u_view, ops, jnp.tile(d_blocks, (1, 1, L)), h0_blocks, L=L, **tiles)
        rows = u_view.shape[1] * u_view.shape[2]
        return g_act.reshape(N_LANE_BLOCKS, rows, LANE), _blocks_to_states(st)

    def finish(o, x1, ffn_tiles):
        x2, hf = _oproj(o, x1, w_o_b, gains["g_ffn"], ROW_TILE)
        return _ffn(x2, hf, w_gate[0], w_up[0], w_down[0], gains["g_final"], *ffn_tiles)

    mem = mem_prompt.reshape(B * N_MEM, D_MODEL)
    k_rows, k_b = _mem_kv(mem, vec(g_mem[0]), w_k[0], ROW_TILE)
    v_rows, v_b = _mem_kv(mem, vec(g_mem[0]), w_v[0], ROW_TILE)
    xp = x_prompt.reshape(B * T, D_MODEL)
    us_p, po_p, tail_p = _in_proj_pool(xp, gains["g_mix"], w_in_b, w_pool_b, gains["scale"],
                                       tc=ROW_TILE, rows_per_seq=T)
    h0_p = jnp.zeros((N_LANE_BLOCKS, B, 2 * STATE_BLOCK), F32)
    g_p, (re_p, im_p) = ssm(us_p.reshape(N_LANE_BLOCKS, B, T, LANE), PROMPT_CHUNK, h0_p,
                            Rn=PROMPT_CHUNK_ROWS, seq=True)
    x1_p, o_p = _mix_out(g_p, po_p, xp, w_glu_b, gains["b_glu"], w_out_b, gains["g_cross"], w_q_b, ROW_TILE,
                         mem_kv=(k_b, v_b), rows_per_seq=T)
    yp = finish(o_p, x1_p, PROMPT_FFN_TILES)
    pb_p = tail_p[:, POOL_HIST - POOL_BUF:]

    assert PAST_LEN >= POOL_BUF
    xs = x_sample.reshape(S * Ts, D_MODEL)
    u_s = _in_proj(xs, gains["g_mix"], w_in_b, ROW_TILE)
    po_s, buf_tm = _pool_mix_buf(u_s, state_pool_buf[0].transpose(1, 0, 2), w_pool_b, gains["scale"], L=Ts)
    pb_s = buf_tm.transpose(1, 0, 2)
    h0_s = _states_to_blocks(state_ssm_re[0], state_ssm_im[0])
    g_s, (re_s, im_s) = ssm(u_s.reshape(MIX_LANE_BLOCKS, 1, S * Ts, LANE), Ts, h0_s, Rn=S, seq=False)
    x1_s, q_s = _mix_out(g_s, po_s, xs, w_glu_b, gains["b_glu"], w_out_b, gains["g_cross"], w_q_b, ROW_TILE)
    q8 = jnp.pad(q_s.reshape(S, Ts, D_MODEL).astype(F32), ((0, 0), (0, 8 - Ts), (0, 0)))
    o8 = _attn_cache(q8, _cache_rows_view(cache_mem_k[0]), _cache_rows_view(cache_mem_v[0]), CACHE_SEQS_PER_STEP)
    ys = finish(o8[:, :Ts].reshape(S * Ts, D_MODEL).astype(BF16), x1_s, SAMPLE_FFN_TILES)

    mk = _cache_rows_unview(k_rows, B)[None]
    mv = _cache_rows_unview(v_rows, B)[None]
    return (yp.reshape(B, T, D_MODEL), ys.reshape(S, Ts, D_MODEL), pb_p[None], re_p, im_p, mk, mv,
            pb_s[None], re_s, im_s)
```

```python
import functools

import jax
import jax.numpy as jnp
from jax import lax
from jax.experimental import pallas as pl
from jax.experimental.pallas import tpu as pltpu

F32 = jnp.float32
BF16 = jnp.bfloat16

D_MODEL = 2048
POOL_WIDTH = 1024
SSM_WIDTH = 1024
POOL_WINDOWS = (2, 4, 8, 16)
POOL_GROUP = POOL_WIDTH // len(POOL_WINDOWS)
POOL_BUF = max(POOL_WINDOWS) - 1
SSM_GROUP = 16
N_SSM_GROUPS = SSM_WIDTH // SSM_GROUP
SSM_STATE = 64
N_MEM = 256
N_XHEADS = 4
XHEAD_DIM = D_MODEL // N_XHEADS
D_FF = 5632
EPS = 1e-6
PAST_LEN = 16384

LANE = 128
N_LANE_BLOCKS = SSM_WIDTH // LANE
GROUPS_PER_BLOCK = LANE // SSM_GROUP
STATE_BLOCK = GROUPS_PER_BLOCK * SSM_STATE
MIB = 1024 * 1024
HI = lax.Precision.HIGHEST

ROW_TILE = 512
PROMPT_CHUNK = 8
PROMPT_CHUNK_ROWS = 128
PROMPT_FFN_TILES = (1024, 256)
SAMPLE_FFN_TILES = (512, 512)
CACHE_SEQS_PER_STEP = 4


def _params(semantics, vmem_mib):
    return pltpu.CompilerParams(dimension_semantics=semantics, vmem_limit_bytes=vmem_mib * MIB)


def _const_spec(shape):
    return pl.BlockSpec(shape, lambda *_: (0,) * len(shape), pipeline_mode=pl.Buffered(1))


def _rms(x, g):
    r = lax.rsqrt(jnp.mean(x * x, axis=-1, keepdims=True) + EPS)
    return x * r * g


def _dot(a, b):
    return jnp.dot(a, b, preferred_element_type=F32)


MIX_LANE_BLOCKS = D_MODEL // LANE


def _in_proj_kernel(x_ref, g_ref, w_ref, u_ref):
    h = _rms(x_ref[...], g_ref[...]).astype(BF16)
    u = _dot(h, w_ref[...])
    for j in range(MIX_LANE_BLOCKS):
        u_ref[j] = u[:, j * LANE:(j + 1) * LANE]


def _in_proj(x, g, w, tm):
    rows = x.shape[0]
    return pl.pallas_call(
        _in_proj_kernel,
        grid=(rows // tm,),
        in_specs=[pl.BlockSpec((tm, D_MODEL), lambda i: (i, 0)),
                  _const_spec((1, D_MODEL)),
                  _const_spec((D_MODEL, D_MODEL))],
        out_specs=pl.BlockSpec((MIX_LANE_BLOCKS, tm, LANE), lambda i: (0, i, 0)),
        out_shape=jax.ShapeDtypeStruct((MIX_LANE_BLOCKS, rows, LANE), F32),
        compiler_params=_params(("arbitrary",), 40),
        name="in_proj",
    )(x, g, w)


def _lane_blocks(ref, rows=slice(None)):
    return jnp.concatenate([ref[j, rows, :] for j in range(ref.shape[0])], axis=1)


POOL_HIST = POOL_BUF + 1


def _in_proj_pool_kernel(x_ref, g_ref, w_ref, wp_ref, sc_ref, us_ref, o_ref, tail_ref, ext_ref, *, tc,
                         tiles_per_seq):
    u = _dot(_rms(x_ref[...], g_ref[...]).astype(BF16), w_ref[...])
    for j in range(N_LANE_BLOCKS):
        us_ref[j] = u[:, POOL_WIDTH + j * LANE:POOL_WIDTH + (j + 1) * LANE]

    it = lax.rem(pl.program_id(0), tiles_per_seq)

    @pl.when(it == 0)
    def _():
        ext_ref[:POOL_HIST, :] = jnp.zeros((POOL_HIST, POOL_WIDTH), F32)

    ext_ref[POOL_HIST:, :] = u[:, :POOL_WIDTH]
    pos = it * tc + lax.broadcasted_iota(jnp.int32, (tc, 1), 0)
    for g, w in enumerate(POOL_WINDOWS):
        sl = slice(g * POOL_GROUP, (g + 1) * POOL_GROUP)
        z = ext_ref[:, sl]
        s, span = z, 1
        while span < w:
            s = s + pltpu.roll(s, span, 0)
            span *= 2
        cnt = jnp.minimum(pos + 1, w).astype(F32)
        pooled = (s[POOL_HIST:] / cnt - z[POOL_HIST:]).astype(BF16)
        out = _dot(pooled, wp_ref[g]) * sc_ref[:, sl]
        for jj in range(POOL_GROUP // LANE):
            o_ref[g * (POOL_GROUP // LANE) + jj] = out[:, jj * LANE:(jj + 1) * LANE]
    last = ext_ref[tc:tc + POOL_HIST, :]
    tail_ref[0] = last
    ext_ref[:POOL_HIST, :] = last


POOL_LANE_BLOCKS = POOL_WIDTH // LANE


def _in_proj_pool(x, g, w, w_pool, scale, *, tc, rows_per_seq):
    rows = x.shape[0]
    tiles_per_seq = rows_per_seq // tc
    slab_spec = pl.BlockSpec((N_LANE_BLOCKS, tc, LANE), lambda i: (0, i, 0))
    slabs = jax.ShapeDtypeStruct((N_LANE_BLOCKS, rows, LANE), F32)
    return pl.pallas_call(
        functools.partial(_in_proj_pool_kernel, tc=tc, tiles_per_seq=tiles_per_seq),
        grid=(rows // tc,),
        in_specs=[pl.BlockSpec((tc, D_MODEL), lambda i: (i, 0)),
                  _const_spec((1, D_MODEL)),
                  _const_spec((D_MODEL, D_MODEL)),
                  _const_spec((len(POOL_WINDOWS), POOL_GROUP, POOL_GROUP)),
                  _const_spec((1, POOL_WIDTH))],
        out_specs=[slab_spec, slab_spec,
                   pl.BlockSpec((1, POOL_HIST, POOL_WIDTH), lambda i: (i // tiles_per_seq, 0, 0))],
        out_shape=[slabs, slabs,
                   jax.ShapeDtypeStruct((rows // rows_per_seq, POOL_HIST, POOL_WIDTH), F32)],
        scratch_shapes=[pltpu.VMEM((tc + POOL_HIST, POOL_WIDTH), F32)],
        compiler_params=_params(("arbitrary",), 48),
        name="in_proj_pool",
    )(x, g, w, w_pool, scale)


def _pool_buf_kernel(u_ref, buf_ref, wp_ref, sc_ref, o_ref, nb_ref, *, L):
    nseq = buf_ref.shape[1]
    per_group = POOL_GROUP // LANE

    def token_rows(t):
        return pl.ds(t, nseq, stride=L)

    def slab(idx, g):
        if idx < POOL_BUF:
            return buf_ref[idx, :, g * POOL_GROUP:(g + 1) * POOL_GROUP]
        return jnp.concatenate([u_ref[g * per_group + jj, token_rows(idx - POOL_BUF), :]
                                for jj in range(per_group)], axis=1)

    for g, w in enumerate(POOL_WINDOWS):
        sl = slice(g * POOL_GROUP, (g + 1) * POOL_GROUP)
        pooled = []
        for t in range(L):
            acc = slab(POOL_BUF + t, g)
            for back in range(1, w):
                acc = acc + slab(POOL_BUF + t - back, g)
            pooled.append((acc / float(w) - slab(POOL_BUF + t, g)).astype(BF16))
        out = _dot(jnp.concatenate(pooled, axis=0), wp_ref[g]) * sc_ref[:, sl]
        for t in range(L):
            for jj in range(per_group):
                o_ref[g * per_group + jj, token_rows(t), :] = out[t * nseq:(t + 1) * nseq,
                                                                  jj * LANE:(jj + 1) * LANE]
    keep = POOL_BUF - L
    for k in range(keep):
        nb_ref[k] = buf_ref[k + L]
    for t in range(L):
        nb_ref[keep + t] = _lane_blocks(u_ref, token_rows(t))


def _pool_mix_buf(u_blocks, buf_tm, w_pool, scale, *, L):
    rows = u_blocks.shape[1]
    slab_spec = pl.BlockSpec((POOL_LANE_BLOCKS, rows, LANE), lambda i: (0, 0, 0))
    buf_spec = pl.BlockSpec(buf_tm.shape, lambda i: (0, 0, 0))
    return pl.pallas_call(
        functools.partial(_pool_buf_kernel, L=L),
        grid=(1,),
        in_specs=[slab_spec, buf_spec, _const_spec((len(POOL_WINDOWS), POOL_GROUP, POOL_GROUP)),
                  _const_spec((1, POOL_WIDTH))],
        out_specs=[slab_spec, buf_spec],
        out_shape=[jax.ShapeDtypeStruct((POOL_LANE_BLOCKS, rows, LANE), F32),
                   jax.ShapeDtypeStruct(buf_tm.shape, F32)],
        compiler_params=_params(("arbitrary",), 48),
        name="pool_mix_buf",
    )(u_blocks, buf_tm, w_pool, scale)


def _cmul(ar, ai, br, bi):
    return ar * br - ai * bi, ar * bi + ai * br


def _cexp(lam_re, lam_im, log_step):
    delta = jnp.exp(log_step)
    mag = jnp.exp(lam_re * delta)
    ang = lam_im * delta
    return mag * jnp.cos(ang), mag * jnp.sin(ang)


def _ssm_prep_kernel(row_ref, col_ref, flat_ref, *out_refs, chunks):
    L = max(chunks)
    lr, li, ls, br, bi = (row_ref[i] for i in range(5))
    ar, ai = _cexp(lr, li, ls)
    den = lr * lr + li * li
    xr = ar - 1.0
    fr = (xr * lr + ai * li) / den
    fi = (ai * lr - xr * li) / den
    zs = [_cmul(fr, fi, br, bi)]
    for _ in range(1, L):
        zs.append(_cmul(ar, ai, *zs[-1]))

    lrc, lic, lsc, cr, ci = (col_ref[i] for i in range(5))
    acr, aci = _cexp(lrc, lic, lsc)
    xs = []
    cur = (cr, ci)
    for _ in range(L):
        cur = _cmul(acr, aci, *cur)
        xs.append(cur)

    def iota(shape, dim):
        return lax.broadcasted_iota(jnp.int32, shape, dim)

    same_tt = (iota((LANE, LANE), 0) >> 4) == (iota((LANE, LANE), 1) >> 4)
    c_stack = jnp.concatenate([cr, -ci], axis=0)
    zero_tile = jnp.zeros((LANE, LANE), BF16)
    lag = []
    for d in range(L):
        z_stack = jnp.concatenate([zs[d][0], zs[d][1]], axis=1)
        tile = jnp.dot(z_stack, c_stack, precision=HI, preferred_element_type=F32)
        lag.append(jnp.where(same_tt, tile, 0.0).astype(BF16))

    same_w = (iota((LANE, STATE_BLOCK), 0) >> 4) == (iota((LANE, STATE_BLOCK), 1) >> 6)
    w_tiles = []
    for zr, zi in zs:
        wr = jnp.where(same_w, jnp.concatenate([zr] * GROUPS_PER_BLOCK, axis=1), 0.0)
        wi = jnp.where(same_w, jnp.concatenate([zi] * GROUPS_PER_BLOCK, axis=1), 0.0)
        w_tiles.append(jnp.concatenate([wr, wi], axis=1).astype(BF16))

    same_v = (iota((STATE_BLOCK, LANE), 0) >> 6) == (iota((STATE_BLOCK, LANE), 1) >> 4)
    v_tiles = []
    for xr_t, xi_t in xs:
        vr = jnp.where(same_v, jnp.concatenate([xr_t] * GROUPS_PER_BLOCK, axis=0), 0.0)
        vi = jnp.where(same_v, jnp.concatenate([xi_t] * GROUPS_PER_BLOCK, axis=0), 0.0)
        v_tiles.append(jnp.concatenate([vr, -vi], axis=0).astype(BF16))

    flr, fli, fls = (flat_ref[i, 0] for i in range(3))
    far, fai = _cexp(flr, fli, fls)
    powers = [(far, fai)]
    for _ in range(L - 1):
        powers.append(_cmul(far, fai, *powers[-1]))

    for n, Lc in enumerate(chunks):
        t_ref, w_ref, v_ref, al_ref = out_refs[4 * n:4 * n + 4]
        for k in range(Lc):
            for t in range(Lc):
                t_ref[0, k * LANE:(k + 1) * LANE, t * LANE:(t + 1) * LANE] = lag[t - k] if t >= k else zero_tile
            w_ref[0, k * LANE:(k + 1) * LANE, :] = w_tiles[Lc - 1 - k]
        for t in range(Lc):
            v_ref[0, :, t * LANE:(t + 1) * LANE] = v_tiles[t]
        al_ref[0] = jnp.concatenate(powers[Lc - 1], axis=1)


def _ssm_prep(row5, col5, flat3, chunks):
    out_specs, out_shape = [], []
    for L in chunks:
        lk = L * LANE
        for shape, dtype in (((lk, lk), BF16), ((lk, 2 * STATE_BLOCK), BF16), ((2 * STATE_BLOCK, lk), BF16),
                             ((1, 2 * STATE_BLOCK), F32)):
            out_specs.append(pl.BlockSpec((1,) + shape, lambda j: (j, 0, 0)))
            out_shape.append(jax.ShapeDtypeStruct((N_LANE_BLOCKS,) + shape, dtype))
    outs = pl.pallas_call(
        functools.partial(_ssm_prep_kernel, chunks=tuple(chunks)),
        grid=(N_LANE_BLOCKS,),
        in_specs=[pl.BlockSpec((5, LANE, SSM_STATE), lambda j: (0, j, 0)),
                  pl.BlockSpec((5, SSM_STATE, LANE), lambda j: (0, 0, j)),
                  pl.BlockSpec((3, 1, 1, STATE_BLOCK), lambda j: (0, j, 0, 0))],
        out_specs=out_specs,
        out_shape=out_shape,
        compiler_params=_params(("arbitrary",), 40),
        name="ssm_prep",
    )(row5, col5, flat3)
    return {L: tuple(outs[4 * n:4 * n + 4]) for n, L in enumerate(chunks)}


def _ssm_kernel(u_ref, t_ref, w_ref, v_ref, al_ref, d_ref, h0_ref, g_ref, st_ref, *scratch, B, Rn, L, seq):
    M = B * Rn
    sb = STATE_BLOCK
    io_scr = scratch[0]
    for b in range(B):
        for k in range(L):
            io_scr[k, pl.ds(b, Rn, stride=B), :] = u_ref[0, b, pl.ds(k, Rn, stride=L), :]
    u = jnp.concatenate([io_scr[k] for k in range(L)], axis=1)
    ub = u.astype(BF16)
    e = _dot(ub, w_ref[0])
    ar = al_ref[0][:, :sb]
    ai = al_ref[0][:, sb:]

    def advance(s, eb):
        sr, si = s[:, :sb], s[:, sb:]
        return jnp.concatenate([ar * sr - ai * si + eb[:, :sb], ar * si + ai * sr + eb[:, sb:]], axis=1)

    if seq:
        e_scr, s_scr, c_scr = scratch[1:]

        assert 2 * B == 8 and Rn % 2 == 0, "two chunk rows of B sequences fill one sublane tile"

        @pl.when(pl.program_id(1) == 0)
        def _():
            c_scr[:B, :] = h0_ref[0]
            c_scr[B:, :] = jnp.zeros((B, 2 * sb), F32)

        e_scr[...] = e
        top = lax.broadcasted_iota(jnp.int32, (2 * B, 1), 0) < B

        def body(i, s):
            rows = pl.ds(pl.multiple_of(i * 2 * B, 2 * B), 2 * B)
            e2 = e_scr[rows, :]
            mid = pltpu.roll(advance(s, e2), B, 0)
            s_scr[rows, :] = jnp.where(top, s, mid)
            return pltpu.roll(advance(mid, e2), B, 0)

        c_scr[...] = lax.fori_loop(0, Rn // 2, body, c_scr[...], unroll=True)
        st_ref[0] = c_scr[:B, :]
        s_start = s_scr[...]
    else:
        s_start = h0_ref[0]
        st_ref[0] = advance(s_start, e)
    y = _dot(ub, t_ref[0]) + _dot(s_start.astype(BF16), v_ref[0]) + d_ref[0] * u
    g = jax.nn.gelu(y)
    for k in range(L):
        io_scr[k] = g[:, k * LANE:(k + 1) * LANE]
    for b in range(B):
        for k in range(L):
            g_ref[0, b, pl.ds(k, Rn, stride=L), :] = io_scr[k, pl.ds(b, Rn, stride=B), :]


def _ssm_mix(u_view, ops, d_tiles, h0, *, L, Rn, seq):
    t_op, w_op, v_op, a_l = ops
    nj = N_LANE_BLOCKS
    _, B, trows, _ = u_view.shape
    lk = L * LANE
    nseq = h0.shape[1]
    tiles = trows // (Rn * L)
    M = B * Rn
    sb2 = 2 * STATE_BLOCK
    first = u_view.shape[0] - nj
    u_spec = pl.BlockSpec((1, B, Rn * L, LANE), lambda j, i: (j + first, 0, i, 0))
    g_spec = pl.BlockSpec((1, B, Rn * L, LANE), lambda j, i: (j, 0, i, 0))
    scratch = [pltpu.VMEM((L, M, LANE), F32)]
    if seq:
        scratch += [pltpu.VMEM((M, sb2), F32), pltpu.VMEM((M, sb2), F32), pltpu.VMEM((2 * B, sb2), F32)]
    return pl.pallas_call(
        functools.partial(_ssm_kernel, B=B, Rn=Rn, L=L, seq=seq),
        grid=(nj, tiles),
        in_specs=[u_spec,
                  pl.BlockSpec((1, lk, lk), lambda j, i: (j, 0, 0)),
                  pl.BlockSpec((1, lk, sb2), lambda j, i: (j, 0, 0)),
                  pl.BlockSpec((1, sb2, lk), lambda j, i: (j, 0, 0)),
                  pl.BlockSpec((1, 1, sb2), lambda j, i: (j, 0, 0)),
                  pl.BlockSpec((1, 1, lk), lambda j, i: (j, 0, 0)),
                  pl.BlockSpec((1, nseq, sb2), lambda j, i: (j, 0, 0))],
        out_specs=[g_spec, pl.BlockSpec((1, nseq, sb2), lambda j, i: (j, 0, 0))],
        out_shape=[jax.ShapeDtypeStruct((nj,) + u_view.shape[1:], F32),
                   jax.ShapeDtypeStruct((nj, nseq, sb2), F32)],
        scratch_shapes=scratch,
        compiler_params=_params(("arbitrary", "arbitrary"), 48),
        name="ssm_mix_seq" if seq else "ssm_mix_rows",
    )(u_view, t_op, w_op, v_op, a_l, d_tiles, h0)


def _mix_out_kernel(g_ref, po_ref, x_ref, wglu_ref, bglu_ref, wout_ref, gc_ref, wq_ref, *rest, attend):
    if attend:
        k_ref, v_ref, x1_ref, q_ref = rest
    else:
        x1_ref, q_ref = rest
    g = _lane_blocks(g_ref)
    gate = jax.nn.sigmoid(_dot(g.astype(BF16), wglu_ref[...]) + bglu_ref[...])
    ssm_out = (g * gate).astype(BF16)
    mix = (_dot(_lane_blocks(po_ref).astype(BF16), wout_ref[:POOL_WIDTH, :])
           + _dot(ssm_out, wout_ref[POOL_WIDTH:, :]))
    x1 = x_ref[...] + mix
    x1_ref[...] = x1
    q = _dot(_rms(x1, gc_ref[...]).astype(BF16), wq_ref[...]).astype(BF16)
    if attend:
        for h in range(N_XHEADS):
            sl = slice(h * XHEAD_DIM, (h + 1) * XHEAD_DIM)
            s = lax.dot_general(q[:, sl], k_ref[:, sl], _NT, preferred_element_type=F32)
            p = _softmax_rows(s * (XHEAD_DIM ** -0.5))
            q_ref[:, sl] = _dot(p.astype(BF16), v_ref[:, sl]).astype(BF16)
    else:
        q_ref[...] = q


def _mix_out(g_rows, pool_out, x, w_glu, b_glu, w_out, g_cross, w_q, tm, mem_kv=None, rows_per_seq=None):
    rows = x.shape[0]
    attend = mem_kv is not None
    extra_specs, extra = [], []
    if attend:
        tiles = rows_per_seq // tm
        kv_spec = pl.BlockSpec((N_MEM, D_MODEL), lambda i: (i // tiles, 0))
        extra_specs, extra = [kv_spec, kv_spec], list(mem_kv)
    return pl.pallas_call(
        functools.partial(_mix_out_kernel, attend=attend),
        grid=(rows // tm,),
        in_specs=[pl.BlockSpec((N_LANE_BLOCKS, tm, LANE), lambda i: (0, i, 0)),
                  pl.BlockSpec((POOL_LANE_BLOCKS, tm, LANE), lambda i: (0, i, 0)),
                  pl.BlockSpec((tm, D_MODEL), lambda i: (i, 0)),
                  _const_spec((SSM_WIDTH, SSM_WIDTH)), _const_spec((1, SSM_WIDTH)),
                  _const_spec((D_MODEL, D_MODEL)), _const_spec((1, D_MODEL)),
                  _const_spec((D_MODEL, D_MODEL))] + extra_specs,
        out_specs=[pl.BlockSpec((tm, D_MODEL), lambda i: (i, 0)),
                   pl.BlockSpec((tm, D_MODEL), lambda i: (i, 0))],
        out_shape=[jax.ShapeDtypeStruct((rows, D_MODEL), F32),
                   jax.ShapeDtypeStruct((rows, D_MODEL), BF16)],
        compiler_params=_params(("arbitrary",), 56),
        name="mix_out_attn" if attend else "mix_out",
    )(g_rows, pool_out, x, w_glu, b_glu, w_out, g_cross, w_q, *extra)


HEAD_LANE_BLOCKS = XHEAD_DIM // LANE
CACHE_ROW_PITCH = N_XHEADS * HEAD_LANE_BLOCKS


def _cache_rows_view(cache):
    S = cache.shape[0]
    c5 = cache.reshape(S, N_MEM, N_XHEADS, HEAD_LANE_BLOCKS, LANE)
    return c5.transpose(0, 1, 3, 2, 4).reshape(S, N_MEM * CACHE_ROW_PITCH, LANE)


def _cache_rows_unview(rows, nseq):
    r5 = rows.reshape(nseq, N_MEM, HEAD_LANE_BLOCKS, N_XHEADS, LANE)
    return r5.transpose(0, 1, 3, 2, 4).reshape(nseq, N_MEM, N_XHEADS, XHEAD_DIM)


def _mem_kv_kernel(m_ref, g_ref, w_ref, rows_ref, kvb_ref):
    tm = m_ref.shape[0]
    kv = _dot(_rms(m_ref[...], g_ref[...]).astype(BF16), w_ref[...].astype(BF16))
    kvb_ref[...] = kv.astype(BF16)
    for c in range(HEAD_LANE_BLOCKS):
        for h in range(N_XHEADS):
            lo = h * XHEAD_DIM + c * LANE
            rows_ref[pl.ds(c * N_XHEADS + h, tm, stride=CACHE_ROW_PITCH), :] = kv[:, lo:lo + LANE]


def _mem_kv(mem, g_mem, w, tm):
    rows = mem.shape[0]
    return pl.pallas_call(
        _mem_kv_kernel,
        grid=(rows // tm,),
        in_specs=[pl.BlockSpec((tm, D_MODEL), lambda i: (i, 0)),
                  _const_spec((1, D_MODEL)),
                  _const_spec((D_MODEL, D_MODEL))],
        out_specs=[pl.BlockSpec((tm * CACHE_ROW_PITCH, LANE), lambda i: (i, 0)),
                   pl.BlockSpec((tm, D_MODEL), lambda i: (i, 0))],
        out_shape=[jax.ShapeDtypeStruct((rows * CACHE_ROW_PITCH, LANE), F32),
                   jax.ShapeDtypeStruct((rows, D_MODEL), BF16)],
        compiler_params=_params(("arbitrary",), 48),
        name="mem_kv",
    )(mem, g_mem, w)


_NT = (((1,), (1,)), ((), ()))


def _softmax_rows(s):
    e = jnp.exp(s - jnp.max(s, axis=-1, keepdims=True))
    return e / jnp.sum(e, axis=-1, keepdims=True)


def _attn_cache_kernel(q_ref, k_ref, v_ref, o_ref, *, bb):
    nc = HEAD_LANE_BLOCKS
    mc = N_MEM * nc
    lane_c = lax.broadcasted_iota(jnp.int32, (8, mc), 1) & (nc - 1)
    owns = []
    for b in range(bb):
        q8 = q_ref[b]
        for h in range(N_XHEADS):
            xk = k_ref[b, pl.ds(h, mc, stride=N_XHEADS), :].astype(BF16)
            qc = jnp.concatenate([q8[:, h * XHEAD_DIM + c * LANE:h * XHEAD_DIM + (c + 1) * LANE]
                                  for c in range(nc)], axis=0).astype(BF16)
            part = lax.dot_general(qc, xk, _NT, preferred_element_type=F32)
            own = part[8 * (nc - 1):]
            for c in range(nc - 2, -1, -1):
                own = jnp.where(lane_c == c, part[8 * c:8 * (c + 1)], own)
            owns.append(own)
    own_all = jnp.concatenate(owns, axis=0)
    lane_all = lax.broadcasted_iota(jnp.int32, own_all.shape, 1) & (nc - 1)
    pair = own_all + pltpu.roll(own_all, 1, 1)
    full = pair + pltpu.roll(pair, 2, 1)
    s = jnp.where(lane_all == nc - 1, full * (XHEAD_DIM ** -0.5), -1e30)
    e = jnp.exp(s - jnp.max(s, axis=1, keepdims=True))
    p = e / jnp.sum(e, axis=1, keepdims=True)
    p2 = p + pltpu.roll(p, mc - 1, 1)
    p4 = p2 + pltpu.roll(p2, mc - 2, 1)
    for b in range(bb):
        xv = jnp.concatenate([v_ref[b, pl.ds(h, mc, stride=N_XHEADS), :] for h in range(N_XHEADS)],
                             axis=1).astype(BF16)
        pm_rows = []
        for h in range(N_XHEADS):
            r0 = 8 * (b * N_XHEADS + h)
            pm_rows += [jnp.where(lane_c == c, p4[r0:r0 + 8], 0.0) for c in range(nc)]
        o_all = _dot(jnp.concatenate(pm_rows, axis=0).astype(BF16), xv)
        for h in range(N_XHEADS):
            for c in range(nc):
                g = h * nc + c
                o_ref[b, :, h * XHEAD_DIM + c * LANE:h * XHEAD_DIM + (c + 1) * LANE] = (
                    o_all[8 * g:8 * (g + 1), h * LANE:(h + 1) * LANE])


def _attn_cache(q8, k_rows, v_rows, bb):
    nseq = q8.shape[0]
    rows = N_MEM * CACHE_ROW_PITCH
    return pl.pallas_call(
        functools.partial(_attn_cache_kernel, bb=bb),
        grid=(nseq // bb,),
        in_specs=[pl.BlockSpec((bb, 8, D_MODEL), lambda i: (i, 0, 0)),
                  pl.BlockSpec((bb, rows, LANE), lambda i: (i, 0, 0)),
                  pl.BlockSpec((bb, rows, LANE), lambda i: (i, 0, 0))],
        out_specs=pl.BlockSpec((bb, 8, D_MODEL), lambda i: (i, 0, 0)),
        out_shape=jax.ShapeDtypeStruct((nseq, 8, D_MODEL), F32),
        compiler_params=_params(("arbitrary",), 48),
        name="attn_cache",
    )(q8, k_rows, v_rows)


def _oproj_kernel(o_ref, x1_ref, wo_ref, gf_ref, x2_ref, h_ref):
    x2 = x1_ref[...] + _dot(o_ref[...], wo_ref[...])
    x2_ref[...] = x2
    h_ref[...] = _rms(x2, gf_ref[...]).astype(BF16)


def _oproj(o, x1, w_o, g_ffn, tm):
    rows = x1.shape[0]
    row_spec = pl.BlockSpec((tm, D_MODEL), lambda i: (i, 0))
    return pl.pallas_call(
        _oproj_kernel,
        grid=(rows // tm,),
        in_specs=[row_spec, row_spec, _const_spec((D_MODEL, D_MODEL)), _const_spec((1, D_MODEL))],
        out_specs=[row_spec, row_spec],
        out_shape=[jax.ShapeDtypeStruct((rows, D_MODEL), F32),
                   jax.ShapeDtypeStruct((rows, D_MODEL), BF16)],
        compiler_params=_params(("arbitrary",), 40),
        name="oproj",
    )(o, x1, w_o, g_ffn)


def _ffn_kernel(x_ref, h_ref, wg_ref, wu_ref, wd_ref, gl_ref, y_ref, *bf16_out):
    f = pl.program_id(1)

    @pl.when(f == 0)
    def _():
        y_ref[...] = jnp.zeros_like(y_ref)

    wg, wu, wd = (w[...].astype(BF16) for w in (wg_ref, wu_ref, wd_ref))
    for ref, w in zip(bf16_out, (wg, wu, wd)):
        ref[...] = w
    h = h_ref[...]
    z = (jax.nn.silu(_dot(h, wg)) * _dot(h, wu)).astype(BF16)
    y_ref[...] += _dot(z, wd)

    @pl.when(f == pl.num_programs(1) - 1)
    def _():
        y_ref[...] = _rms(x_ref[...] + y_ref[...], gl_ref[...])


def _ffn(x2, h, w_gate, w_up, w_down, g_final, tm, tf, emit_bf16=False):
    rows = x2.shape[0]
    nf = D_FF // tf
    row_spec = pl.BlockSpec((tm, D_MODEL), lambda i, f: (i, 0))
    out_specs = [pl.BlockSpec((tm, D_MODEL), lambda i, f: (i, 0), pipeline_mode=pl.Buffered(1))]
    out_shape = [jax.ShapeDtypeStruct((rows, D_MODEL), F32)]
    if emit_bf16:
        once = lambda i, f: jnp.where(i == 0, f, nf)
        out_specs += [pl.BlockSpec((D_MODEL, tf), lambda i, f: (0, once(i, f))),
                      pl.BlockSpec((D_MODEL, tf), lambda i, f: (0, once(i, f))),
                      pl.BlockSpec((tf, D_MODEL), lambda i, f: (once(i, f), 0))]
        out_shape += [jax.ShapeDtypeStruct((D_MODEL, D_FF + tf), BF16),
                      jax.ShapeDtypeStruct((D_MODEL, D_FF + tf), BF16),
                      jax.ShapeDtypeStruct((D_FF + tf, D_MODEL), BF16)]
    return pl.pallas_call(
        _ffn_kernel,
        grid=(rows // tm, nf),
        in_specs=[row_spec, row_spec,
                  pl.BlockSpec((D_MODEL, tf), lambda i, f: (0, f)),
                  pl.BlockSpec((D_MODEL, tf), lambda i, f: (0, f)),
                  pl.BlockSpec((tf, D_MODEL), lambda i, f: (f, 0)),
                  _const_spec((1, D_MODEL))],
        out_specs=out_specs,
        out_shape=out_shape,
        compiler_params=_params(("arbitrary", "arbitrary"), 60 if emit_bf16 else 56),
        name="ffn",
    )(x2, h, w_gate, w_up, w_down, g_final)


def _ssm_param_layouts(lam_re, lam_im, log_step, b_re, b_im, c_re, c_im):
    G, P = lam_re.shape
    ls = jnp.broadcast_to(log_step[:, None], (G, P))
    rep = lambda a: jnp.repeat(a, SSM_GROUP, axis=0)
    b_rows = lambda b: jnp.swapaxes(b, 1, 2).reshape(G * SSM_GROUP, P)
    row5 = jnp.stack([rep(lam_re), rep(lam_im), rep(ls), b_rows(b_re), b_rows(b_im)])
    col5 = jnp.stack([rep(lam_re).T, rep(lam_im).T, rep(ls).T,
                      c_re.reshape(G * SSM_GROUP, P).T, c_im.reshape(G * SSM_GROUP, P).T])
    flat = lambda a: a.reshape(N_LANE_BLOCKS, 1, STATE_BLOCK)
    flat3 = jnp.stack([flat(lam_re), flat(lam_im), flat(ls)])
    return row5, col5, flat3


def _states_to_blocks(h_re, h_im):
    S = h_re.shape[0]
    blk = lambda h: h.reshape(S, N_LANE_BLOCKS, STATE_BLOCK).transpose(1, 0, 2)
    return jnp.concatenate([blk(h_re), blk(h_im)], axis=-1)


def _blocks_to_states(st):
    S = st.shape[1]
    unblk = lambda a: a.transpose(1, 0, 2).reshape(1, S, N_SSM_GROUPS, SSM_STATE)
    return unblk(st[:, :, :STATE_BLOCK]), unblk(st[:, :, STATE_BLOCK:])


def kernel(x_prompt, x_sample, mem_prompt, state_pool_buf, state_ssm_re, state_ssm_im, cache_mem_k, cache_mem_v, g_mix, w_in, w_pool, pool_scale, ssm_lam_re, ssm_lam_im, ssm_log_step, ssm_b_re, ssm_b_im, ssm_c_re, ssm_c_im, ssm_d, w_glu, b_glu, w_out, g_cross, g_mem, w_q, w_k, w_v, w_o, g_ffn, w_gate, w_up, w_down, g_final):
    assert g_mix.shape[0] == 1, "single-layer step"
    B, T, _ = x_prompt.shape
    S, Ts, _ = x_sample.shape
    bf = lambda w: w.astype(BF16)
    vec = lambda v: v.reshape(1, -1)

    w_in_b, w_pool_b, w_glu_b, w_out_b = bf(w_in[0]), bf(w_pool[0]), bf(w_glu[0]), bf(w_out[0])
    w_q_b, w_o_b = bf(w_q[0]), bf(w_o[0])

    row5, col5, flat3 = _ssm_param_layouts(ssm_lam_re[0], ssm_lam_im[0], ssm_log_step[0],
                                           ssm_b_re[0], ssm_b_im[0], ssm_c_re[0], ssm_c_im[0])
    d_blocks = ssm_d[0].reshape(N_LANE_BLOCKS, 1, LANE)

    gains = dict(g_mix=vec(g_mix[0]), scale=vec(pool_scale[0]), b_glu=vec(b_glu[0]), g_cross=vec(g_cross[0]),
                 g_ffn=vec(g_ffn[0]), g_final=vec(g_final))

    chunk_ops = _ssm_prep(row5, col5, flat3, sorted({PROMPT_CHUNK, Ts}, reverse=True))

    def ssm(u_view, L, h0_blocks, **tiles):
        g_act, st = _ssm_mix(u_view, chunk_ops[L], jnp.tile(d_blocks, (1, 1, L)), h0_blocks, L=L, **tiles)
        rows = u_view.shape[1] * u_view.shape[2]
        return g_act.reshape(N_LANE_BLOCKS, rows, LANE), _blocks_to_states(st)

    def finish(o, x1, ffn_weights, ffn_tiles, **kw):
        x2, hf = _oproj(o, x1, w_o_b, gains["g_ffn"], ROW_TILE)
        return _ffn(x2, hf, *ffn_weights, gains["g_final"], *ffn_tiles, **kw)

    mem = mem_prompt.reshape(B * N_MEM, D_MODEL)
    k_rows, k_b = _mem_kv(mem, vec(g_mem[0]), w_k[0], ROW_TILE)
    v_rows, v_b = _mem_kv(mem, vec(g_mem[0]), w_v[0], ROW_TILE)
    xp = x_prompt.reshape(B * T, D_MODEL)
    us_p, po_p, tail_p = _in_proj_pool(xp, gains["g_mix"], w_in_b, w_pool_b, gains["scale"],
                                       tc=ROW_TILE, rows_per_seq=T)
    h0_p = jnp.zeros((N_LANE_BLOCKS, B, 2 * STATE_BLOCK), F32)
    g_p, (re_p, im_p) = ssm(us_p.reshape(N_LANE_BLOCKS, B, T, LANE), PROMPT_CHUNK, h0_p,
                            Rn=PROMPT_CHUNK_ROWS, seq=True)
    x1_p, o_p = _mix_out(g_p, po_p, xp, w_glu_b, gains["b_glu"], w_out_b, gains["g_cross"], w_q_b, ROW_TILE,
                         mem_kv=(k_b, v_b), rows_per_seq=T)
    yp, *ffn_bf16 = finish(o_p, x1_p, (w_gate[0], w_up[0], w_down[0]), PROMPT_FFN_TILES, emit_bf16=True)
    pb_p = tail_p[:, POOL_HIST - POOL_BUF:]

    assert PAST_LEN >= POOL_BUF
    xs = x_sample.reshape(S * Ts, D_MODEL)
    u_s = _in_proj(xs, gains["g_mix"], w_in_b, ROW_TILE)
    po_s, buf_tm = _pool_mix_buf(u_s, state_pool_buf[0].transpose(1, 0, 2), w_pool_b, gains["scale"], L=Ts)
    pb_s = buf_tm.transpose(1, 0, 2)
    h0_s = _states_to_blocks(state_ssm_re[0], state_ssm_im[0])
    g_s, (re_s, im_s) = ssm(u_s.reshape(MIX_LANE_BLOCKS, 1, S * Ts, LANE), Ts, h0_s, Rn=S, seq=False)
    x1_s, q_s = _mix_out(g_s, po_s, xs, w_glu_b, gains["b_glu"], w_out_b, gains["g_cross"], w_q_b, ROW_TILE)
    q8 = jnp.pad(q_s.reshape(S, Ts, D_MODEL).astype(F32), ((0, 0), (0, 8 - Ts), (0, 0)))
    o8 = _attn_cache(q8, _cache_rows_view(cache_mem_k[0]), _cache_rows_view(cache_mem_v[0]), CACHE_SEQS_PER_STEP)
    (ys,) = finish(o8[:, :Ts].reshape(S * Ts, D_MODEL).astype(BF16), x1_s, ffn_bf16, SAMPLE_FFN_TILES)

    mk = _cache_rows_unview(k_rows, B)[None]
    mv = _cache_rows_unview(v_rows, B)[None]
    return (yp.reshape(B, T, D_MODEL), ys.reshape(S, Ts, D_MODEL), pb_p[None], re_p, im_p, mk, mv,
            pb_s[None], re_s, im_s)
```

```python
import functools

import jax
import jax.numpy as jnp
from jax import lax
from jax.experimental import pallas as pl
from jax.experimental.pallas import tpu as pltpu

F32 = jnp.float32
BF16 = jnp.bfloat16

D_MODEL = 2048
POOL_WIDTH = 1024
SSM_WIDTH = 1024
POOL_WINDOWS = (2, 4, 8, 16)
POOL_GROUP = POOL_WIDTH // len(POOL_WINDOWS)
POOL_BUF = max(POOL_WINDOWS) - 1
SSM_GROUP = 16
N_SSM_GROUPS = SSM_WIDTH // SSM_GROUP
SSM_STATE = 64
N_MEM = 256
N_XHEADS = 4
XHEAD_DIM = D_MODEL // N_XHEADS
D_FF = 5632
EPS = 1e-6
PAST_LEN = 16384

LANE = 128
N_LANE_BLOCKS = SSM_WIDTH // LANE
GROUPS_PER_BLOCK = LANE // SSM_GROUP
STATE_BLOCK = GROUPS_PER_BLOCK * SSM_STATE
MIB = 1024 * 1024
HI = lax.Precision.HIGHEST

ROW_TILE = 512
PROMPT_CHUNK = 8
PROMPT_CHUNK_ROWS = 128
PROMPT_LANE_BLOCKS_PER_STEP = 2
PROMPT_FFN_TILES = (1024, 256)
SAMPLE_FFN_TILES = (512, 512)
CACHE_SEQS_PER_STEP = 4


def _params(semantics, vmem_mib):
    return pltpu.CompilerParams(dimension_semantics=semantics, vmem_limit_bytes=vmem_mib * MIB)


def _const_spec(shape):
    return pl.BlockSpec(shape, lambda *_: (0,) * len(shape), pipeline_mode=pl.Buffered(1))


def _rms(x, g):
    r = lax.rsqrt(jnp.mean(x * x, axis=-1, keepdims=True) + EPS)
    return x * r * g


def _dot(a, b):
    return jnp.dot(a, b, preferred_element_type=F32)


MIX_LANE_BLOCKS = D_MODEL // LANE


def _in_proj_kernel(x_ref, g_ref, w_ref, u_ref):
    h = _rms(x_ref[...], g_ref[...]).astype(BF16)
    u = _dot(h, w_ref[...])
    for j in range(MIX_LANE_BLOCKS):
        u_ref[j] = u[:, j * LANE:(j + 1) * LANE]


def _in_proj(x, g, w, tm):
    rows = x.shape[0]
    return pl.pallas_call(
        _in_proj_kernel,
        grid=(rows // tm,),
        in_specs=[pl.BlockSpec((tm, D_MODEL), lambda i: (i, 0)),
                  _const_spec((1, D_MODEL)),
                  _const_spec((D_MODEL, D_MODEL))],
        out_specs=pl.BlockSpec((MIX_LANE_BLOCKS, tm, LANE), lambda i: (0, i, 0)),
        out_shape=jax.ShapeDtypeStruct((MIX_LANE_BLOCKS, rows, LANE), F32),
        compiler_params=_params(("arbitrary",), 40),
        name="in_proj",
    )(x, g, w)


def _lane_blocks(ref, rows=slice(None)):
    return jnp.concatenate([ref[j, rows, :] for j in range(ref.shape[0])], axis=1)


POOL_HIST = POOL_BUF + 1


def _in_proj_pool_kernel(x_ref, g_ref, w_ref, wp_ref, sc_ref, us_ref, o_ref, tail_ref, ext_ref, *, tc,
                         tiles_per_seq):
    u = _dot(_rms(x_ref[...], g_ref[...]).astype(BF16), w_ref[...])
    for j in range(N_LANE_BLOCKS):
        us_ref[j] = u[:, POOL_WIDTH + j * LANE:POOL_WIDTH + (j + 1) * LANE]

    it = lax.rem(pl.program_id(0), tiles_per_seq)

    @pl.when(it == 0)
    def _():
        ext_ref[:POOL_HIST, :] = jnp.zeros((POOL_HIST, POOL_WIDTH), F32)

    ext_ref[POOL_HIST:, :] = u[:, :POOL_WIDTH]
    pos = it * tc + lax.broadcasted_iota(jnp.int32, (tc, 1), 0)
    for g, w in enumerate(POOL_WINDOWS):
        sl = slice(g * POOL_GROUP, (g + 1) * POOL_GROUP)
        z = ext_ref[:, sl]
        s, span = z, 1
        while span < w:
            s = s + pltpu.roll(s, span, 0)
            span *= 2
        cnt = jnp.minimum(pos + 1, w).astype(F32)
        pooled = (s[POOL_HIST:] / cnt - z[POOL_HIST:]).astype(BF16)
        out = _dot(pooled, wp_ref[g]) * sc_ref[:, sl]
        for jj in range(POOL_GROUP // LANE):
            o_ref[g * (POOL_GROUP // LANE) + jj] = out[:, jj * LANE:(jj + 1) * LANE]
    last = ext_ref[tc:tc + POOL_HIST, :]
    tail_ref[0] = last
    ext_ref[:POOL_HIST, :] = last


POOL_LANE_BLOCKS = POOL_WIDTH // LANE


def _in_proj_pool(x, g, w, w_pool, scale, *, tc, rows_per_seq):
    rows = x.shape[0]
    tiles_per_seq = rows_per_seq // tc
    slab_spec = pl.BlockSpec((N_LANE_BLOCKS, tc, LANE), lambda i: (0, i, 0))
    slabs = jax.ShapeDtypeStruct((N_LANE_BLOCKS, rows, LANE), F32)
    return pl.pallas_call(
        functools.partial(_in_proj_pool_kernel, tc=tc, tiles_per_seq=tiles_per_seq),
        grid=(rows // tc,),
        in_specs=[pl.BlockSpec((tc, D_MODEL), lambda i: (i, 0)),
                  _const_spec((1, D_MODEL)),
                  _const_spec((D_MODEL, D_MODEL)),
                  _const_spec((len(POOL_WINDOWS), POOL_GROUP, POOL_GROUP)),
                  _const_spec((1, POOL_WIDTH))],
        out_specs=[slab_spec, slab_spec,
                   pl.BlockSpec((1, POOL_HIST, POOL_WIDTH), lambda i: (i // tiles_per_seq, 0, 0))],
        out_shape=[slabs, slabs,
                   jax.ShapeDtypeStruct((rows // rows_per_seq, POOL_HIST, POOL_WIDTH), F32)],
        scratch_shapes=[pltpu.VMEM((tc + POOL_HIST, POOL_WIDTH), F32)],
        compiler_params=_params(("arbitrary",), 48),
        name="in_proj_pool",
    )(x, g, w, w_pool, scale)


def _pool_buf_kernel(u_ref, buf_ref, wp_ref, sc_ref, o_ref, nb_ref, *, L):
    nseq = buf_ref.shape[1]
    per_group = POOL_GROUP // LANE

    def token_rows(t):
        return pl.ds(t, nseq, stride=L)

    def slab(idx, g):
        if idx < POOL_BUF:
            return buf_ref[idx, :, g * POOL_GROUP:(g + 1) * POOL_GROUP]
        return jnp.concatenate([u_ref[g * per_group + jj, token_rows(idx - POOL_BUF), :]
                                for jj in range(per_group)], axis=1)

    for g, w in enumerate(POOL_WINDOWS):
        sl = slice(g * POOL_GROUP, (g + 1) * POOL_GROUP)
        pooled = []
        for t in range(L):
            acc = slab(POOL_BUF + t, g)
            for back in range(1, w):
                acc = acc + slab(POOL_BUF + t - back, g)
            pooled.append((acc / float(w) - slab(POOL_BUF + t, g)).astype(BF16))
        out = _dot(jnp.concatenate(pooled, axis=0), wp_ref[g]) * sc_ref[:, sl]
        for t in range(L):
            for jj in range(per_group):
                o_ref[g * per_group + jj, token_rows(t), :] = out[t * nseq:(t + 1) * nseq,
                                                                  jj * LANE:(jj + 1) * LANE]
    keep = POOL_BUF - L
    for k in range(keep):
        nb_ref[k] = buf_ref[k + L]
    for t in range(L):
        nb_ref[keep + t] = _lane_blocks(u_ref, token_rows(t))


def _pool_mix_buf(u_blocks, buf_tm, w_pool, scale, *, L):
    rows = u_blocks.shape[1]
    slab_spec = pl.BlockSpec((POOL_LANE_BLOCKS, rows, LANE), lambda i: (0, 0, 0))
    buf_spec = pl.BlockSpec(buf_tm.shape, lambda i: (0, 0, 0))
    return pl.pallas_call(
        functools.partial(_pool_buf_kernel, L=L),
        grid=(1,),
        in_specs=[slab_spec, buf_spec, _const_spec((len(POOL_WINDOWS), POOL_GROUP, POOL_GROUP)),
                  _const_spec((1, POOL_WIDTH))],
        out_specs=[slab_spec, buf_spec],
        out_shape=[jax.ShapeDtypeStruct((POOL_LANE_BLOCKS, rows, LANE), F32),
                   jax.ShapeDtypeStruct(buf_tm.shape, F32)],
        compiler_params=_params(("arbitrary",), 48),
        name="pool_mix_buf",
    )(u_blocks, buf_tm, w_pool, scale)


def _cmul(ar, ai, br, bi):
    return ar * br - ai * bi, ar * bi + ai * br


def _cexp(lam_re, lam_im, log_step):
    delta = jnp.exp(log_step)
    mag = jnp.exp(lam_re * delta)
    ang = lam_im * delta
    return mag * jnp.cos(ang), mag * jnp.sin(ang)


def _ssm_prep_kernel(row_ref, col_ref, flat_ref, *out_refs, chunks):
    L = max(chunks)
    lr, li, ls, br, bi = (row_ref[i] for i in range(5))
    ar, ai = _cexp(lr, li, ls)
    den = lr * lr + li * li
    xr = ar - 1.0
    fr = (xr * lr + ai * li) / den
    fi = (ai * lr - xr * li) / den
    zs = [_cmul(fr, fi, br, bi)]
    for _ in range(1, L):
        zs.append(_cmul(ar, ai, *zs[-1]))

    lrc, lic, lsc, cr, ci = (col_ref[i] for i in range(5))
    acr, aci = _cexp(lrc, lic, lsc)
    xs = []
    cur = (cr, ci)
    for _ in range(L):
        cur = _cmul(acr, aci, *cur)
        xs.append(cur)

    def iota(shape, dim):
        return lax.broadcasted_iota(jnp.int32, shape, dim)

    same_tt = (iota((LANE, LANE), 0) >> 4) == (iota((LANE, LANE), 1) >> 4)
    c_stack = jnp.concatenate([cr, -ci], axis=0)
    zero_tile = jnp.zeros((LANE, LANE), BF16)
    lag = []
    for d in range(L):
        z_stack = jnp.concatenate([zs[d][0], zs[d][1]], axis=1)
        tile = jnp.dot(z_stack, c_stack, precision=HI, preferred_element_type=F32)
        lag.append(jnp.where(same_tt, tile, 0.0).astype(BF16))

    same_w = (iota((LANE, STATE_BLOCK), 0) >> 4) == (iota((LANE, STATE_BLOCK), 1) >> 6)
    w_tiles = []
    for zr, zi in zs:
        wr = jnp.where(same_w, jnp.concatenate([zr] * GROUPS_PER_BLOCK, axis=1), 0.0)
        wi = jnp.where(same_w, jnp.concatenate([zi] * GROUPS_PER_BLOCK, axis=1), 0.0)
        w_tiles.append(jnp.concatenate([wr, wi], axis=1).astype(BF16))

    same_v = (iota((STATE_BLOCK, LANE), 0) >> 6) == (iota((STATE_BLOCK, LANE), 1) >> 4)
    v_tiles = []
    for xr_t, xi_t in xs:
        vr = jnp.where(same_v, jnp.concatenate([xr_t] * GROUPS_PER_BLOCK, axis=0), 0.0)
        vi = jnp.where(same_v, jnp.concatenate([xi_t] * GROUPS_PER_BLOCK, axis=0), 0.0)
        v_tiles.append(jnp.concatenate([vr, -vi], axis=0).astype(BF16))

    flr, fli, fls = (flat_ref[i, 0] for i in range(3))
    far, fai = _cexp(flr, fli, fls)
    powers = [(far, fai)]
    for _ in range(L - 1):
        powers.append(_cmul(far, fai, *powers[-1]))

    for n, Lc in enumerate(chunks):
        t_ref, w_ref, v_ref, al_ref = out_refs[4 * n:4 * n + 4]
        for k in range(Lc):
            for t in range(Lc):
                t_ref[0, k * LANE:(k + 1) * LANE, t * LANE:(t + 1) * LANE] = lag[t - k] if t >= k else zero_tile
            w_ref[0, k * LANE:(k + 1) * LANE, :] = w_tiles[Lc - 1 - k]
        for t in range(Lc):
            v_ref[0, :, t * LANE:(t + 1) * LANE] = v_tiles[t]
        al_ref[0] = jnp.concatenate(powers[Lc - 1], axis=1)


def _ssm_prep(row5, col5, flat3, chunks):
    out_specs, out_shape = [], []
    for L in chunks:
        lk = L * LANE
        for shape, dtype in (((lk, lk), BF16), ((lk, 2 * STATE_BLOCK), BF16), ((2 * STATE_BLOCK, lk), BF16),
                             ((1, 2 * STATE_BLOCK), F32)):
            out_specs.append(pl.BlockSpec((1,) + shape, lambda j: (j, 0, 0)))
            out_shape.append(jax.ShapeDtypeStruct((N_LANE_BLOCKS,) + shape, dtype))
    outs = pl.pallas_call(
        functools.partial(_ssm_prep_kernel, chunks=tuple(chunks)),
        grid=(N_LANE_BLOCKS,),
        in_specs=[pl.BlockSpec((5, LANE, SSM_STATE), lambda j: (0, j, 0)),
                  pl.BlockSpec((5, SSM_STATE, LANE), lambda j: (0, 0, j)),
                  pl.BlockSpec((3, 1, 1, STATE_BLOCK), lambda j: (0, j, 0, 0))],
        out_specs=out_specs,
        out_shape=out_shape,
        compiler_params=_params(("arbitrary",), 40),
        name="ssm_prep",
    )(row5, col5, flat3)
    return {L: tuple(outs[4 * n:4 * n + 4]) for n, L in enumerate(chunks)}


def _ssm_kernel(u_ref, t_ref, w_ref, v_ref, al_ref, d_ref, h0_ref, g_ref, st_ref, *scratch, B, Rn, L, seq, jb):
    M = B * Rn
    sb = STATE_BLOCK
    io_scr = scratch[0]

    def advance(a, s, eb):
        ar, ai = a
        sr, si = s[:, :sb], s[:, sb:]
        return jnp.concatenate([ar * sr - ai * si + eb[:, :sb], ar * si + ai * sr + eb[:, sb:]], axis=1)

    es, local, decay = [], [], []
    for n in range(jb):
        for b in range(B):
            for k in range(L):
                io_scr[n, k, pl.ds(b, Rn, stride=B), :] = u_ref[n, b, pl.ds(k, Rn, stride=L), :]
        u = jnp.concatenate([io_scr[n, k] for k in range(L)], axis=1)
        ub = u.astype(BF16)
        es.append(_dot(ub, w_ref[n]))
        local.append(_dot(ub, t_ref[n]) + d_ref[n] * u)
        decay.append((al_ref[n][:, :sb], al_ref[n][:, sb:]))

    starts = []
    if seq:
        e_scr, s_scr, c_scr = scratch[1:]
        assert 2 * B == 8 and Rn % 2 == 0, "two chunk rows of B sequences fill one sublane tile"

        @pl.when(pl.program_id(1) == 0)
        def _():
            for n in range(jb):
                c_scr[n, :B, :] = h0_ref[n]
                c_scr[n, B:, :] = jnp.zeros((B, 2 * sb), F32)

        top = lax.broadcasted_iota(jnp.int32, (2 * B, 1), 0) < B
        for n in range(jb):
            e_scr[n] = es[n]

            def body(i, s, n=n):
                rows = pl.ds(pl.multiple_of(i * 2 * B, 2 * B), 2 * B)
                e2 = e_scr[n, rows, :]
                mid = pltpu.roll(advance(decay[n], s, e2), B, 0)
                s_scr[n, rows, :] = jnp.where(top, s, mid)
                return pltpu.roll(advance(decay[n], mid, e2), B, 0)

            c_scr[n] = lax.fori_loop(0, Rn // 2, body, c_scr[n], unroll=True)
            st_ref[n] = c_scr[n, :B, :]
            starts.append(s_scr[n])
    else:
        for n in range(jb):
            starts.append(h0_ref[n])
            st_ref[n] = advance(decay[n], starts[n], es[n])

    for n in range(jb):
        g = jax.nn.gelu(local[n] + _dot(starts[n].astype(BF16), v_ref[n]))
        for k in range(L):
            io_scr[n, k] = g[:, k * LANE:(k + 1) * LANE]
        for b in range(B):
            for k in range(L):
                g_ref[n, b, pl.ds(k, Rn, stride=L), :] = io_scr[n, k, pl.ds(b, Rn, stride=B), :]


def _ssm_mix(u_view, ops, d_tiles, h0, *, L, Rn, seq, jb):
    t_op, w_op, v_op, a_l = ops
    nj = N_LANE_BLOCKS
    _, B, trows, _ = u_view.shape
    lk = L * LANE
    nseq = h0.shape[1]
    tiles = trows // (Rn * L)
    M = B * Rn
    sb2 = 2 * STATE_BLOCK
    first = (u_view.shape[0] - nj) // jb
    u_spec = pl.BlockSpec((jb, B, Rn * L, LANE), lambda j, i: (j + first, 0, i, 0))
    g_spec = pl.BlockSpec((jb, B, Rn * L, LANE), lambda j, i: (j, 0, i, 0))
    per_block = lambda *shape: pl.BlockSpec((jb,) + shape, lambda j, i: (j, 0, 0))
    scratch = [pltpu.VMEM((jb, L, M, LANE), F32)]
    if seq:
        scratch += [pltpu.VMEM((jb, M, sb2), F32), pltpu.VMEM((jb, M, sb2), F32),
                    pltpu.VMEM((jb, 2 * B, sb2), F32)]
    return pl.pallas_call(
        functools.partial(_ssm_kernel, B=B, Rn=Rn, L=L, seq=seq, jb=jb),
        grid=(nj // jb, tiles),
        in_specs=[u_spec, per_block(lk, lk), per_block(lk, sb2), per_block(sb2, lk), per_block(1, sb2),
                  per_block(1, lk), per_block(nseq, sb2)],
        out_specs=[g_spec, per_block(nseq, sb2)],
        out_shape=[jax.ShapeDtypeStruct((nj,) + u_view.shape[1:], F32),
                   jax.ShapeDtypeStruct((nj, nseq, sb2), F32)],
        scratch_shapes=scratch,
        compiler_params=_params(("arbitrary", "arbitrary"), 56),
        name="ssm_mix_seq" if seq else "ssm_mix_rows",
    )(u_view, t_op, w_op, v_op, a_l, d_tiles, h0)


def _mix_out_kernel(g_ref, po_ref, x_ref, wglu_ref, bglu_ref, wout_ref, gc_ref, wq_ref, *rest, attend):
    if attend:
        k_ref, v_ref, x1_ref, q_ref = rest
    else:
        x1_ref, q_ref = rest
    g = _lane_blocks(g_ref)
    gate = jax.nn.sigmoid(_dot(g.astype(BF16), wglu_ref[...]) + bglu_ref[...])
    ssm_out = (g * gate).astype(BF16)
    mix = (_dot(_lane_blocks(po_ref).astype(BF16), wout_ref[:POOL_WIDTH, :])
           + _dot(ssm_out, wout_ref[POOL_WIDTH:, :]))
    x1 = x_ref[...] + mix
    x1_ref[...] = x1
    q = _dot(_rms(x1, gc_ref[...]).astype(BF16), wq_ref[...]).astype(BF16)
    if attend:
        for h in range(N_XHEADS):
            sl = slice(h * XHEAD_DIM, (h + 1) * XHEAD_DIM)
            s = lax.dot_general(q[:, sl], k_ref[:, sl], _NT, preferred_element_type=F32)
            p = _softmax_rows(s * (XHEAD_DIM ** -0.5))
            q_ref[:, sl] = _dot(p.astype(BF16), v_ref[:, sl]).astype(BF16)
    else:
        q_ref[...] = q


def _mix_out(g_rows, pool_out, x, w_glu, b_glu, w_out, g_cross, w_q, tm, mem_kv=None, rows_per_seq=None):
    rows = x.shape[0]
    attend = mem_kv is not None
    extra_specs, extra = [], []
    if attend:
        tiles = rows_per_seq // tm
        kv_spec = pl.BlockSpec((N_MEM, D_MODEL), lambda i: (i // tiles, 0))
        extra_specs, extra = [kv_spec, kv_spec], list(mem_kv)
    return pl.pallas_call(
        functools.partial(_mix_out_kernel, attend=attend),
        grid=(rows // tm,),
        in_specs=[pl.BlockSpec((N_LANE_BLOCKS, tm, LANE), lambda i: (0, i, 0)),
                  pl.BlockSpec((POOL_LANE_BLOCKS, tm, LANE), lambda i: (0, i, 0)),
                  pl.BlockSpec((tm, D_MODEL), lambda i: (i, 0)),
                  _const_spec((SSM_WIDTH, SSM_WIDTH)), _const_spec((1, SSM_WIDTH)),
                  _const_spec((D_MODEL, D_MODEL)), _const_spec((1, D_MODEL)),
                  _const_spec((D_MODEL, D_MODEL))] + extra_specs,
        out_specs=[pl.BlockSpec((tm, D_MODEL), lambda i: (i, 0)),
                   pl.BlockSpec((tm, D_MODEL), lambda i: (i, 0))],
        out_shape=[jax.ShapeDtypeStruct((rows, D_MODEL), F32),
                   jax.ShapeDtypeStruct((rows, D_MODEL), BF16)],
        compiler_params=_params(("arbitrary",), 56),
        name="mix_out_attn" if attend else "mix_out",
    )(g_rows, pool_out, x, w_glu, b_glu, w_out, g_cross, w_q, *extra)


HEAD_LANE_BLOCKS = XHEAD_DIM // LANE
CACHE_ROW_PITCH = N_XHEADS * HEAD_LANE_BLOCKS


def _cache_rows_view(cache):
    S = cache.shape[0]
    c5 = cache.reshape(S, N_MEM, N_XHEADS, HEAD_LANE_BLOCKS, LANE)
    return c5.transpose(0, 1, 3, 2, 4).reshape(S, N_MEM * CACHE_ROW_PITCH, LANE)


def _cache_rows_unview(rows, nseq):
    r5 = rows.reshape(nseq, N_MEM, HEAD_LANE_BLOCKS, N_XHEADS, LANE)
    return r5.transpose(0, 1, 3, 2, 4).reshape(nseq, N_MEM, N_XHEADS, XHEAD_DIM)


def _mem_kv_kernel(m_ref, g_ref, w_ref, rows_ref, kvb_ref):
    tm = m_ref.shape[0]
    kv = _dot(_rms(m_ref[...], g_ref[...]).astype(BF16), w_ref[...].astype(BF16))
    kvb_ref[...] = kv.astype(BF16)
    for c in range(HEAD_LANE_BLOCKS):
        for h in range(N_XHEADS):
            lo = h * XHEAD_DIM + c * LANE
            rows_ref[pl.ds(c * N_XHEADS + h, tm, stride=CACHE_ROW_PITCH), :] = kv[:, lo:lo + LANE]


def _mem_kv(mem, g_mem, w, tm):
    rows = mem.shape[0]
    return pl.pallas_call(
        _mem_kv_kernel,
        grid=(rows // tm,),
        in_specs=[pl.BlockSpec((tm, D_MODEL), lambda i: (i, 0)),
                  _const_spec((1, D_MODEL)),
                  _const_spec((D_MODEL, D_MODEL))],
        out_specs=[pl.BlockSpec((tm * CACHE_ROW_PITCH, LANE), lambda i: (i, 0)),
                   pl.BlockSpec((tm, D_MODEL), lambda i: (i, 0))],
        out_shape=[jax.ShapeDtypeStruct((rows * CACHE_ROW_PITCH, LANE), F32),
                   jax.ShapeDtypeStruct((rows, D_MODEL), BF16)],
        compiler_params=_params(("arbitrary",), 48),
        name="mem_kv",
    )(mem, g_mem, w)


_NT = (((1,), (1,)), ((), ()))


def _softmax_rows(s):
    e = jnp.exp(s - jnp.max(s, axis=-1, keepdims=True))
    return e / jnp.sum(e, axis=-1, keepdims=True)


def _attn_cache_kernel(q_ref, k_ref, v_ref, o_ref, *, bb):
    nc = HEAD_LANE_BLOCKS
    mc = N_MEM * nc
    lane_c = lax.broadcasted_iota(jnp.int32, (8, mc), 1) & (nc - 1)
    owns = []
    for b in range(bb):
        q8 = q_ref[b]
        for h in range(N_XHEADS):
            xk = k_ref[b, pl.ds(h, mc, stride=N_XHEADS), :].astype(BF16)
            qc = jnp.concatenate([q8[:, h * XHEAD_DIM + c * LANE:h * XHEAD_DIM + (c + 1) * LANE]
                                  for c in range(nc)], axis=0).astype(BF16)
            part = lax.dot_general(qc, xk, _NT, preferred_element_type=F32)
            own = part[8 * (nc - 1):]
            for c in range(nc - 2, -1, -1):
                own = jnp.where(lane_c == c, part[8 * c:8 * (c + 1)], own)
            owns.append(own)
    own_all = jnp.concatenate(owns, axis=0)
    lane_all = lax.broadcasted_iota(jnp.int32, own_all.shape, 1) & (nc - 1)
    pair = own_all + pltpu.roll(own_all, 1, 1)
    full = pair + pltpu.roll(pair, 2, 1)
    s = jnp.where(lane_all == nc - 1, full * (XHEAD_DIM ** -0.5), -1e30)
    e = jnp.exp(s - jnp.max(s, axis=1, keepdims=True))
    p = e / jnp.sum(e, axis=1, keepdims=True)
    p2 = p + pltpu.roll(p, mc - 1, 1)
    p4 = p2 + pltpu.roll(p2, mc - 2, 1)
    for b in range(bb):
        xv = jnp.concatenate([v_ref[b, pl.ds(h, mc, stride=N_XHEADS), :] for h in range(N_XHEADS)],
                             axis=1).astype(BF16)
        pm_rows = []
        for h in range(N_XHEADS):
            r0 = 8 * (b * N_XHEADS + h)
            pm_rows += [jnp.where(lane_c == c, p4[r0:r0 + 8], 0.0) for c in range(nc)]
        o_all = _dot(jnp.concatenate(pm_rows, axis=0).astype(BF16), xv)
        for h in range(N_XHEADS):
            for c in range(nc):
                g = h * nc + c
                o_ref[b, :, h * XHEAD_DIM + c * LANE:h * XHEAD_DIM + (c + 1) * LANE] = (
                    o_all[8 * g:8 * (g + 1), h * LANE:(h + 1) * LANE])


def _attn_cache(q8, k_rows, v_rows, bb):
    nseq = q8.shape[0]
    rows = N_MEM * CACHE_ROW_PITCH
    return pl.pallas_call(
        functools.partial(_attn_cache_kernel, bb=bb),
        grid=(nseq // bb,),
        in_specs=[pl.BlockSpec((bb, 8, D_MODEL), lambda i: (i, 0, 0)),
                  pl.BlockSpec((bb, rows, LANE), lambda i: (i, 0, 0)),
                  pl.BlockSpec((bb, rows, LANE), lambda i: (i, 0, 0))],
        out_specs=pl.BlockSpec((bb, 8, D_MODEL), lambda i: (i, 0, 0)),
        out_shape=jax.ShapeDtypeStruct((nseq, 8, D_MODEL), F32),
        compiler_params=_params(("arbitrary",), 48),
        name="attn_cache",
    )(q8, k_rows, v_rows)


def _oproj_kernel(o_ref, x1_ref, wo_ref, gf_ref, x2_ref, h_ref):
    x2 = x1_ref[...] + _dot(o_ref[...], wo_ref[...])
    x2_ref[...] = x2
    h_ref[...] = _rms(x2, gf_ref[...]).astype(BF16)


def _oproj(o, x1, w_o, g_ffn, tm):
    rows = x1.shape[0]
    row_spec = pl.BlockSpec((tm, D_MODEL), lambda i: (i, 0))
    return pl.pallas_call(
        _oproj_kernel,
        grid=(rows // tm,),
        in_specs=[row_spec, row_spec, _const_spec((D_MODEL, D_MODEL)), _const_spec((1, D_MODEL))],
        out_specs=[row_spec, row_spec],
        out_shape=[jax.ShapeDtypeStruct((rows, D_MODEL), F32),
                   jax.ShapeDtypeStruct((rows, D_MODEL), BF16)],
        compiler_params=_params(("arbitrary",), 40),
        name="oproj",
    )(o, x1, w_o, g_ffn)


def _ffn_kernel(x_ref, h_ref, wg_ref, wu_ref, wd_ref, gl_ref, y_ref, *bf16_out):
    f = pl.program_id(1)
    last = pl.num_programs(1) - 1

    def down():
        wg, wu, wd = (w[...].astype(BF16) for w in (wg_ref, wu_ref, wd_ref))
        for ref, w in zip(bf16_out, (wg, wu, wd)):
            ref[...] = w
        h = h_ref[...]
        z = (jax.nn.silu(_dot(h, wg)) * _dot(h, wu)).astype(BF16)
        return _dot(z, wd)

    @pl.when(f == 0)
    def _():
        y_ref[...] = down()

    @pl.when((f > 0) & (f < last))
    def _():
        y_ref[...] += down()

    @pl.when(f == last)
    def _():
        y_ref[...] = _rms(x_ref[...] + (y_ref[...] + down()), gl_ref[...])


def _ffn(x2, h, w_gate, w_up, w_down, g_final, tm, tf, emit_bf16=False):
    rows = x2.shape[0]
    nf = D_FF // tf
    row_spec = pl.BlockSpec((tm, D_MODEL), lambda i, f: (i, 0))
    out_specs = [pl.BlockSpec((tm, D_MODEL), lambda i, f: (i, 0), pipeline_mode=pl.Buffered(1))]
    out_shape = [jax.ShapeDtypeStruct((rows, D_MODEL), F32)]
    if emit_bf16:
        once = lambda i, f: jnp.where(i == 0, f, nf)
        out_specs += [pl.BlockSpec((D_MODEL, tf), lambda i, f: (0, once(i, f))),
                      pl.BlockSpec((D_MODEL, tf), lambda i, f: (0, once(i, f))),
                      pl.BlockSpec((tf, D_MODEL), lambda i, f: (once(i, f), 0))]
        out_shape += [jax.ShapeDtypeStruct((D_MODEL, D_FF + tf), BF16),
                      jax.ShapeDtypeStruct((D_MODEL, D_FF + tf), BF16),
                      jax.ShapeDtypeStruct((D_FF + tf, D_MODEL), BF16)]
    return pl.pallas_call(
        _ffn_kernel,
        grid=(rows // tm, nf),
        in_specs=[row_spec, row_spec,
                  pl.BlockSpec((D_MODEL, tf), lambda i, f: (0, f)),
                  pl.BlockSpec((D_MODEL, tf), lambda i, f: (0, f)),
                  pl.BlockSpec((tf, D_MODEL), lambda i, f: (f, 0)),
                  _const_spec((1, D_MODEL))],
        out_specs=out_specs,
        out_shape=out_shape,
        compiler_params=_params(("arbitrary", "arbitrary"), 60 if emit_bf16 else 56),
        name="ffn",
    )(x2, h, w_gate, w_up, w_down, g_final)


def _ssm_param_layouts(lam_re, lam_im, log_step, b_re, b_im, c_re, c_im):
    G, P = lam_re.shape
    ls = jnp.broadcast_to(log_step[:, None], (G, P))
    rep = lambda a: jnp.repeat(a, SSM_GROUP, axis=0)
    b_rows = lambda b: jnp.swapaxes(b, 1, 2).reshape(G * SSM_GROUP, P)
    row5 = jnp.stack([rep(lam_re), rep(lam_im), rep(ls), b_rows(b_re), b_rows(b_im)])
    col5 = jnp.stack([rep(lam_re).T, rep(lam_im).T, rep(ls).T,
                      c_re.reshape(G * SSM_GROUP, P).T, c_im.reshape(G * SSM_GROUP, P).T])
    flat = lambda a: a.reshape(N_LANE_BLOCKS, 1, STATE_BLOCK)
    flat3 = jnp.stack([flat(lam_re), flat(lam_im), flat(ls)])
    return row5, col5, flat3


def _states_to_blocks(h_re, h_im):
    S = h_re.shape[0]
    blk = lambda h: h.reshape(S, N_LANE_BLOCKS, STATE_BLOCK).transpose(1, 0, 2)
    return jnp.concatenate([blk(h_re), blk(h_im)], axis=-1)


def _blocks_to_states(st):
    S = st.shape[1]
    unblk = lambda a: a.transpose(1, 0, 2).reshape(1, S, N_SSM_GROUPS, SSM_STATE)
    return unblk(st[:, :, :STATE_BLOCK]), unblk(st[:, :, STATE_BLOCK:])


def kernel(x_prompt, x_sample, mem_prompt, state_pool_buf, state_ssm_re, state_ssm_im, cache_mem_k, cache_mem_v, g_mix, w_in, w_pool, pool_scale, ssm_lam_re, ssm_lam_im, ssm_log_step, ssm_b_re, ssm_b_im, ssm_c_re, ssm_c_im, ssm_d, w_glu, b_glu, w_out, g_cross, g_mem, w_q, w_k, w_v, w_o, g_ffn, w_gate, w_up, w_down, g_final):
    assert g_mix.shape[0] == 1, "single-layer step"
    B, T, _ = x_prompt.shape
    S, Ts, _ = x_sample.shape
    bf = lambda w: w.astype(BF16)
    vec = lambda v: v.reshape(1, -1)

    w_in_b, w_pool_b, w_glu_b, w_out_b = bf(w_in[0]), bf(w_pool[0]), bf(w_glu[0]), bf(w_out[0])
    w_q_b, w_o_b = bf(w_q[0]), bf(w_o[0])

    row5, col5, flat3 = _ssm_param_layouts(ssm_lam_re[0], ssm_lam_im[0], ssm_log_step[0],
                                           ssm_b_re[0], ssm_b_im[0], ssm_c_re[0], ssm_c_im[0])
    d_blocks = ssm_d[0].reshape(N_LANE_BLOCKS, 1, LANE)

    gains = dict(g_mix=vec(g_mix[0]), scale=vec(pool_scale[0]), b_glu=vec(b_glu[0]), g_cross=vec(g_cross[0]),
                 g_ffn=vec(g_ffn[0]), g_final=vec(g_final))

    chunk_ops = _ssm_prep(row5, col5, flat3, sorted({PROMPT_CHUNK, Ts}, reverse=True))

    def ssm(u_view, L, h0_blocks, **tiles):
        g_act, st = _ssm_mix(u_view, chunk_ops[L], jnp.tile(d_blocks, (1, 1, L)), h0_blocks, L=L, **tiles)
        rows = u_view.shape[1] * u_view.shape[2]
        return g_act.reshape(N_LANE_BLOCKS, rows, LANE), _blocks_to_states(st)

    def finish(o, x1, ffn_weights, ffn_tiles, **kw):
        x2, hf = _oproj(o, x1, w_o_b, gains["g_ffn"], ROW_TILE)
        return _ffn(x2, hf, *ffn_weights, gains["g_final"], *ffn_tiles, **kw)

    mem = mem_prompt.reshape(B * N_MEM, D_MODEL)
    k_rows, k_b = _mem_kv(mem, vec(g_mem[0]), w_k[0], ROW_TILE)
    v_rows, v_b = _mem_kv(mem, vec(g_mem[0]), w_v[0], ROW_TILE)
    xp = x_prompt.reshape(B * T, D_MODEL)
    us_p, po_p, tail_p = _in_proj_pool(xp, gains["g_mix"], w_in_b, w_pool_b, gains["scale"],
                                       tc=ROW_TILE, rows_per_seq=T)
    h0_p = jnp.zeros((N_LANE_BLOCKS, B, 2 * STATE_BLOCK), F32)
    g_p, (re_p, im_p) = ssm(us_p.reshape(N_LANE_BLOCKS, B, T, LANE), PROMPT_CHUNK, h0_p,
                            Rn=PROMPT_CHUNK_ROWS, seq=True, jb=PROMPT_LANE_BLOCKS_PER_STEP)
    x1_p, o_p = _mix_out(g_p, po_p, xp, w_glu_b, gains["b_glu"], w_out_b, gains["g_cross"], w_q_b, ROW_TILE,
                         mem_kv=(k_b, v_b), rows_per_seq=T)
    yp, *ffn_bf16 = finish(o_p, x1_p, (w_gate[0], w_up[0], w_down[0]), PROMPT_FFN_TILES, emit_bf16=True)
    pb_p = tail_p[:, POOL_HIST - POOL_BUF:]

    assert PAST_LEN >= POOL_BUF
    xs = x_sample.reshape(S * Ts, D_MODEL)
    u_s = _in_proj(xs, gains["g_mix"], w_in_b, ROW_TILE)
    po_s, buf_tm = _pool_mix_buf(u_s, state_pool_buf[0].transpose(1, 0, 2), w_pool_b, gains["scale"], L=Ts)
    pb_s = buf_tm.transpose(1, 0, 2)
    h0_s = _states_to_blocks(state_ssm_re[0], state_ssm_im[0])
    g_s, (re_s, im_s) = ssm(u_s.reshape(MIX_LANE_BLOCKS, 1, S * Ts, LANE), Ts, h0_s, Rn=S, seq=False, jb=1)
    x1_s, q_s = _mix_out(g_s, po_s, xs, w_glu_b, gains["b_glu"], w_out_b, gains["g_cross"], w_q_b, ROW_TILE)
    q8 = jnp.pad(q_s.reshape(S, Ts, D_MODEL).astype(F32), ((0, 0), (0, 8 - Ts), (0, 0)))
    o8 = _attn_cache(q8, _cache_rows_view(cache_mem_k[0]), _cache_rows_view(cache_mem_v[0]), CACHE_SEQS_PER_STEP)
    (ys,) = finish(o8[:, :Ts].reshape(S * Ts, D_MODEL).astype(BF16), x1_s, ffn_bf16, SAMPLE_FFN_TILES)

    mk = _cache_rows_unview(k_rows, B)[None]
    mv = _cache_rows_unview(v_rows, B)[None]
    return (yp.reshape(B, T, D_MODEL), ys.reshape(S, Ts, D_MODEL), pb_p[None], re_p, im_p, mk, mv,
            pb_s[None], re_s, im_s)
```

```python
import functools

import jax
import jax.numpy as jnp
from jax import lax
from jax.experimental import pallas as pl
from jax.experimental.pallas import tpu as pltpu

F32 = jnp.float32
BF16 = jnp.bfloat16

D_MODEL = 2048
POOL_WIDTH = 1024
SSM_WIDTH = 1024
POOL_WINDOWS = (2, 4, 8, 16)
POOL_GROUP = POOL_WIDTH // len(POOL_WINDOWS)
POOL_BUF = max(POOL_WINDOWS) - 1
SSM_GROUP = 16
N_SSM_GROUPS = SSM_WIDTH // SSM_GROUP
SSM_STATE = 64
N_MEM = 256
N_XHEADS = 4
XHEAD_DIM = D_MODEL // N_XHEADS
D_FF = 5632
EPS = 1e-6
PAST_LEN = 16384

LANE = 128
N_LANE_BLOCKS = SSM_WIDTH // LANE
GROUPS_PER_BLOCK = LANE // SSM_GROUP
STATE_BLOCK = GROUPS_PER_BLOCK * SSM_STATE
MIB = 1024 * 1024
HI = lax.Precision.HIGHEST

ROW_TILE = 512
ROW_GROUPS = 4
PROMPT_CHUNK = 8
PROMPT_CHUNK_ROWS = 128
PROMPT_FFN_TILES = (1024, 256)
SAMPLE_FFN_TILES = (512, 512)
CACHE_SEQS_PER_STEP = 4


def _params(semantics, vmem_mib):
    return pltpu.CompilerParams(dimension_semantics=semantics, vmem_limit_bytes=vmem_mib * MIB)


def _const_spec(shape):
    return pl.BlockSpec(shape, lambda *_: (0,) * len(shape), pipeline_mode=pl.Buffered(1))


def _rms(x, g):
    r = lax.rsqrt(jnp.mean(x * x, axis=-1, keepdims=True) + EPS)
    return x * r * g


def _dot(a, b):
    return jnp.dot(a, b, preferred_element_type=F32)


MIX_LANE_BLOCKS = D_MODEL // LANE


def _in_proj_kernel(x_ref, g_ref, w_ref, u_ref):
    h = _rms(x_ref[...], g_ref[...]).astype(BF16)
    u = _dot(h, w_ref[...])
    for j in range(MIX_LANE_BLOCKS):
        u_ref[j] = u[:, j * LANE:(j + 1) * LANE]


def _in_proj(x, g, w, tm):
    rows = x.shape[0]
    return pl.pallas_call(
        _in_proj_kernel,
        grid=(rows // tm,),
        in_specs=[pl.BlockSpec((tm, D_MODEL), lambda i: (i, 0)),
                  _const_spec((1, D_MODEL)),
                  _const_spec((D_MODEL, D_MODEL))],
        out_specs=pl.BlockSpec((MIX_LANE_BLOCKS, tm, LANE), lambda i: (0, i, 0)),
        out_shape=jax.ShapeDtypeStruct((MIX_LANE_BLOCKS, rows, LANE), F32),
        compiler_params=_params(("arbitrary",), 40),
        name="in_proj",
    )(x, g, w)


def _lane_blocks(ref, rows=slice(None)):
    return jnp.concatenate([ref[j, rows, :] for j in range(ref.shape[0])], axis=1)


POOL_HIST = POOL_BUF + 1


def _in_proj_pool_kernel(x_ref, g_ref, w_ref, wp_ref, sc_ref, us_ref, o_ref, tail_ref, ext_ref, *, tc,
                         tiles_per_seq):
    it = lax.rem(pl.program_id(0), tiles_per_seq)

    @pl.when(it == 0)
    def _():
        ext_ref[:POOL_HIST, :] = jnp.zeros((POOL_HIST, POOL_WIDTH), F32)

    nq = ROW_GROUPS
    rq = tc // nq
    for c in range(nq):
        rows = slice(c * rq, (c + 1) * rq)
        u = _dot(_rms(x_ref[rows, :], g_ref[...]).astype(BF16), w_ref[...])
        for j in range(N_LANE_BLOCKS):
            us_ref[j, rows, :] = u[:, POOL_WIDTH + j * LANE:POOL_WIDTH + (j + 1) * LANE]
        ext_ref[POOL_HIST + c * rq:POOL_HIST + (c + 1) * rq, :] = u[:, :POOL_WIDTH]
    for c in range(nq):
        rows = slice(c * rq, (c + 1) * rq)
        pos = it * tc + c * rq + lax.broadcasted_iota(jnp.int32, (rq, 1), 0)
        for g, w in enumerate(POOL_WINDOWS):
            sl = slice(g * POOL_GROUP, (g + 1) * POOL_GROUP)
            z = ext_ref[c * rq:(c + 1) * rq + POOL_HIST, sl]
            s, span = z, 1
            while span < w:
                s = s + pltpu.roll(s, span, 0)
                span *= 2
            cnt = jnp.minimum(pos + 1, w).astype(F32)
            pooled = (s[POOL_HIST:] / cnt - z[POOL_HIST:]).astype(BF16)
            out = _dot(pooled, wp_ref[g]) * sc_ref[:, sl]
            for jj in range(POOL_GROUP // LANE):
                o_ref[g * (POOL_GROUP // LANE) + jj, rows, :] = out[:, jj * LANE:(jj + 1) * LANE]
    last = ext_ref[tc:tc + POOL_HIST, :]
    tail_ref[0] = last
    ext_ref[:POOL_HIST, :] = last


POOL_LANE_BLOCKS = POOL_WIDTH // LANE


def _in_proj_pool(x, g, w, w_pool, scale, *, tc, rows_per_seq):
    rows = x.shape[0]
    tiles_per_seq = rows_per_seq // tc
    slab_spec = pl.BlockSpec((N_LANE_BLOCKS, tc, LANE), lambda i: (0, i, 0))
    slabs = jax.ShapeDtypeStruct((N_LANE_BLOCKS, rows, LANE), F32)
    return pl.pallas_call(
        functools.partial(_in_proj_pool_kernel, tc=tc, tiles_per_seq=tiles_per_seq),
        grid=(rows // tc,),
        in_specs=[pl.BlockSpec((tc, D_MODEL), lambda i: (i, 0)),
                  _const_spec((1, D_MODEL)),
                  _const_spec((D_MODEL, D_MODEL)),
                  _const_spec((len(POOL_WINDOWS), POOL_GROUP, POOL_GROUP)),
                  _const_spec((1, POOL_WIDTH))],
        out_specs=[slab_spec, slab_spec,
                   pl.BlockSpec((1, POOL_HIST, POOL_WIDTH), lambda i: (i // tiles_per_seq, 0, 0))],
        out_shape=[slabs, slabs,
                   jax.ShapeDtypeStruct((rows // rows_per_seq, POOL_HIST, POOL_WIDTH), F32)],
        scratch_shapes=[pltpu.VMEM((tc + POOL_HIST, POOL_WIDTH), F32)],
        compiler_params=_params(("arbitrary",), 48),
        name="in_proj_pool",
    )(x, g, w, w_pool, scale)


def _pool_buf_kernel(u_ref, buf_ref, wp_ref, sc_ref, o_ref, nb_ref, *, L):
    nseq = buf_ref.shape[1]
    per_group = POOL_GROUP // LANE

    def token_rows(t):
        return pl.ds(t, nseq, stride=L)

    def slab(idx, g):
        if idx < POOL_BUF:
            return buf_ref[idx, :, g * POOL_GROUP:(g + 1) * POOL_GROUP]
        return jnp.concatenate([u_ref[g * per_group + jj, token_rows(idx - POOL_BUF), :]
                                for jj in range(per_group)], axis=1)

    for g, w in enumerate(POOL_WINDOWS):
        sl = slice(g * POOL_GROUP, (g + 1) * POOL_GROUP)
        pooled = []
        for t in range(L):
            acc = slab(POOL_BUF + t, g)
            for back in range(1, w):
                acc = acc + slab(POOL_BUF + t - back, g)
            pooled.append((acc / float(w) - slab(POOL_BUF + t, g)).astype(BF16))
        out = _dot(jnp.concatenate(pooled, axis=0), wp_ref[g]) * sc_ref[:, sl]
        for t in range(L):
            for jj in range(per_group):
                o_ref[g * per_group + jj, token_rows(t), :] = out[t * nseq:(t + 1) * nseq,
                                                                  jj * LANE:(jj + 1) * LANE]
    keep = POOL_BUF - L
    for k in range(keep):
        nb_ref[k] = buf_ref[k + L]
    for t in range(L):
        nb_ref[keep + t] = _lane_blocks(u_ref, token_rows(t))


def _pool_mix_buf(u_blocks, buf_tm, w_pool, scale, *, L):
    rows = u_blocks.shape[1]
    slab_spec = pl.BlockSpec((POOL_LANE_BLOCKS, rows, LANE), lambda i: (0, 0, 0))
    buf_spec = pl.BlockSpec(buf_tm.shape, lambda i: (0, 0, 0))
    return pl.pallas_call(
        functools.partial(_pool_buf_kernel, L=L),
        grid=(1,),
        in_specs=[slab_spec, buf_spec, _const_spec((len(POOL_WINDOWS), POOL_GROUP, POOL_GROUP)),
                  _const_spec((1, POOL_WIDTH))],
        out_specs=[slab_spec, buf_spec],
        out_shape=[jax.ShapeDtypeStruct((POOL_LANE_BLOCKS, rows, LANE), F32),
                   jax.ShapeDtypeStruct(buf_tm.shape, F32)],
        compiler_params=_params(("arbitrary",), 48),
        name="pool_mix_buf",
    )(u_blocks, buf_tm, w_pool, scale)


def _cmul(ar, ai, br, bi):
    return ar * br - ai * bi, ar * bi + ai * br


def _cexp(lam_re, lam_im, log_step):
    delta = jnp.exp(log_step)
    mag = jnp.exp(lam_re * delta)
    ang = lam_im * delta
    return mag * jnp.cos(ang), mag * jnp.sin(ang)


def _ssm_prep_kernel(row_ref, col_ref, flat_ref, *out_refs, chunks):
    L = max(chunks)
    lr, li, ls, br, bi = (row_ref[i] for i in range(5))
    ar, ai = _cexp(lr, li, ls)
    den = lr * lr + li * li
    xr = ar - 1.0
    fr = (xr * lr + ai * li) / den
    fi = (ai * lr - xr * li) / den
    zs = [_cmul(fr, fi, br, bi)]
    for _ in range(1, L):
        zs.append(_cmul(ar, ai, *zs[-1]))

    lrc, lic, lsc, cr, ci = (col_ref[i] for i in range(5))
    acr, aci = _cexp(lrc, lic, lsc)
    xs = []
    cur = (cr, ci)
    for _ in range(L):
        cur = _cmul(acr, aci, *cur)
        xs.append(cur)

    def iota(shape, dim):
        return lax.broadcasted_iota(jnp.int32, shape, dim)

    same_tt = (iota((LANE, LANE), 0) >> 4) == (iota((LANE, LANE), 1) >> 4)
    c_stack = jnp.concatenate([cr, -ci], axis=0)
    zero_tile = jnp.zeros((LANE, LANE), BF16)
    lag = []
    for d in range(L):
        z_stack = jnp.concatenate([zs[d][0], zs[d][1]], axis=1)
        tile = jnp.dot(z_stack, c_stack, precision=HI, preferred_element_type=F32)
        lag.append(jnp.where(same_tt, tile, 0.0).astype(BF16))

    same_w = (iota((LANE, STATE_BLOCK), 0) >> 4) == (iota((LANE, STATE_BLOCK), 1) >> 6)
    w_tiles = []
    for zr, zi in zs:
        wr = jnp.where(same_w, jnp.concatenate([zr] * GROUPS_PER_BLOCK, axis=1), 0.0)
        wi = jnp.where(same_w, jnp.concatenate([zi] * GROUPS_PER_BLOCK, axis=1), 0.0)
        w_tiles.append(jnp.concatenate([wr, wi], axis=1).astype(BF16))

    same_v = (iota((STATE_BLOCK, LANE), 0) >> 6) == (iota((STATE_BLOCK, LANE), 1) >> 4)
    v_tiles = []
    for xr_t, xi_t in xs:
        vr = jnp.where(same_v, jnp.concatenate([xr_t] * GROUPS_PER_BLOCK, axis=0), 0.0)
        vi = jnp.where(same_v, jnp.concatenate([xi_t] * GROUPS_PER_BLOCK, axis=0), 0.0)
        v_tiles.append(jnp.concatenate([vr, -vi], axis=0).astype(BF16))

    flr, fli, fls = (flat_ref[i, 0] for i in range(3))
    far, fai = _cexp(flr, fli, fls)
    powers = [(far, fai)]
    for _ in range(L - 1):
        powers.append(_cmul(far, fai, *powers[-1]))

    for n, Lc in enumerate(chunks):
        t_ref, w_ref, v_ref, al_ref = out_refs[4 * n:4 * n + 4]
        for k in range(Lc):
            for t in range(Lc):
                t_ref[0, k * LANE:(k + 1) * LANE, t * LANE:(t + 1) * LANE] = lag[t - k] if t >= k else zero_tile
            w_ref[0, k * LANE:(k + 1) * LANE, :] = w_tiles[Lc - 1 - k]
        for t in range(Lc):
            v_ref[0, :, t * LANE:(t + 1) * LANE] = v_tiles[t]
        al_ref[0] = jnp.concatenate(powers[Lc - 1], axis=1)


def _ssm_prep(row5, col5, flat3, chunks):
    out_specs, out_shape = [], []
    for L in chunks:
        lk = L * LANE
        for shape, dtype in (((lk, lk), BF16), ((lk, 2 * STATE_BLOCK), BF16), ((2 * STATE_BLOCK, lk), BF16),
                             ((1, 2 * STATE_BLOCK), F32)):
            out_specs.append(pl.BlockSpec((1,) + shape, lambda j: (j, 0, 0)))
            out_shape.append(jax.ShapeDtypeStruct((N_LANE_BLOCKS,) + shape, dtype))
    outs = pl.pallas_call(
        functools.partial(_ssm_prep_kernel, chunks=tuple(chunks)),
        grid=(N_LANE_BLOCKS,),
        in_specs=[pl.BlockSpec((5, LANE, SSM_STATE), lambda j: (0, j, 0)),
                  pl.BlockSpec((5, SSM_STATE, LANE), lambda j: (0, 0, j)),
                  pl.BlockSpec((3, 1, 1, STATE_BLOCK), lambda j: (0, j, 0, 0))],
        out_specs=out_specs,
        out_shape=out_shape,
        compiler_params=_params(("arbitrary",), 40),
        name="ssm_prep",
    )(row5, col5, flat3)
    return {L: tuple(outs[4 * n:4 * n + 4]) for n, L in enumerate(chunks)}


def _ssm_kernel(u_ref, t_ref, w_ref, v_ref, al_ref, d_ref, h0_ref, g_ref, st_ref, *scratch, B, Rn, L, seq):
    M = B * Rn
    sb = STATE_BLOCK
    io_scr = scratch[0]
    for b in range(B):
        for k in range(L):
            io_scr[k, pl.ds(b, Rn, stride=B), :] = u_ref[0, b, pl.ds(k, Rn, stride=L), :]
    u = jnp.concatenate([io_scr[k] for k in range(L)], axis=1)
    ub = u.astype(BF16)
    e = _dot(ub, w_ref[0])
    ar = al_ref[0][:, :sb]
    ai = al_ref[0][:, sb:]

    def advance(s, eb):
        sr, si = s[:, :sb], s[:, sb:]
        return jnp.concatenate([ar * sr - ai * si + eb[:, :sb], ar * si + ai * sr + eb[:, sb:]], axis=1)

    if seq:
        e_scr, s_scr, c_scr = scratch[1:]

        assert 2 * B == 8 and Rn % 2 == 0, "two chunk rows of B sequences fill one sublane tile"

        @pl.when(pl.program_id(1) == 0)
        def _():
            c_scr[:B, :] = h0_ref[0]
            c_scr[B:, :] = jnp.zeros((B, 2 * sb), F32)

        e_scr[...] = e
        top = lax.broadcasted_iota(jnp.int32, (2 * B, 1), 0) < B

        def body(i, s):
            rows = pl.ds(pl.multiple_of(i * 2 * B, 2 * B), 2 * B)
            e2 = e_scr[rows, :]
            mid = pltpu.roll(advance(s, e2), B, 0)
            s_scr[rows, :] = jnp.where(top, s, mid)
            return pltpu.roll(advance(mid, e2), B, 0)

        c_scr[...] = lax.fori_loop(0, Rn // 2, body, c_scr[...], unroll=True)
        st_ref[0] = c_scr[:B, :]
        s_start = s_scr[...]
    else:
        s_start = h0_ref[0]
        st_ref[0] = advance(s_start, e)
    y = _dot(ub, t_ref[0]) + _dot(s_start.astype(BF16), v_ref[0]) + d_ref[0] * u
    g = jax.nn.gelu(y)
    for k in range(L):
        io_scr[k] = g[:, k * LANE:(k + 1) * LANE]
    for b in range(B):
        for k in range(L):
            g_ref[0, b, pl.ds(k, Rn, stride=L), :] = io_scr[k, pl.ds(b, Rn, stride=B), :]


def _ssm_mix(u_view, ops, d_tiles, h0, *, L, Rn, seq):
    t_op, w_op, v_op, a_l = ops
    nj = N_LANE_BLOCKS
    _, B, trows, _ = u_view.shape
    lk = L * LANE
    nseq = h0.shape[1]
    tiles = trows // (Rn * L)
    M = B * Rn
    sb2 = 2 * STATE_BLOCK
    first = u_view.shape[0] - nj
    u_spec = pl.BlockSpec((1, B, Rn * L, LANE), lambda j, i: (j + first, 0, i, 0))
    g_spec = pl.BlockSpec((1, B, Rn * L, LANE), lambda j, i: (j, 0, i, 0))
    scratch = [pltpu.VMEM((L, M, LANE), F32)]
    if seq:
        scratch += [pltpu.VMEM((M, sb2), F32), pltpu.VMEM((M, sb2), F32), pltpu.VMEM((2 * B, sb2), F32)]
    return pl.pallas_call(
        functools.partial(_ssm_kernel, B=B, Rn=Rn, L=L, seq=seq),
        grid=(nj, tiles),
        in_specs=[u_spec,
                  pl.BlockSpec((1, lk, lk), lambda j, i: (j, 0, 0)),
                  pl.BlockSpec((1, lk, sb2), lambda j, i: (j, 0, 0)),
                  pl.BlockSpec((1, sb2, lk), lambda j, i: (j, 0, 0)),
                  pl.BlockSpec((1, 1, sb2), lambda j, i: (j, 0, 0)),
                  pl.BlockSpec((1, 1, lk), lambda j, i: (j, 0, 0)),
                  pl.BlockSpec((1, nseq, sb2), lambda j, i: (j, 0, 0))],
        out_specs=[g_spec, pl.BlockSpec((1, nseq, sb2), lambda j, i: (j, 0, 0))],
        out_shape=[jax.ShapeDtypeStruct((nj,) + u_view.shape[1:], F32),
                   jax.ShapeDtypeStruct((nj, nseq, sb2), F32)],
        scratch_shapes=scratch,
        compiler_params=_params(("arbitrary", "arbitrary"), 48),
        name="ssm_mix_seq" if seq else "ssm_mix_rows",
    )(u_view, t_op, w_op, v_op, a_l, d_tiles, h0)


def _mix_out_kernel(g_ref, po_ref, x_ref, wglu_ref, bglu_ref, wout_ref, gc_ref, wq_ref, *rest, attend):
    if attend:
        k_ref, v_ref, x1_ref, q_ref = rest
    else:
        x1_ref, q_ref = rest
    g = _lane_blocks(g_ref)
    gate = jax.nn.sigmoid(_dot(g.astype(BF16), wglu_ref[...]) + bglu_ref[...])
    ssm_out = (g * gate).astype(BF16)
    mix = (_dot(_lane_blocks(po_ref).astype(BF16), wout_ref[:POOL_WIDTH, :])
           + _dot(ssm_out, wout_ref[POOL_WIDTH:, :]))
    x1 = x_ref[...] + mix
    x1_ref[...] = x1
    q = _dot(_rms(x1, gc_ref[...]).astype(BF16), wq_ref[...]).astype(BF16)
    if attend:
        for h in range(N_XHEADS):
            sl = slice(h * XHEAD_DIM, (h + 1) * XHEAD_DIM)
            s = lax.dot_general(q[:, sl], k_ref[:, sl], _NT, preferred_element_type=F32)
            p = _softmax_rows(s * (XHEAD_DIM ** -0.5))
            q_ref[:, sl] = _dot(p.astype(BF16), v_ref[:, sl]).astype(BF16)
    else:
        q_ref[...] = q


def _mix_out(g_rows, pool_out, x, w_glu, b_glu, w_out, g_cross, w_q, tm, mem_kv=None, rows_per_seq=None):
    rows = x.shape[0]
    attend = mem_kv is not None
    extra_specs, extra = [], []
    if attend:
        tiles = rows_per_seq // tm
        kv_spec = pl.BlockSpec((N_MEM, D_MODEL), lambda i: (i // tiles, 0))
        extra_specs, extra = [kv_spec, kv_spec], list(mem_kv)
    return pl.pallas_call(
        functools.partial(_mix_out_kernel, attend=attend),
        grid=(rows // tm,),
        in_specs=[pl.BlockSpec((N_LANE_BLOCKS, tm, LANE), lambda i: (0, i, 0)),
                  pl.BlockSpec((POOL_LANE_BLOCKS, tm, LANE), lambda i: (0, i, 0)),
                  pl.BlockSpec((tm, D_MODEL), lambda i: (i, 0)),
                  _const_spec((SSM_WIDTH, SSM_WIDTH)), _const_spec((1, SSM_WIDTH)),
                  _const_spec((D_MODEL, D_MODEL)), _const_spec((1, D_MODEL)),
                  _const_spec((D_MODEL, D_MODEL))] + extra_specs,
        out_specs=[pl.BlockSpec((tm, D_MODEL), lambda i: (i, 0)),
                   pl.BlockSpec((tm, D_MODEL), lambda i: (i, 0))],
        out_shape=[jax.ShapeDtypeStruct((rows, D_MODEL), F32),
                   jax.ShapeDtypeStruct((rows, D_MODEL), BF16)],
        compiler_params=_params(("arbitrary",), 56),
        name="mix_out_attn" if attend else "mix_out",
    )(g_rows, pool_out, x, w_glu, b_glu, w_out, g_cross, w_q, *extra)


HEAD_LANE_BLOCKS = XHEAD_DIM // LANE
CACHE_ROW_PITCH = N_XHEADS * HEAD_LANE_BLOCKS


def _cache_rows_view(cache):
    S = cache.shape[0]
    c5 = cache.reshape(S, N_MEM, N_XHEADS, HEAD_LANE_BLOCKS, LANE)
    return c5.transpose(0, 1, 3, 2, 4).reshape(S, N_MEM * CACHE_ROW_PITCH, LANE)


def _cache_rows_unview(rows, nseq):
    r5 = rows.reshape(nseq, N_MEM, HEAD_LANE_BLOCKS, N_XHEADS, LANE)
    return r5.transpose(0, 1, 3, 2, 4).reshape(nseq, N_MEM, N_XHEADS, XHEAD_DIM)


def _mem_kv_kernel(m_ref, g_ref, w_ref, rows_ref, kvb_ref):
    tm = m_ref.shape[0]
    kv = _dot(_rms(m_ref[...], g_ref[...]).astype(BF16), w_ref[...].astype(BF16))
    kvb_ref[...] = kv.astype(BF16)
    for c in range(HEAD_LANE_BLOCKS):
        for h in range(N_XHEADS):
            lo = h * XHEAD_DIM + c * LANE
            rows_ref[pl.ds(c * N_XHEADS + h, tm, stride=CACHE_ROW_PITCH), :] = kv[:, lo:lo + LANE]


def _mem_kv(mem, g_mem, w, tm):
    rows = mem.shape[0]
    return pl.pallas_call(
        _mem_kv_kernel,
        grid=(rows // tm,),
        in_specs=[pl.BlockSpec((tm, D_MODEL), lambda i: (i, 0)),
                  _const_spec((1, D_MODEL)),
                  _const_spec((D_MODEL, D_MODEL))],
        out_specs=[pl.BlockSpec((tm * CACHE_ROW_PITCH, LANE), lambda i: (i, 0)),
                   pl.BlockSpec((tm, D_MODEL), lambda i: (i, 0))],
        out_shape=[jax.ShapeDtypeStruct((rows * CACHE_ROW_PITCH, LANE), F32),
                   jax.ShapeDtypeStruct((rows, D_MODEL), BF16)],
        compiler_params=_params(("arbitrary",), 48),
        name="mem_kv",
    )(mem, g_mem, w)


_NT = (((1,), (1,)), ((), ()))


def _softmax_rows(s):
    e = jnp.exp(s - jnp.max(s, axis=-1, keepdims=True))
    return e / jnp.sum(e, axis=-1, keepdims=True)


def _attn_cache_kernel(q_ref, k_ref, v_ref, o_ref, *, bb):
    nc = HEAD_LANE_BLOCKS
    mc = N_MEM * nc
    lane_c = lax.broadcasted_iota(jnp.int32, (8, mc), 1) & (nc - 1)
    owns = []
    for b in range(bb):
        q8 = q_ref[b]
        for h in range(N_XHEADS):
            xk = k_ref[b, pl.ds(h, mc, stride=N_XHEADS), :].astype(BF16)
            qc = jnp.concatenate([q8[:, h * XHEAD_DIM + c * LANE:h * XHEAD_DIM + (c + 1) * LANE]
                                  for c in range(nc)], axis=0).astype(BF16)
            part = lax.dot_general(qc, xk, _NT, preferred_element_type=F32)
            own = part[8 * (nc - 1):]
            for c in range(nc - 2, -1, -1):
                own = jnp.where(lane_c == c, part[8 * c:8 * (c + 1)], own)
            owns.append(own)
    own_all = jnp.concatenate(owns, axis=0)
    lane_all = lax.broadcasted_iota(jnp.int32, own_all.shape, 1) & (nc - 1)
    pair = own_all + pltpu.roll(own_all, 1, 1)
    full = pair + pltpu.roll(pair, 2, 1)
    s = jnp.where(lane_all == nc - 1, full * (XHEAD_DIM ** -0.5), -1e30)
    e = jnp.exp(s - jnp.max(s, axis=1, keepdims=True))
    p = e / jnp.sum(e, axis=1, keepdims=True)
    p2 = p + pltpu.roll(p, mc - 1, 1)
    p4 = p2 + pltpu.roll(p2, mc - 2, 1)
    for b in range(bb):
        xv = jnp.concatenate([v_ref[b, pl.ds(h, mc, stride=N_XHEADS), :] for h in range(N_XHEADS)],
                             axis=1).astype(BF16)
        pm_rows = []
        for h in range(N_XHEADS):
            r0 = 8 * (b * N_XHEADS + h)
            pm_rows += [jnp.where(lane_c == c, p4[r0:r0 + 8], 0.0) for c in range(nc)]
        o_all = _dot(jnp.concatenate(pm_rows, axis=0).astype(BF16), xv)
        for h in range(N_XHEADS):
            for c in range(nc):
                g = h * nc + c
                o_ref[b, :, h * XHEAD_DIM + c * LANE:h * XHEAD_DIM + (c + 1) * LANE] = (
                    o_all[8 * g:8 * (g + 1), h * LANE:(h + 1) * LANE])


def _attn_cache(q8, k_rows, v_rows, bb):
    nseq = q8.shape[0]
    rows = N_MEM * CACHE_ROW_PITCH
    return pl.pallas_call(
        functools.partial(_attn_cache_kernel, bb=bb),
        grid=(nseq // bb,),
        in_specs=[pl.BlockSpec((bb, 8, D_MODEL), lambda i: (i, 0, 0)),
                  pl.BlockSpec((bb, rows, LANE), lambda i: (i, 0, 0)),
                  pl.BlockSpec((bb, rows, LANE), lambda i: (i, 0, 0))],
        out_specs=pl.BlockSpec((bb, 8, D_MODEL), lambda i: (i, 0, 0)),
        out_shape=jax.ShapeDtypeStruct((nseq, 8, D_MODEL), F32),
        compiler_params=_params(("arbitrary",), 48),
        name="attn_cache",
    )(q8, k_rows, v_rows)


def _oproj_kernel(o_ref, x1_ref, wo_ref, gf_ref, x2_ref, h_ref):
    x2 = x1_ref[...] + _dot(o_ref[...], wo_ref[...])
    x2_ref[...] = x2
    h_ref[...] = _rms(x2, gf_ref[...]).astype(BF16)


def _oproj(o, x1, w_o, g_ffn, tm):
    rows = x1.shape[0]
    row_spec = pl.BlockSpec((tm, D_MODEL), lambda i: (i, 0))
    return pl.pallas_call(
        _oproj_kernel,
        grid=(rows // tm,),
        in_specs=[row_spec, row_spec, _const_spec((D_MODEL, D_MODEL)), _const_spec((1, D_MODEL))],
        out_specs=[row_spec, row_spec],
        out_shape=[jax.ShapeDtypeStruct((rows, D_MODEL), F32),
                   jax.ShapeDtypeStruct((rows, D_MODEL), BF16)],
        compiler_params=_params(("arbitrary",), 40),
        name="oproj",
    )(o, x1, w_o, g_ffn)


def _ffn_kernel(x_ref, h_ref, wg_ref, wu_ref, wd_ref, gl_ref, y_ref, *bf16_out):
    f = pl.program_id(1)

    @pl.when(f == 0)
    def _():
        y_ref[...] = jnp.zeros_like(y_ref)

    wg, wu, wd = (w[...].astype(BF16) for w in (wg_ref, wu_ref, wd_ref))
    for ref, w in zip(bf16_out, (wg, wu, wd)):
        ref[...] = w
    h = h_ref[...]
    z = (jax.nn.silu(_dot(h, wg)) * _dot(h, wu)).astype(BF16)
    y_ref[...] += _dot(z, wd)

    @pl.when(f == pl.num_programs(1) - 1)
    def _():
        y_ref[...] = _rms(x_ref[...] + y_ref[...], gl_ref[...])


def _ffn(x2, h, w_gate, w_up, w_down, g_final, tm, tf, emit_bf16=False):
    rows = x2.shape[0]
    nf = D_FF // tf
    row_spec = pl.BlockSpec((tm, D_MODEL), lambda i, f: (i, 0))
    out_specs = [pl.BlockSpec((tm, D_MODEL), lambda i, f: (i, 0), pipeline_mode=pl.Buffered(1))]
    out_shape = [jax.ShapeDtypeStruct((rows, D_MODEL), F32)]
    if emit_bf16:
        once = lambda i, f: jnp.where(i == 0, f, nf)
        out_specs += [pl.BlockSpec((D_MODEL, tf), lambda i, f: (0, once(i, f))),
                      pl.BlockSpec((D_MODEL, tf), lambda i, f: (0, once(i, f))),
                      pl.BlockSpec((tf, D_MODEL), lambda i, f: (once(i, f), 0))]
        out_shape += [jax.ShapeDtypeStruct((D_MODEL, D_FF + tf), BF16),
                      jax.ShapeDtypeStruct((D_MODEL, D_FF + tf), BF16),
                      jax.ShapeDtypeStruct((D_FF + tf, D_MODEL), BF16)]
    return pl.pallas_call(
        _ffn_kernel,
        grid=(rows // tm, nf),
        in_specs=[row_spec, row_spec,
                  pl.BlockSpec((D_MODEL, tf), lambda i, f: (0, f)),
                  pl.BlockSpec((D_MODEL, tf), lambda i, f: (0, f)),
                  pl.BlockSpec((tf, D_MODEL), lambda i, f: (f, 0)),
                  _const_spec((1, D_MODEL))],
        out_specs=out_specs,
        out_shape=out_shape,
        compiler_params=_params(("arbitrary", "arbitrary"), 60 if emit_bf16 else 56),
        name="ffn",
    )(x2, h, w_gate, w_up, w_down, g_final)


def _ssm_param_layouts(lam_re, lam_im, log_step, b_re, b_im, c_re, c_im):
    G, P = lam_re.shape
    ls = jnp.broadcast_to(log_step[:, None], (G, P))
    rep = lambda a: jnp.repeat(a, SSM_GROUP, axis=0)
    b_rows = lambda b: jnp.swapaxes(b, 1, 2).reshape(G * SSM_GROUP, P)
    row5 = jnp.stack([rep(lam_re), rep(lam_im), rep(ls), b_rows(b_re), b_rows(b_im)])
    col5 = jnp.stack([rep(lam_re).T, rep(lam_im).T, rep(ls).T,
                      c_re.reshape(G * SSM_GROUP, P).T, c_im.reshape(G * SSM_GROUP, P).T])
    flat = lambda a: a.reshape(N_LANE_BLOCKS, 1, STATE_BLOCK)
    flat3 = jnp.stack([flat(lam_re), flat(lam_im), flat(ls)])
    return row5, col5, flat3


def _states_to_blocks(h_re, h_im):
    S = h_re.shape[0]
    blk = lambda h: h.reshape(S, N_LANE_BLOCKS, STATE_BLOCK).transpose(1, 0, 2)
    return jnp.concatenate([blk(h_re), blk(h_im)], axis=-1)


def _blocks_to_states(st):
    S = st.shape[1]
    unblk = lambda a: a.transpose(1, 0, 2).reshape(1, S, N_SSM_GROUPS, SSM_STATE)
    return unblk(st[:, :, :STATE_BLOCK]), unblk(st[:, :, STATE_BLOCK:])


def kernel(x_prompt, x_sample, mem_prompt, state_pool_buf, state_ssm_re, state_ssm_im, cache_mem_k, cache_mem_v, g_mix, w_in, w_pool, pool_scale, ssm_lam_re, ssm_lam_im, ssm_log_step, ssm_b_re, ssm_b_im, ssm_c_re, ssm_c_im, ssm_d, w_glu, b_glu, w_out, g_cross, g_mem, w_q, w_k, w_v, w_o, g_ffn, w_gate, w_up, w_down, g_final):
    assert g_mix.shape[0] == 1, "single-layer step"
    B, T, _ = x_prompt.shape
    S, Ts, _ = x_sample.shape
    bf = lambda w: w.astype(BF16)
    vec = lambda v: v.reshape(1, -1)

    w_in_b, w_pool_b, w_glu_b, w_out_b = bf(w_in[0]), bf(w_pool[0]), bf(w_glu[0]), bf(w_out[0])
    w_q_b, w_o_b = bf(w_q[0]), bf(w_o[0])

    row5, col5, flat3 = _ssm_param_layouts(ssm_lam_re[0], ssm_lam_im[0], ssm_log_step[0],
                                           ssm_b_re[0], ssm_b_im[0], ssm_c_re[0], ssm_c_im[0])
    d_blocks = ssm_d[0].reshape(N_LANE_BLOCKS, 1, LANE)

    gains = dict(g_mix=vec(g_mix[0]), scale=vec(pool_scale[0]), b_glu=vec(b_glu[0]), g_cross=vec(g_cross[0]),
                 g_ffn=vec(g_ffn[0]), g_final=vec(g_final))

    chunk_ops = _ssm_prep(row5, col5, flat3, sorted({PROMPT_CHUNK, Ts}, reverse=True))

    def ssm(u_view, L, h0_blocks, **tiles):
        g_act, st = _ssm_mix(u_view, chunk_ops[L], jnp.tile(d_blocks, (1, 1, L)), h0_blocks, L=L, **tiles)
        rows = u_view.shape[1] * u_view.shape[2]
        return g_act.reshape(N_LANE_BLOCKS, rows, LANE), _blocks_to_states(st)

    def finish(o, x1, ffn_weights, ffn_tiles, **kw):
        x2, hf = _oproj(o, x1, w_o_b, gains["g_ffn"], ROW_TILE)
        return _ffn(x2, hf, *ffn_weights, gains["g_final"], *ffn_tiles, **kw)

    mem = mem_prompt.reshape(B * N_MEM, D_MODEL)
    k_rows, k_b = _mem_kv(mem, vec(g_mem[0]), w_k[0], ROW_TILE)
    v_rows, v_b = _mem_kv(mem, vec(g_mem[0]), w_v[0], ROW_TILE)
    xp = x_prompt.reshape(B * T, D_MODEL)
    us_p, po_p, tail_p = _in_proj_pool(xp, gains["g_mix"], w_in_b, w_pool_b, gains["scale"],
                                       tc=ROW_TILE, rows_per_seq=T)
    h0_p = jnp.zeros((N_LANE_BLOCKS, B, 2 * STATE_BLOCK), F32)
    g_p, (re_p, im_p) = ssm(us_p.reshape(N_LANE_BLOCKS, B, T, LANE), PROMPT_CHUNK, h0_p,
                            Rn=PROMPT_CHUNK_ROWS, seq=True)
    x1_p, o_p = _mix_out(g_p, po_p, xp, w_glu_b, gains["b_glu"], w_out_b, gains["g_cross"], w_q_b, ROW_TILE,
                         mem_kv=(k_b, v_b), rows_per_seq=T)
    yp, *ffn_bf16 = finish(o_p, x1_p, (w_gate[0], w_up[0], w_down[0]), PROMPT_FFN_TILES, emit_bf16=True)
    pb_p = tail_p[:, POOL_HIST - POOL_BUF:]

    assert PAST_LEN >= POOL_BUF
    xs = x_sample.reshape(S * Ts, D_MODEL)
    u_s = _in_proj(xs, gains["g_mix"], w_in_b, ROW_TILE)
    po_s, buf_tm = _pool_mix_buf(u_s, state_pool_buf[0].transpose(1, 0, 2), w_pool_b, gains["scale"], L=Ts)
    pb_s = buf_tm.transpose(1, 0, 2)
    h0_s = _states_to_blocks(state_ssm_re[0], state_ssm_im[0])
    g_s, (re_s, im_s) = ssm(u_s.reshape(MIX_LANE_BLOCKS, 1, S * Ts, LANE), Ts, h0_s, Rn=S, seq=False)
    x1_s, q_s = _mix_out(g_s, po_s, xs, w_glu_b, gains["b_glu"], w_out_b, gains["g_cross"], w_q_b, ROW_TILE)
    q8 = jnp.pad(q_s.reshape(S, Ts, D_MODEL).astype(F32), ((0, 0), (0, 8 - Ts), (0, 0)))
    o8 = _attn_cache(q8, _cache_rows_view(cache_mem_k[0]), _cache_rows_view(cache_mem_v[0]), CACHE_SEQS_PER_STEP)
    (ys,) = finish(o8[:, :Ts].reshape(S * Ts, D_MODEL).astype(BF16), x1_s, ffn_bf16, SAMPLE_FFN_TILES)

    mk = _cache_rows_unview(k_rows, B)[None]
    mv = _cache_rows_unview(v_rows, B)[None]
    return (yp.reshape(B, T, D_MODEL), ys.reshape(S, Ts, D_MODEL), pb_p[None], re_p, im_p, mk, mv,
            pb_s[None], re_s, im_s)
```

```python
import functools

import jax
import jax.numpy as jnp
from jax import lax
from jax.experimental import pallas as pl
from jax.experimental.pallas import tpu as pltpu

F32 = jnp.float32
BF16 = jnp.bfloat16

D_MODEL = 2048
POOL_WIDTH = 1024
SSM_WIDTH = 1024
POOL_WINDOWS = (2, 4, 8, 16)
POOL_GROUP = POOL_WIDTH // len(POOL_WINDOWS)
POOL_BUF = max(POOL_WINDOWS) - 1
SSM_GROUP = 16
N_SSM_GROUPS = SSM_WIDTH // SSM_GROUP
SSM_STATE = 64
N_MEM = 256
N_XHEADS = 4
XHEAD_DIM = D_MODEL // N_XHEADS
D_FF = 5632
EPS = 1e-6
PAST_LEN = 16384

LANE = 128
N_LANE_BLOCKS = SSM_WIDTH // LANE
GROUPS_PER_BLOCK = LANE // SSM_GROUP
STATE_BLOCK = GROUPS_PER_BLOCK * SSM_STATE
MIB = 1024 * 1024
HI = lax.Precision.HIGHEST

ROW_TILE = 512
ROW_GROUPS = 4
PROMPT_CHUNK = 8
PROMPT_CHUNK_ROWS = 128
PROMPT_FFN_TILES = (1024, 256)
SAMPLE_FFN_TILES = (512, 512)
CACHE_SEQS_PER_STEP = 4


def _params(semantics, vmem_mib):
    return pltpu.CompilerParams(dimension_semantics=semantics, vmem_limit_bytes=vmem_mib * MIB)


def _const_spec(shape):
    return pl.BlockSpec(shape, lambda *_: (0,) * len(shape), pipeline_mode=pl.Buffered(1))


def _rms(x, g):
    r = lax.rsqrt(jnp.mean(x * x, axis=-1, keepdims=True) + EPS)
    return x * r * g


def _dot(a, b):
    return jnp.dot(a, b, preferred_element_type=F32)


MIX_LANE_BLOCKS = D_MODEL // LANE


def _in_proj_kernel(x_ref, g_ref, w_ref, u_ref):
    h = _rms(x_ref[...], g_ref[...]).astype(BF16)
    u = _dot(h, w_ref[...])
    for j in range(MIX_LANE_BLOCKS):
        u_ref[j] = u[:, j * LANE:(j + 1) * LANE]


def _in_proj(x, g, w, tm):
    rows = x.shape[0]
    return pl.pallas_call(
        _in_proj_kernel,
        grid=(rows // tm,),
        in_specs=[pl.BlockSpec((tm, D_MODEL), lambda i: (i, 0)),
                  _const_spec((1, D_MODEL)),
                  _const_spec((D_MODEL, D_MODEL))],
        out_specs=pl.BlockSpec((MIX_LANE_BLOCKS, tm, LANE), lambda i: (0, i, 0)),
        out_shape=jax.ShapeDtypeStruct((MIX_LANE_BLOCKS, rows, LANE), F32),
        compiler_params=_params(("arbitrary",), 40),
        name="in_proj",
    )(x, g, w)


def _lane_blocks(ref, rows=slice(None)):
    return jnp.concatenate([ref[j, rows, :] for j in range(ref.shape[0])], axis=1)


POOL_HIST = POOL_BUF + 1


def _in_proj_pool_kernel(x_ref, g_ref, w_ref, wp_ref, sc_ref, us_ref, o_ref, tail_ref, ext_ref, *, tc,
                         tiles_per_seq):
    it = lax.rem(pl.program_id(0), tiles_per_seq)

    @pl.when(it == 0)
    def _():
        ext_ref[:POOL_HIST, :] = jnp.zeros((POOL_HIST, POOL_WIDTH), F32)

    nq = ROW_GROUPS
    rq = tc // nq
    for c in range(nq):
        rows = slice(c * rq, (c + 1) * rq)
        u = _dot(_rms(x_ref[rows, :], g_ref[...]).astype(BF16), w_ref[...])
        for j in range(N_LANE_BLOCKS):
            us_ref[j, rows, :] = u[:, POOL_WIDTH + j * LANE:POOL_WIDTH + (j + 1) * LANE]
        ext_ref[POOL_HIST + c * rq:POOL_HIST + (c + 1) * rq, :] = u[:, :POOL_WIDTH]
    for c in range(nq):
        rows = slice(c * rq, (c + 1) * rq)
        pos = it * tc + c * rq + lax.broadcasted_iota(jnp.int32, (rq, 1), 0)
        for g, w in enumerate(POOL_WINDOWS):
            sl = slice(g * POOL_GROUP, (g + 1) * POOL_GROUP)
            z = ext_ref[c * rq:(c + 1) * rq + POOL_HIST, sl]
            s, span = z, 1
            while span < w:
                s = s + pltpu.roll(s, span, 0)
                span *= 2
            cnt = jnp.minimum(pos + 1, w).astype(F32)
            pooled = (s[POOL_HIST:] / cnt - z[POOL_HIST:]).astype(BF16)
            out = _dot(pooled, wp_ref[g]) * sc_ref[:, sl]
            for jj in range(POOL_GROUP // LANE):
                o_ref[g * (POOL_GROUP // LANE) + jj, rows, :] = out[:, jj * LANE:(jj + 1) * LANE]
    last = ext_ref[tc:tc + POOL_HIST, :]
    tail_ref[0] = last
    ext_ref[:POOL_HIST, :] = last


POOL_LANE_BLOCKS = POOL_WIDTH // LANE


def _in_proj_pool(x, g, w, w_pool, scale, *, tc, rows_per_seq):
    rows = x.shape[0]
    tiles_per_seq = rows_per_seq // tc
    slab_spec = pl.BlockSpec((N_LANE_BLOCKS, tc, LANE), lambda i: (0, i, 0))
    slabs = jax.ShapeDtypeStruct((N_LANE_BLOCKS, rows, LANE), F32)
    return pl.pallas_call(
        functools.partial(_in_proj_pool_kernel, tc=tc, tiles_per_seq=tiles_per_seq),
        grid=(rows // tc,),
        in_specs=[pl.BlockSpec((tc, D_MODEL), lambda i: (i, 0)),
                  _const_spec((1, D_MODEL)),
                  _const_spec((D_MODEL, D_MODEL)),
                  _const_spec((len(POOL_WINDOWS), POOL_GROUP, POOL_GROUP)),
                  _const_spec((1, POOL_WIDTH))],
        out_specs=[slab_spec, slab_spec,
                   pl.BlockSpec((1, POOL_HIST, POOL_WIDTH), lambda i: (i // tiles_per_seq, 0, 0))],
        out_shape=[slabs, slabs,
                   jax.ShapeDtypeStruct((rows // rows_per_seq, POOL_HIST, POOL_WIDTH), F32)],
        scratch_shapes=[pltpu.VMEM((tc + POOL_HIST, POOL_WIDTH), F32)],
        compiler_params=_params(("arbitrary",), 48),
        name="in_proj_pool",
    )(x, g, w, w_pool, scale)


def _pool_buf_kernel(u_ref, buf_ref, wp_ref, sc_ref, o_ref, nb_ref, *, L):
    nseq = buf_ref.shape[1]
    per_group = POOL_GROUP // LANE

    def token_rows(t):
        return pl.ds(t, nseq, stride=L)

    def slab(idx, g):
        if idx < POOL_BUF:
            return buf_ref[idx, :, g * POOL_GROUP:(g + 1) * POOL_GROUP]
        return jnp.concatenate([u_ref[g * per_group + jj, token_rows(idx - POOL_BUF), :]
                                for jj in range(per_group)], axis=1)

    for g, w in enumerate(POOL_WINDOWS):
        sl = slice(g * POOL_GROUP, (g + 1) * POOL_GROUP)
        pooled = []
        for t in range(L):
            acc = slab(POOL_BUF + t, g)
            for back in range(1, w):
                acc = acc + slab(POOL_BUF + t - back, g)
            pooled.append((acc / float(w) - slab(POOL_BUF + t, g)).astype(BF16))
        out = _dot(jnp.concatenate(pooled, axis=0), wp_ref[g]) * sc_ref[:, sl]
        for t in range(L):
            for jj in range(per_group):
                o_ref[g * per_group + jj, token_rows(t), :] = out[t * nseq:(t + 1) * nseq,
                                                                  jj * LANE:(jj + 1) * LANE]
    keep = POOL_BUF - L
    for k in range(keep):
        nb_ref[k] = buf_ref[k + L]
    for t in range(L):
        nb_ref[keep + t] = _lane_blocks(u_ref, token_rows(t))


def _pool_mix_buf(u_blocks, buf_tm, w_pool, scale, *, L):
    rows = u_blocks.shape[1]
    slab_spec = pl.BlockSpec((POOL_LANE_BLOCKS, rows, LANE), lambda i: (0, 0, 0))
    buf_spec = pl.BlockSpec(buf_tm.shape, lambda i: (0, 0, 0))
    return pl.pallas_call(
        functools.partial(_pool_buf_kernel, L=L),
        grid=(1,),
        in_specs=[slab_spec, buf_spec, _const_spec((len(POOL_WINDOWS), POOL_GROUP, POOL_GROUP)),
                  _const_spec((1, POOL_WIDTH))],
        out_specs=[slab_spec, buf_spec],
        out_shape=[jax.ShapeDtypeStruct((POOL_LANE_BLOCKS, rows, LANE), F32),
                   jax.ShapeDtypeStruct(buf_tm.shape, F32)],
        compiler_params=_params(("arbitrary",), 48),
        name="pool_mix_buf",
    )(u_blocks, buf_tm, w_pool, scale)


def _cmul(ar, ai, br, bi):
    return ar * br - ai * bi, ar * bi + ai * br


def _cexp(lam_re, lam_im, log_step):
    delta = jnp.exp(log_step)
    mag = jnp.exp(lam_re * delta)
    ang = lam_im * delta
    return mag * jnp.cos(ang), mag * jnp.sin(ang)


def _ssm_prep_kernel(row_ref, col_ref, flat_ref, *out_refs, chunks):
    L = max(chunks)
    lr, li, ls, br, bi = (row_ref[i] for i in range(5))
    ar, ai = _cexp(lr, li, ls)
    den = lr * lr + li * li
    xr = ar - 1.0
    fr = (xr * lr + ai * li) / den
    fi = (ai * lr - xr * li) / den
    zs = [_cmul(fr, fi, br, bi)]
    for _ in range(1, L):
        zs.append(_cmul(ar, ai, *zs[-1]))

    lrc, lic, lsc, cr, ci = (col_ref[i] for i in range(5))
    acr, aci = _cexp(lrc, lic, lsc)
    xs = []
    cur = (cr, ci)
    for _ in range(L):
        cur = _cmul(acr, aci, *cur)
        xs.append(cur)

    def iota(shape, dim):
        return lax.broadcasted_iota(jnp.int32, shape, dim)

    same_tt = (iota((LANE, LANE), 0) >> 4) == (iota((LANE, LANE), 1) >> 4)
    c_stack = jnp.concatenate([cr, -ci], axis=0)
    zero_tile = jnp.zeros((LANE, LANE), BF16)
    lag = []
    for d in range(L):
        z_stack = jnp.concatenate([zs[d][0], zs[d][1]], axis=1)
        tile = jnp.dot(z_stack, c_stack, precision=HI, preferred_element_type=F32)
        lag.append(jnp.where(same_tt, tile, 0.0).astype(BF16))

    same_w = (iota((LANE, STATE_BLOCK), 0) >> 4) == (iota((LANE, STATE_BLOCK), 1) >> 6)
    w_tiles = []
    for zr, zi in zs:
        wr = jnp.where(same_w, jnp.concatenate([zr] * GROUPS_PER_BLOCK, axis=1), 0.0)
        wi = jnp.where(same_w, jnp.concatenate([zi] * GROUPS_PER_BLOCK, axis=1), 0.0)
        w_tiles.append(jnp.concatenate([wr, wi], axis=1).astype(BF16))

    same_v = (iota((STATE_BLOCK, LANE), 0) >> 6) == (iota((STATE_BLOCK, LANE), 1) >> 4)
    v_tiles = []
    for xr_t, xi_t in xs:
        vr = jnp.where(same_v, jnp.concatenate([xr_t] * GROUPS_PER_BLOCK, axis=0), 0.0)
        vi = jnp.where(same_v, jnp.concatenate([xi_t] * GROUPS_PER_BLOCK, axis=0), 0.0)
        v_tiles.append(jnp.concatenate([vr, -vi], axis=0).astype(BF16))

    flr, fli, fls = (flat_ref[i, 0] for i in range(3))
    far, fai = _cexp(flr, fli, fls)
    powers = [(far, fai)]
    for _ in range(L - 1):
        powers.append(_cmul(far, fai, *powers[-1]))

    for n, Lc in enumerate(chunks):
        t_ref, w_ref, v_ref, al_ref = out_refs[4 * n:4 * n + 4]
        for k in range(Lc):
            for t in range(Lc):
                t_ref[0, k * LANE:(k + 1) * LANE, t * LANE:(t + 1) * LANE] = lag[t - k] if t >= k else zero_tile
            w_ref[0, k * LANE:(k + 1) * LANE, :] = w_tiles[Lc - 1 - k]
        for t in range(Lc):
            v_ref[0, :, t * LANE:(t + 1) * LANE] = v_tiles[t]
        al_ref[0] = jnp.concatenate(powers[Lc - 1], axis=1)


def _ssm_prep(row5, col5, flat3, chunks):
    out_specs, out_shape = [], []
    for L in chunks:
        lk = L * LANE
        for shape, dtype in (((lk, lk), BF16), ((lk, 2 * STATE_BLOCK), BF16), ((2 * STATE_BLOCK, lk), BF16),
                             ((1, 2 * STATE_BLOCK), F32)):
            out_specs.append(pl.BlockSpec((1,) + shape, lambda j: (j, 0, 0)))
            out_shape.append(jax.ShapeDtypeStruct((N_LANE_BLOCKS,) + shape, dtype))
    outs = pl.pallas_call(
        functools.partial(_ssm_prep_kernel, chunks=tuple(chunks)),
        grid=(N_LANE_BLOCKS,),
        in_specs=[pl.BlockSpec((5, LANE, SSM_STATE), lambda j: (0, j, 0)),
                  pl.BlockSpec((5, SSM_STATE, LANE), lambda j: (0, 0, j)),
                  pl.BlockSpec((3, 1, 1, STATE_BLOCK), lambda j: (0, j, 0, 0))],
        out_specs=out_specs,
        out_shape=out_shape,
        compiler_params=_params(("arbitrary",), 40),
        name="ssm_prep",
    )(row5, col5, flat3)
    return {L: tuple(outs[4 * n:4 * n + 4]) for n, L in enumerate(chunks)}


def _ssm_kernel(u_ref, t_ref, w_ref, v_ref, al_ref, d_ref, h0_ref, g_ref, st_ref, *scratch, B, Rn, L, seq):
    M = B * Rn
    sb = STATE_BLOCK
    io_scr = scratch[0]
    for b in range(B):
        for k in range(L):
            io_scr[k, pl.ds(b, Rn, stride=B), :] = u_ref[0, b, pl.ds(k, Rn, stride=L), :]
    u = jnp.concatenate([io_scr[k] for k in range(L)], axis=1)
    ub = u.astype(BF16)
    e = _dot(ub, w_ref[0])
    ar = al_ref[0][:, :sb]
    ai = al_ref[0][:, sb:]

    def advance(s, eb):
        sr, si = s[:, :sb], s[:, sb:]
        return jnp.concatenate([ar * sr - ai * si + eb[:, :sb], ar * si + ai * sr + eb[:, sb:]], axis=1)

    if seq:
        e_scr, s_scr, c_scr = scratch[1:]

        assert 2 * B == 8 and Rn % 2 == 0, "two chunk rows of B sequences fill one sublane tile"

        @pl.when(pl.program_id(1) == 0)
        def _():
            c_scr[:B, :] = h0_ref[0]
            c_scr[B:, :] = jnp.zeros((B, 2 * sb), F32)

        e_scr[...] = e
        top = lax.broadcasted_iota(jnp.int32, (2 * B, 1), 0) < B

        def body(i, s):
            rows = pl.ds(pl.multiple_of(i * 2 * B, 2 * B), 2 * B)
            e2 = e_scr[rows, :]
            mid = pltpu.roll(advance(s, e2), B, 0)
            s_scr[rows, :] = jnp.where(top, s, mid)
            return pltpu.roll(advance(mid, e2), B, 0)

        c_scr[...] = lax.fori_loop(0, Rn // 2, body, c_scr[...], unroll=True)
        st_ref[0] = c_scr[:B, :]
        s_start = s_scr[...]
    else:
        s_start = h0_ref[0]
        st_ref[0] = advance(s_start, e)
    y = _dot(ub, t_ref[0]) + _dot(s_start.astype(BF16), v_ref[0]) + d_ref[0] * u
    g = jax.nn.gelu(y)
    for k in range(L):
        io_scr[k] = g[:, k * LANE:(k + 1) * LANE]
    for b in range(B):
        for k in range(L):
            g_ref[0, b, pl.ds(k, Rn, stride=L), :] = io_scr[k, pl.ds(b, Rn, stride=B), :]


def _ssm_mix(u_view, ops, d_tiles, h0, *, L, Rn, seq):
    t_op, w_op, v_op, a_l = ops
    nj = N_LANE_BLOCKS
    _, B, trows, _ = u_view.shape
    lk = L * LANE
    nseq = h0.shape[1]
    tiles = trows // (Rn * L)
    M = B * Rn
    sb2 = 2 * STATE_BLOCK
    first = u_view.shape[0] - nj
    u_spec = pl.BlockSpec((1, B, Rn * L, LANE), lambda j, i: (j + first, 0, i, 0))
    g_spec = pl.BlockSpec((1, B, Rn * L, LANE), lambda j, i: (j, 0, i, 0))
    scratch = [pltpu.VMEM((L, M, LANE), F32)]
    if seq:
        scratch += [pltpu.VMEM((M, sb2), F32), pltpu.VMEM((M, sb2), F32), pltpu.VMEM((2 * B, sb2), F32)]
    return pl.pallas_call(
        functools.partial(_ssm_kernel, B=B, Rn=Rn, L=L, seq=seq),
        grid=(nj, tiles),
        in_specs=[u_spec,
                  pl.BlockSpec((1, lk, lk), lambda j, i: (j, 0, 0)),
                  pl.BlockSpec((1, lk, sb2), lambda j, i: (j, 0, 0)),
                  pl.BlockSpec((1, sb2, lk), lambda j, i: (j, 0, 0)),
                  pl.BlockSpec((1, 1, sb2), lambda j, i: (j, 0, 0)),
                  pl.BlockSpec((1, 1, lk), lambda j, i: (j, 0, 0)),
                  pl.BlockSpec((1, nseq, sb2), lambda j, i: (j, 0, 0))],
        out_specs=[g_spec, pl.BlockSpec((1, nseq, sb2), lambda j, i: (j, 0, 0))],
        out_shape=[jax.ShapeDtypeStruct((nj,) + u_view.shape[1:], F32),
                   jax.ShapeDtypeStruct((nj, nseq, sb2), F32)],
        scratch_shapes=scratch,
        compiler_params=_params(("arbitrary", "arbitrary"), 48),
        name="ssm_mix_seq" if seq else "ssm_mix_rows",
    )(u_view, t_op, w_op, v_op, a_l, d_tiles, h0)


def _mix_out_kernel(g_ref, po_ref, x_ref, wglu_ref, bglu_ref, wout_ref, gc_ref, wq_ref, *rest, attend):
    if attend:
        k_ref, v_ref, x1_ref, q_ref = rest
    else:
        x1_ref, q_ref = rest
    g = _lane_blocks(g_ref)
    gate = jax.nn.sigmoid(_dot(g.astype(BF16), wglu_ref[...]) + bglu_ref[...])
    ssm_out = (g * gate).astype(BF16)
    mix = (_dot(_lane_blocks(po_ref).astype(BF16), wout_ref[:POOL_WIDTH, :])
           + _dot(ssm_out, wout_ref[POOL_WIDTH:, :]))
    x1 = x_ref[...] + mix
    x1_ref[...] = x1
    q = _dot(_rms(x1, gc_ref[...]).astype(BF16), wq_ref[...]).astype(BF16)
    if attend:
        heads = [slice(h * XHEAD_DIM, (h + 1) * XHEAD_DIM) for h in range(N_XHEADS)]
        scores = [lax.dot_general(q[:, sl], k_ref[:, sl], _NT, preferred_element_type=F32) for sl in heads]
        probs = [_softmax_rows(s * (XHEAD_DIM ** -0.5)).astype(BF16) for s in scores]
        for sl, p in zip(heads, probs):
            q_ref[:, sl] = _dot(p, v_ref[:, sl]).astype(BF16)
    else:
        q_ref[...] = q


def _mix_out(g_rows, pool_out, x, w_glu, b_glu, w_out, g_cross, w_q, tm, mem_kv=None, rows_per_seq=None):
    rows = x.shape[0]
    attend = mem_kv is not None
    extra_specs, extra = [], []
    if attend:
        tiles = rows_per_seq // tm
        kv_spec = pl.BlockSpec((N_MEM, D_MODEL), lambda i: (i // tiles, 0))
        extra_specs, extra = [kv_spec, kv_spec], list(mem_kv)
    return pl.pallas_call(
        functools.partial(_mix_out_kernel, attend=attend),
        grid=(rows // tm,),
        in_specs=[pl.BlockSpec((N_LANE_BLOCKS, tm, LANE), lambda i: (0, i, 0)),
                  pl.BlockSpec((POOL_LANE_BLOCKS, tm, LANE), lambda i: (0, i, 0)),
                  pl.BlockSpec((tm, D_MODEL), lambda i: (i, 0)),
                  _const_spec((SSM_WIDTH, SSM_WIDTH)), _const_spec((1, SSM_WIDTH)),
                  _const_spec((D_MODEL, D_MODEL)), _const_spec((1, D_MODEL)),
                  _const_spec((D_MODEL, D_MODEL))] + extra_specs,
        out_specs=[pl.BlockSpec((tm, D_MODEL), lambda i: (i, 0)),
                   pl.BlockSpec((tm, D_MODEL), lambda i: (i, 0))],
        out_shape=[jax.ShapeDtypeStruct((rows, D_MODEL), F32),
                   jax.ShapeDtypeStruct((rows, D_MODEL), BF16)],
        compiler_params=_params(("arbitrary",), 56),
        name="mix_out_attn" if attend else "mix_out",
    )(g_rows, pool_out, x, w_glu, b_glu, w_out, g_cross, w_q, *extra)


HEAD_LANE_BLOCKS = XHEAD_DIM // LANE
CACHE_ROW_PITCH = N_XHEADS * HEAD_LANE_BLOCKS


def _cache_rows_view(cache):
    S = cache.shape[0]
    c5 = cache.reshape(S, N_MEM, N_XHEADS, HEAD_LANE_BLOCKS, LANE)
    return c5.transpose(0, 1, 3, 2, 4).reshape(S, N_MEM * CACHE_ROW_PITCH, LANE)


def _cache_rows_unview(rows, nseq):
    r5 = rows.reshape(nseq, N_MEM, HEAD_LANE_BLOCKS, N_XHEADS, LANE)
    return r5.transpose(0, 1, 3, 2, 4).reshape(nseq, N_MEM, N_XHEADS, XHEAD_DIM)


def _mem_kv_kernel(m_ref, g_ref, w_ref, rows_ref, kvb_ref):
    tm = m_ref.shape[0]
    kv = _dot(_rms(m_ref[...], g_ref[...]).astype(BF16), w_ref[...].astype(BF16))
    kvb_ref[...] = kv.astype(BF16)
    for c in range(HEAD_LANE_BLOCKS):
        for h in range(N_XHEADS):
            lo = h * XHEAD_DIM + c * LANE
            rows_ref[pl.ds(c * N_XHEADS + h, tm, stride=CACHE_ROW_PITCH), :] = kv[:, lo:lo + LANE]


def _mem_kv(mem, g_mem, w, tm):
    rows = mem.shape[0]
    return pl.pallas_call(
        _mem_kv_kernel,
        grid=(rows // tm,),
        in_specs=[pl.BlockSpec((tm, D_MODEL), lambda i: (i, 0)),
                  _const_spec((1, D_MODEL)),
                  _const_spec((D_MODEL, D_MODEL))],
        out_specs=[pl.BlockSpec((tm * CACHE_ROW_PITCH, LANE), lambda i: (i, 0)),
                   pl.BlockSpec((tm, D_MODEL), lambda i: (i, 0))],
        out_shape=[jax.ShapeDtypeStruct((rows * CACHE_ROW_PITCH, LANE), F32),
                   jax.ShapeDtypeStruct((rows, D_MODEL), BF16)],
        compiler_params=_params(("arbitrary",), 48),
        name="mem_kv",
    )(mem, g_mem, w)


_NT = (((1,), (1,)), ((), ()))


def _softmax_rows(s):
    e = jnp.exp(s - jnp.max(s, axis=-1, keepdims=True))
    return e / jnp.sum(e, axis=-1, keepdims=True)


def _attn_cache_kernel(q_ref, k_ref, v_ref, o_ref, *, bb):
    nc = HEAD_LANE_BLOCKS
    mc = N_MEM * nc
    lane_c = lax.broadcasted_iota(jnp.int32, (8, mc), 1) & (nc - 1)
    owns = []
    for b in range(bb):
        q8 = q_ref[b]
        for h in range(N_XHEADS):
            xk = k_ref[b, pl.ds(h, mc, stride=N_XHEADS), :].astype(BF16)
            qc = jnp.concatenate([q8[:, h * XHEAD_DIM + c * LANE:h * XHEAD_DIM + (c + 1) * LANE]
                                  for c in range(nc)], axis=0).astype(BF16)
            part = lax.dot_general(qc, xk, _NT, preferred_element_type=F32)
            own = part[8 * (nc - 1):]
            for c in range(nc - 2, -1, -1):
                own = jnp.where(lane_c == c, part[8 * c:8 * (c + 1)], own)
            owns.append(own)
    own_all = jnp.concatenate(owns, axis=0)
    lane_all = lax.broadcasted_iota(jnp.int32, own_all.shape, 1) & (nc - 1)
    pair = own_all + pltpu.roll(own_all, 1, 1)
    full = pair + pltpu.roll(pair, 2, 1)
    s = jnp.where(lane_all == nc - 1, full * (XHEAD_DIM ** -0.5), -1e30)
    e = jnp.exp(s - jnp.max(s, axis=1, keepdims=True))
    p = e / jnp.sum(e, axis=1, keepdims=True)
    p2 = p + pltpu.roll(p, mc - 1, 1)
    p4 = p2 + pltpu.roll(p2, mc - 2, 1)
    for b in range(bb):
        xv = jnp.concatenate([v_ref[b, pl.ds(h, mc, stride=N_XHEADS), :] for h in range(N_XHEADS)],
                             axis=1).astype(BF16)
        pm_rows = []
        for h in range(N_XHEADS):
            r0 = 8 * (b * N_XHEADS + h)
            pm_rows += [jnp.where(lane_c == c, p4[r0:r0 + 8], 0.0) for c in range(nc)]
        o_all = _dot(jnp.concatenate(pm_rows, axis=0).astype(BF16), xv)
        for h in range(N_XHEADS):
            for c in range(nc):
                g = h * nc + c
                o_ref[b, :, h * XHEAD_DIM + c * LANE:h * XHEAD_DIM + (c + 1) * LANE] = (
                    o_all[8 * g:8 * (g + 1), h * LANE:(h + 1) * LANE])


def _attn_cache(q8, k_rows, v_rows, bb):
    nseq = q8.shape[0]
    rows = N_MEM * CACHE_ROW_PITCH
    return pl.pallas_call(
        functools.partial(_attn_cache_kernel, bb=bb),
        grid=(nseq // bb,),
        in_specs=[pl.BlockSpec((bb, 8, D_MODEL), lambda i: (i, 0, 0)),
                  pl.BlockSpec((bb, rows, LANE), lambda i: (i, 0, 0)),
                  pl.BlockSpec((bb, rows, LANE), lambda i: (i, 0, 0))],
        out_specs=pl.BlockSpec((bb, 8, D_MODEL), lambda i: (i, 0, 0)),
        out_shape=jax.ShapeDtypeStruct((nseq, 8, D_MODEL), F32),
        compiler_params=_params(("arbitrary",), 48),
        name="attn_cache",
    )(q8, k_rows, v_rows)


def _oproj_kernel(o_ref, x1_ref, wo_ref, gf_ref, x2_ref, h_ref):
    x2 = x1_ref[...] + _dot(o_ref[...], wo_ref[...])
    x2_ref[...] = x2
    h_ref[...] = _rms(x2, gf_ref[...]).astype(BF16)


def _oproj(o, x1, w_o, g_ffn, tm):
    rows = x1.shape[0]
    row_spec = pl.BlockSpec((tm, D_MODEL), lambda i: (i, 0))
    return pl.pallas_call(
        _oproj_kernel,
        grid=(rows // tm,),
        in_specs=[row_spec, row_spec, _const_spec((D_MODEL, D_MODEL)), _const_spec((1, D_MODEL))],
        out_specs=[row_spec, row_spec],
        out_shape=[jax.ShapeDtypeStruct((rows, D_MODEL), F32),
                   jax.ShapeDtypeStruct((rows, D_MODEL), BF16)],
        compiler_params=_params(("arbitrary",), 40),
        name="oproj",
    )(o, x1, w_o, g_ffn)


def _ffn_kernel(x_ref, h_ref, wg_ref, wu_ref, wd_ref, gl_ref, y_ref, *bf16_out):
    f = pl.program_id(1)
    last = pl.num_programs(1) - 1

    def down():
        wg, wu, wd = (w[...].astype(BF16) for w in (wg_ref, wu_ref, wd_ref))
        for ref, w in zip(bf16_out, (wg, wu, wd)):
            ref[...] = w
        h = h_ref[...]
        z = (jax.nn.silu(_dot(h, wg)) * _dot(h, wu)).astype(BF16)
        return _dot(z, wd)

    @pl.when(f == 0)
    def _():
        y_ref[...] = down()

    @pl.when((f > 0) & (f < last))
    def _():
        y_ref[...] += down()

    @pl.when(f == last)
    def _():
        y_ref[...] = _rms(x_ref[...] + (y_ref[...] + down()), gl_ref[...])


def _ffn(x2, h, w_gate, w_up, w_down, g_final, tm, tf, emit_bf16=False):
    rows = x2.shape[0]
    nf = D_FF // tf
    row_spec = pl.BlockSpec((tm, D_MODEL), lambda i, f: (i, 0))
    out_specs = [pl.BlockSpec((tm, D_MODEL), lambda i, f: (i, 0), pipeline_mode=pl.Buffered(1))]
    out_shape = [jax.ShapeDtypeStruct((rows, D_MODEL), F32)]
    if emit_bf16:
        once = lambda i, f: jnp.where(i == 0, f, nf)
        out_specs += [pl.BlockSpec((D_MODEL, tf), lambda i, f: (0, once(i, f))),
                      pl.BlockSpec((D_MODEL, tf), lambda i, f: (0, once(i, f))),
                      pl.BlockSpec((tf, D_MODEL), lambda i, f: (once(i, f), 0))]
        out_shape += [jax.ShapeDtypeStruct((D_MODEL, D_FF + tf), BF16),
                      jax.ShapeDtypeStruct((D_MODEL, D_FF + tf), BF16),
                      jax.ShapeDtypeStruct((D_FF + tf, D_MODEL), BF16)]
    return pl.pallas_call(
        _ffn_kernel,
        grid=(rows // tm, nf),
        in_specs=[row_spec, row_spec,
                  pl.BlockSpec((D_MODEL, tf), lambda i, f: (0, f)),
                  pl.BlockSpec((D_MODEL, tf), lambda i, f: (0, f)),
                  pl.BlockSpec((tf, D_MODEL), lambda i, f: (f, 0)),
                  _const_spec((1, D_MODEL))],
        out_specs=out_specs,
        out_shape=out_shape,
        compiler_params=_params(("arbitrary", "arbitrary"), 60 if emit_bf16 else 56),
        name="ffn",
    )(x2, h, w_gate, w_up, w_down, g_final)


def _ssm_param_layouts(lam_re, lam_im, log_step, b_re, b_im, c_re, c_im):
    G, P = lam_re.shape
    ls = jnp.broadcast_to(log_step[:, None], (G, P))
    rep = lambda a: jnp.repeat(a, SSM_GROUP, axis=0)
    b_rows = lambda b: jnp.swapaxes(b, 1, 2).reshape(G * SSM_GROUP, P)
    row5 = jnp.stack([rep(lam_re), rep(lam_im), rep(ls), b_rows(b_re), b_rows(b_im)])
    col5 = jnp.stack([rep(lam_re).T, rep(lam_im).T, rep(ls).T,
                      c_re.reshape(G * SSM_GROUP, P).T, c_im.reshape(G * SSM_GROUP, P).T])
    flat = lambda a: a.reshape(N_LANE_BLOCKS, 1, STATE_BLOCK)
    flat3 = jnp.stack([flat(lam_re), flat(lam_im), flat(ls)])
    return row5, col5, flat3


def _states_to_blocks(h_re, h_im):
    S = h_re.shape[0]
    blk = lambda h: h.reshape(S, N_LANE_BLOCKS, STATE_BLOCK).transpose(1, 0, 2)
    return jnp.concatenate([blk(h_re), blk(h_im)], axis=-1)


def _blocks_to_states(st):
    S = st.shape[1]
    unblk = lambda a: a.transpose(1, 0, 2).reshape(1, S, N_SSM_GROUPS, SSM_STATE)
    return unblk(st[:, :, :STATE_BLOCK]), unblk(st[:, :, STATE_BLOCK:])


def kernel(x_prompt, x_sample, mem_prompt, state_pool_buf, state_ssm_re, state_ssm_im, cache_mem_k, cache_mem_v, g_mix, w_in, w_pool, pool_scale, ssm_lam_re, ssm_lam_im, ssm_log_step, ssm_b_re, ssm_b_im, ssm_c_re, ssm_c_im, ssm_d, w_glu, b_glu, w_out, g_cross, g_mem, w_q, w_k, w_v, w_o, g_ffn, w_gate, w_up, w_down, g_final):
    assert g_mix.shape[0] == 1, "single-layer step"
    B, T, _ = x_prompt.shape
    S, Ts, _ = x_sample.shape
    bf = lambda w: w.astype(BF16)
    vec = lambda v: v.reshape(1, -1)

    w_in_b, w_pool_b, w_glu_b, w_out_b = bf(w_in[0]), bf(w_pool[0]), bf(w_glu[0]), bf(w_out[0])
    w_q_b, w_o_b = bf(w_q[0]), bf(w_o[0])

    row5, col5, flat3 = _ssm_param_layouts(ssm_lam_re[0], ssm_lam_im[0], ssm_log_step[0],
                                           ssm_b_re[0], ssm_b_im[0], ssm_c_re[0], ssm_c_im[0])
    d_blocks = ssm_d[0].reshape(N_LANE_BLOCKS, 1, LANE)

    gains = dict(g_mix=vec(g_mix[0]), scale=vec(pool_scale[0]), b_glu=vec(b_glu[0]), g_cross=vec(g_cross[0]),
                 g_ffn=vec(g_ffn[0]), g_final=vec(g_final))

    chunk_ops = _ssm_prep(row5, col5, flat3, sorted({PROMPT_CHUNK, Ts}, reverse=True))

    def ssm(u_view, L, h0_blocks, **tiles):
        g_act, st = _ssm_mix(u_view, chunk_ops[L], jnp.tile(d_blocks, (1, 1, L)), h0_blocks, L=L, **tiles)
        rows = u_view.shape[1] * u_view.shape[2]
        return g_act.reshape(N_LANE_BLOCKS, rows, LANE), _blocks_to_states(st)

    def finish(o, x1, ffn_weights, ffn_tiles, **kw):
        x2, hf = _oproj(o, x1, w_o_b, gains["g_ffn"], ROW_TILE)
        return _ffn(x2, hf, *ffn_weights, gains["g_final"], *ffn_tiles, **kw)

    mem = mem_prompt.reshape(B * N_MEM, D_MODEL)
    k_rows, k_b = _mem_kv(mem, vec(g_mem[0]), w_k[0], ROW_TILE)
    v_rows, v_b = _mem_kv(mem, vec(g_mem[0]), w_v[0], ROW_TILE)
    xp = x_prompt.reshape(B * T, D_MODEL)
    us_p, po_p, tail_p = _in_proj_pool(xp, gains["g_mix"], w_in_b, w_pool_b, gains["scale"],
                                       tc=ROW_TILE, rows_per_seq=T)
    h0_p = jnp.zeros((N_LANE_BLOCKS, B, 2 * STATE_BLOCK), F32)
    g_p, (re_p, im_p) = ssm(us_p.reshape(N_LANE_BLOCKS, B, T, LANE), PROMPT_CHUNK, h0_p,
                            Rn=PROMPT_CHUNK_ROWS, seq=True)
    x1_p, o_p = _mix_out(g_p, po_p, xp, w_glu_b, gains["b_glu"], w_out_b, gains["g_cross"], w_q_b, ROW_TILE,
                         mem_kv=(k_b, v_b), rows_per_seq=T)
    yp, *ffn_bf16 = finish(o_p, x1_p, (w_gate[0], w_up[0], w_down[0]), PROMPT_FFN_TILES, emit_bf16=True)
    pb_p = tail_p[:, POOL_HIST - POOL_BUF:]

    assert PAST_LEN >= POOL_BUF
    xs = x_sample.reshape(S * Ts, D_MODEL)
    u_s = _in_proj(xs, gains["g_mix"], w_in_b, ROW_TILE)
    po_s, buf_tm = _pool_mix_buf(u_s, state_pool_buf[0].transpose(1, 0, 2), w_pool_b, gains["scale"], L=Ts)
    pb_s = buf_tm.transpose(1, 0, 2)
    h0_s = _states_to_blocks(state_ssm_re[0], state_ssm_im[0])
    g_s, (re_s, im_s) = ssm(u_s.reshape(MIX_LANE_BLOCKS, 1, S * Ts, LANE), Ts, h0_s, Rn=S, seq=False)
    x1_s, q_s = _mix_out(g_s, po_s, xs, w_glu_b, gains["b_glu"], w_out_b, gains["g_cross"], w_q_b, ROW_TILE)
    q8 = jnp.pad(q_s.reshape(S, Ts, D_MODEL).astype(F32), ((0, 0), (0, 8 - Ts), (0, 0)))
    o8 = _attn_cache(q8, _cache_rows_view(cache_mem_k[0]), _cache_rows_view(cache_mem_v[0]), CACHE_SEQS_PER_STEP)
    (ys,) = finish(o8[:, :Ts].reshape(S * Ts, D_MODEL).astype(BF16), x1_s, ffn_bf16, SAMPLE_FFN_TILES)

    mk = _cache_rows_unview(k_rows, B)[None]
    mv = _cache_rows_unview(v_rows, B)[None]
    return (yp.reshape(B, T, D_MODEL), ys.reshape(S, Ts, D_MODEL), pb_p[None], re_p, im_p, mk, mv,
            pb_s[None], re_s, im_s)
```

```python
import functools

import jax
import jax.numpy as jnp
from jax import lax
from jax.experimental import pallas as pl
from jax.experimental.pallas import tpu as pltpu

F32 = jnp.float32
BF16 = jnp.bfloat16

D_MODEL = 2048
POOL_WIDTH = 1024
SSM_WIDTH = 1024
POOL_WINDOWS = (2, 4, 8, 16)
POOL_GROUP = POOL_WIDTH // len(POOL_WINDOWS)
POOL_BUF = max(POOL_WINDOWS) - 1
SSM_GROUP = 16
N_SSM_GROUPS = SSM_WIDTH // SSM_GROUP
SSM_STATE = 64
N_MEM = 256
N_XHEADS = 4
XHEAD_DIM = D_MODEL // N_XHEADS
D_FF = 5632
EPS = 1e-6
PAST_LEN = 16384

LANE = 128
N_LANE_BLOCKS = SSM_WIDTH // LANE
GROUPS_PER_BLOCK = LANE // SSM_GROUP
STATE_BLOCK = GROUPS_PER_BLOCK * SSM_STATE
MIB = 1024 * 1024
HI = lax.Precision.HIGHEST

ROW_TILE = 512
ROW_GROUPS = 4
PROMPT_CHUNK = 8
PROMPT_CHUNK_ROWS = 128
PROMPT_FFN_TILES = (1024, 256)
SAMPLE_FFN_TILES = (512, 512)
CACHE_SEQS_PER_STEP = 4


def _params(semantics, vmem_mib):
    return pltpu.CompilerParams(dimension_semantics=semantics, vmem_limit_bytes=vmem_mib * MIB)


def _const_spec(shape):
    return pl.BlockSpec(shape, lambda *_: (0,) * len(shape), pipeline_mode=pl.Buffered(1))


def _rms(x, g):
    r = lax.rsqrt(jnp.mean(x * x, axis=-1, keepdims=True) + EPS)
    return x * r * g


def _dot(a, b):
    return jnp.dot(a, b, preferred_element_type=F32)


MIX_LANE_BLOCKS = D_MODEL // LANE


def _in_proj_kernel(x_ref, g_ref, w_ref, u_ref):
    h = _rms(x_ref[...], g_ref[...]).astype(BF16)
    u = _dot(h, w_ref[...])
    for j in range(MIX_LANE_BLOCKS):
        u_ref[j] = u[:, j * LANE:(j + 1) * LANE]


def _in_proj(x, g, w, tm):
    rows = x.shape[0]
    return pl.pallas_call(
        _in_proj_kernel,
        grid=(rows // tm,),
        in_specs=[pl.BlockSpec((tm, D_MODEL), lambda i: (i, 0)),
                  _const_spec((1, D_MODEL)),
                  _const_spec((D_MODEL, D_MODEL))],
        out_specs=pl.BlockSpec((MIX_LANE_BLOCKS, tm, LANE), lambda i: (0, i, 0)),
        out_shape=jax.ShapeDtypeStruct((MIX_LANE_BLOCKS, rows, LANE), F32),
        compiler_params=_params(("arbitrary",), 40),
        name="in_proj",
    )(x, g, w)


def _lane_blocks(ref, rows=slice(None)):
    return jnp.concatenate([ref[j, rows, :] for j in range(ref.shape[0])], axis=1)


POOL_HIST = POOL_BUF + 1


def _in_proj_pool_kernel(x_ref, g_ref, w_ref, wp_ref, sc_ref, us_ref, o_ref, tail_ref, ext_ref, *, tc,
                         tiles_per_seq):
    it = lax.rem(pl.program_id(0), tiles_per_seq)

    @pl.when(it == 0)
    def _():
        ext_ref[:POOL_HIST, :] = jnp.zeros((POOL_HIST, POOL_WIDTH), F32)

    nq = ROW_GROUPS
    rq = tc // nq
    for c in range(nq):
        rows = slice(c * rq, (c + 1) * rq)
        u = _dot(_rms(x_ref[rows, :], g_ref[...]).astype(BF16), w_ref[...])
        for j in range(N_LANE_BLOCKS):
            us_ref[j, rows, :] = u[:, POOL_WIDTH + j * LANE:POOL_WIDTH + (j + 1) * LANE]
        ext_ref[POOL_HIST + c * rq:POOL_HIST + (c + 1) * rq, :] = u[:, :POOL_WIDTH]
    for c in range(nq):
        rows = slice(c * rq, (c + 1) * rq)
        pos = it * tc + c * rq + lax.broadcasted_iota(jnp.int32, (rq, 1), 0)
        for g, w in enumerate(POOL_WINDOWS):
            sl = slice(g * POOL_GROUP, (g + 1) * POOL_GROUP)
            z = ext_ref[c * rq:(c + 1) * rq + POOL_HIST, sl]
            s, span = z, 1
            while span < w:
                s = s + pltpu.roll(s, span, 0)
                span *= 2
            cnt = jnp.minimum(pos + 1, w).astype(F32)
            pooled = (s[POOL_HIST:] / cnt - z[POOL_HIST:]).astype(BF16)
            out = _dot(pooled, wp_ref[g]) * sc_ref[:, sl]
            for jj in range(POOL_GROUP // LANE):
                o_ref[g * (POOL_GROUP // LANE) + jj, rows, :] = out[:, jj * LANE:(jj + 1) * LANE]
    last = ext_ref[tc:tc + POOL_HIST, :]
    tail_ref[0] = last
    ext_ref[:POOL_HIST, :] = last


POOL_LANE_BLOCKS = POOL_WIDTH // LANE


def _in_proj_pool(x, g, w, w_pool, scale, *, tc, rows_per_seq):
    rows = x.shape[0]
    tiles_per_seq = rows_per_seq // tc
    slab_spec = pl.BlockSpec((N_LANE_BLOCKS, tc, LANE), lambda i: (0, i, 0))
    slabs = jax.ShapeDtypeStruct((N_LANE_BLOCKS, rows, LANE), F32)
    return pl.pallas_call(
        functools.partial(_in_proj_pool_kernel, tc=tc, tiles_per_seq=tiles_per_seq),
        grid=(rows // tc,),
        in_specs=[pl.BlockSpec((tc, D_MODEL), lambda i: (i, 0)),
                  _const_spec((1, D_MODEL)),
                  _const_spec((D_MODEL, D_MODEL)),
                  _const_spec((len(POOL_WINDOWS), POOL_GROUP, POOL_GROUP)),
                  _const_spec((1, POOL_WIDTH))],
        out_specs=[slab_spec, slab_spec,
                   pl.BlockSpec((1, POOL_HIST, POOL_WIDTH), lambda i: (i // tiles_per_seq, 0, 0))],
        out_shape=[slabs, slabs,
                   jax.ShapeDtypeStruct((rows // rows_per_seq, POOL_HIST, POOL_WIDTH), F32)],
        scratch_shapes=[pltpu.VMEM((tc + POOL_HIST, POOL_WIDTH), F32)],
        compiler_params=_params(("arbitrary",), 48),
        name="in_proj_pool",
    )(x, g, w, w_pool, scale)


def _pool_buf_kernel(u_ref, buf_ref, wp_ref, sc_ref, o_ref, nb_ref, *, L):
    nseq = buf_ref.shape[1]
    per_group = POOL_GROUP // LANE

    def token_rows(t):
        return pl.ds(t, nseq, stride=L)

    def slab(idx, g):
        if idx < POOL_BUF:
            return buf_ref[idx, :, g * POOL_GROUP:(g + 1) * POOL_GROUP]
        return jnp.concatenate([u_ref[g * per_group + jj, token_rows(idx - POOL_BUF), :]
                                for jj in range(per_group)], axis=1)

    for g, w in enumerate(POOL_WINDOWS):
        sl = slice(g * POOL_GROUP, (g + 1) * POOL_GROUP)
        pooled = []
        for t in range(L):
            acc = slab(POOL_BUF + t, g)
            for back in range(1, w):
                acc = acc + slab(POOL_BUF + t - back, g)
            pooled.append((acc / float(w) - slab(POOL_BUF + t, g)).astype(BF16))
        out = _dot(jnp.concatenate(pooled, axis=0), wp_ref[g]) * sc_ref[:, sl]
        for t in range(L):
            for jj in range(per_group):
                o_ref[g * per_group + jj, token_rows(t), :] = out[t * nseq:(t + 1) * nseq,
                                                                  jj * LANE:(jj + 1) * LANE]
    keep = POOL_BUF - L
    for k in range(keep):
        nb_ref[k] = buf_ref[k + L]
    for t in range(L):
        nb_ref[keep + t] = _lane_blocks(u_ref, token_rows(t))


def _pool_mix_buf(u_blocks, buf_tm, w_pool, scale, *, L):
    rows = u_blocks.shape[1]
    slab_spec = pl.BlockSpec((POOL_LANE_BLOCKS, rows, LANE), lambda i: (0, 0, 0))
    buf_spec = pl.BlockSpec(buf_tm.shape, lambda i: (0, 0, 0))
    return pl.pallas_call(
        functools.partial(_pool_buf_kernel, L=L),
        grid=(1,),
        in_specs=[slab_spec, buf_spec, _const_spec((len(POOL_WINDOWS), POOL_GROUP, POOL_GROUP)),
                  _const_spec((1, POOL_WIDTH))],
        out_specs=[slab_spec, buf_spec],
        out_shape=[jax.ShapeDtypeStruct((POOL_LANE_BLOCKS, rows, LANE), F32),
                   jax.ShapeDtypeStruct(buf_tm.shape, F32)],
        compiler_params=_params(("arbitrary",), 48),
        name="pool_mix_buf",
    )(u_blocks, buf_tm, w_pool, scale)


def _cmul(ar, ai, br, bi):
    return ar * br - ai * bi, ar * bi + ai * br


def _cexp(lam_re, lam_im, log_step):
    delta = jnp.exp(log_step)
    mag = jnp.exp(lam_re * delta)
    ang = lam_im * delta
    return mag * jnp.cos(ang), mag * jnp.sin(ang)


def _ssm_prep_kernel(row_ref, col_ref, flat_ref, *out_refs, chunks):
    L = max(chunks)
    lr, li, ls, br, bi = (row_ref[i] for i in range(5))
    ar, ai = _cexp(lr, li, ls)
    den = lr * lr + li * li
    xr = ar - 1.0
    fr = (xr * lr + ai * li) / den
    fi = (ai * lr - xr * li) / den
    zs = [_cmul(fr, fi, br, bi)]
    for _ in range(1, L):
        zs.append(_cmul(ar, ai, *zs[-1]))

    lrc, lic, lsc, cr, ci = (col_ref[i] for i in range(5))
    acr, aci = _cexp(lrc, lic, lsc)
    xs = []
    cur = (cr, ci)
    for _ in range(L):
        cur = _cmul(acr, aci, *cur)
        xs.append(cur)

    def iota(shape, dim):
        return lax.broadcasted_iota(jnp.int32, shape, dim)

    same_tt = (iota((LANE, LANE), 0) >> 4) == (iota((LANE, LANE), 1) >> 4)
    c_stack = jnp.concatenate([cr, -ci], axis=0)
    zero_tile = jnp.zeros((LANE, LANE), BF16)
    lag = []
    for d in range(L):
        z_stack = jnp.concatenate([zs[d][0], zs[d][1]], axis=1)
        tile = jnp.dot(z_stack, c_stack, precision=HI, preferred_element_type=F32)
        lag.append(jnp.where(same_tt, tile, 0.0).astype(BF16))

    same_w = (iota((LANE, STATE_BLOCK), 0) >> 4) == (iota((LANE, STATE_BLOCK), 1) >> 6)
    w_tiles = []
    for zr, zi in zs:
        wr = jnp.where(same_w, jnp.concatenate([zr] * GROUPS_PER_BLOCK, axis=1), 0.0)
        wi = jnp.where(same_w, jnp.concatenate([zi] * GROUPS_PER_BLOCK, axis=1), 0.0)
        w_tiles.append(jnp.concatenate([wr, wi], axis=1).astype(BF16))

    same_v = (iota((STATE_BLOCK, LANE), 0) >> 6) == (iota((STATE_BLOCK, LANE), 1) >> 4)
    v_tiles = []
    for xr_t, xi_t in xs:
        vr = jnp.where(same_v, jnp.concatenate([xr_t] * GROUPS_PER_BLOCK, axis=0), 0.0)
        vi = jnp.where(same_v, jnp.concatenate([xi_t] * GROUPS_PER_BLOCK, axis=0), 0.0)
        v_tiles.append(jnp.concatenate([vr, -vi], axis=0).astype(BF16))

    flr, fli, fls = (flat_ref[i, 0] for i in range(3))
    far, fai = _cexp(flr, fli, fls)
    powers = [(far, fai)]
    for _ in range(L - 1):
        powers.append(_cmul(far, fai, *powers[-1]))

    for n, Lc in enumerate(chunks):
        t_ref, w_ref, v_ref, al_ref = out_refs[4 * n:4 * n + 4]
        for k in range(Lc):
            for t in range(Lc):
                t_ref[0, k * LANE:(k + 1) * LANE, t * LANE:(t + 1) * LANE] = lag[t - k] if t >= k else zero_tile
            w_ref[0, k * LANE:(k + 1) * LANE, :] = w_tiles[Lc - 1 - k]
        for t in range(Lc):
            v_ref[0, :, t * LANE:(t + 1) * LANE] = v_tiles[t]
        al_ref[0] = jnp.concatenate(powers[Lc - 1], axis=1)


def _ssm_prep(row5, col5, flat3, chunks):
    out_specs, out_shape = [], []
    for L in chunks:
        lk = L * LANE
        for shape, dtype in (((lk, lk), BF16), ((lk, 2 * STATE_BLOCK), BF16), ((2 * STATE_BLOCK, lk), BF16),
                             ((1, 2 * STATE_BLOCK), F32)):
            out_specs.append(pl.BlockSpec((1,) + shape, lambda j: (j, 0, 0)))
            out_shape.append(jax.ShapeDtypeStruct((N_LANE_BLOCKS,) + shape, dtype))
    outs = pl.pallas_call(
        functools.partial(_ssm_prep_kernel, chunks=tuple(chunks)),
        grid=(N_LANE_BLOCKS,),
        in_specs=[pl.BlockSpec((5, LANE, SSM_STATE), lambda j: (0, j, 0)),
                  pl.BlockSpec((5, SSM_STATE, LANE), lambda j: (0, 0, j)),
                  pl.BlockSpec((3, 1, 1, STATE_BLOCK), lambda j: (0, j, 0, 0))],
        out_specs=out_specs,
        out_shape=out_shape,
        compiler_params=_params(("arbitrary",), 40),
        name="ssm_prep",
    )(row5, col5, flat3)
    return {L: tuple(outs[4 * n:4 * n + 4]) for n, L in enumerate(chunks)}


def _ssm_kernel(u_ref, t_ref, w_ref, v_ref, al_ref, d_ref, h0_ref, g_ref, st_ref, *scratch, B, Rn, L, seq):
    M = B * Rn
    sb = STATE_BLOCK
    io_scr = scratch[0]
    if seq:
        c_scr = scratch[1]
        assert 2 * B == 8 and Rn % 2 == 0, "two chunk rows of B sequences fill one sublane tile"

        @pl.when(pl.program_id(1) == 0)
        def _():
            c_scr[:B, :] = h0_ref[0]
            c_scr[B:, :] = jnp.zeros((B, 2 * sb), F32)

    for b in range(B):
        for k in range(L):
            io_scr[k, pl.ds(b, Rn, stride=B), :] = u_ref[0, b, pl.ds(k, Rn, stride=L), :]
    u = jnp.concatenate([io_scr[k] for k in range(L)], axis=1)
    ub = u.astype(BF16)
    e = _dot(ub, w_ref[0])
    ar = al_ref[0][:, :sb]
    ai = al_ref[0][:, sb:]

    def advance(s, eb):
        sr, si = s[:, :sb], s[:, sb:]
        return jnp.concatenate([ar * sr - ai * si + eb[:, :sb], ar * si + ai * sr + eb[:, sb:]], axis=1)

    if seq:
        local = _dot(ub, t_ref[0]) + d_ref[0] * u
        top = lax.broadcasted_iota(jnp.int32, (2 * B, 1), 0) < B
        s = c_scr[...]
        starts = []
        for i in range(Rn // 2):
            e2 = e[i * 2 * B:(i + 1) * 2 * B]
            mid = pltpu.roll(advance(s, e2), B, 0)
            starts.append(jnp.where(top, s, mid))
            s = pltpu.roll(advance(mid, e2), B, 0)
        c_scr[...] = s
        st_ref[0] = s[:B, :]
        s_start = jnp.concatenate(starts, axis=0)
    else:
        local = _dot(ub, t_ref[0]) + d_ref[0] * u
        s_start = h0_ref[0]
        st_ref[0] = advance(s_start, e)
    g = jax.nn.gelu(local + _dot(s_start.astype(BF16), v_ref[0]))
    for k in range(L):
        io_scr[k] = g[:, k * LANE:(k + 1) * LANE]
    for b in range(B):
        for k in range(L):
            g_ref[0, b, pl.ds(k, Rn, stride=L), :] = io_scr[k, pl.ds(b, Rn, stride=B), :]


def _ssm_mix(u_view, ops, d_tiles, h0, *, L, Rn, seq):
    t_op, w_op, v_op, a_l = ops
    nj = N_LANE_BLOCKS
    _, B, trows, _ = u_view.shape
    lk = L * LANE
    nseq = h0.shape[1]
    tiles = trows // (Rn * L)
    M = B * Rn
    sb2 = 2 * STATE_BLOCK
    first = u_view.shape[0] - nj
    u_spec = pl.BlockSpec((1, B, Rn * L, LANE), lambda j, i: (j + first, 0, i, 0))
    g_spec = pl.BlockSpec((1, B, Rn * L, LANE), lambda j, i: (j, 0, i, 0))
    scratch = [pltpu.VMEM((L, M, LANE), F32)]
    if seq:
        scratch += [pltpu.VMEM((2 * B, sb2), F32)]
    return pl.pallas_call(
        functools.partial(_ssm_kernel, B=B, Rn=Rn, L=L, seq=seq),
        grid=(nj, tiles),
        in_specs=[u_spec,
                  pl.BlockSpec((1, lk, lk), lambda j, i: (j, 0, 0)),
                  pl.BlockSpec((1, lk, sb2), lambda j, i: (j, 0, 0)),
                  pl.BlockSpec((1, sb2, lk), lambda j, i: (j, 0, 0)),
                  pl.BlockSpec((1, 1, sb2), lambda j, i: (j, 0, 0)),
                  pl.BlockSpec((1, 1, lk), lambda j, i: (j, 0, 0)),
                  pl.BlockSpec((1, nseq, sb2), lambda j, i: (j, 0, 0))],
        out_specs=[g_spec, pl.BlockSpec((1, nseq, sb2), lambda j, i: (j, 0, 0))],
        out_shape=[jax.ShapeDtypeStruct((nj,) + u_view.shape[1:], F32),
                   jax.ShapeDtypeStruct((nj, nseq, sb2), F32)],
        scratch_shapes=scratch,
        compiler_params=_params(("arbitrary", "arbitrary"), 48),
        name="ssm_mix_seq" if seq else "ssm_mix_rows",
    )(u_view, t_op, w_op, v_op, a_l, d_tiles, h0)


def _mix_out_kernel(g_ref, po_ref, x_ref, wglu_ref, bglu_ref, wout_ref, gc_ref, wq_ref, *rest, attend):
    if attend:
        k_ref, v_ref, x1_ref, q_ref = rest
    else:
        x1_ref, q_ref = rest
    g = _lane_blocks(g_ref)
    gate = jax.nn.sigmoid(_dot(g.astype(BF16), wglu_ref[...]) + bglu_ref[...])
    ssm_out = (g * gate).astype(BF16)
    mix = (_dot(_lane_blocks(po_ref).astype(BF16), wout_ref[:POOL_WIDTH, :])
           + _dot(ssm_out, wout_ref[POOL_WIDTH:, :]))
    x1 = x_ref[...] + mix
    x1_ref[...] = x1
    q = _dot(_rms(x1, gc_ref[...]).astype(BF16), wq_ref[...]).astype(BF16)
    if attend:
        heads = [slice(h * XHEAD_DIM, (h + 1) * XHEAD_DIM) for h in range(N_XHEADS)]
        scores = [lax.dot_general(q[:, sl], k_ref[:, sl], _NT, preferred_element_type=F32) for sl in heads]
        probs = [_softmax_rows(s * (XHEAD_DIM ** -0.5)).astype(BF16) for s in scores]
        for sl, p in zip(heads, probs):
            q_ref[:, sl] = _dot(p, v_ref[:, sl]).astype(BF16)
    else:
        q_ref[...] = q


def _mix_out(g_rows, pool_out, x, w_glu, b_glu, w_out, g_cross, w_q, tm, mem_kv=None, rows_per_seq=None):
    rows = x.shape[0]
    attend = mem_kv is not None
    extra_specs, extra = [], []
    if attend:
        tiles = rows_per_seq // tm
        kv_spec = pl.BlockSpec((N_MEM, D_MODEL), lambda i: (i // tiles, 0))
        extra_specs, extra = [kv_spec, kv_spec], list(mem_kv)
    return pl.pallas_call(
        functools.partial(_mix_out_kernel, attend=attend),
        grid=(rows // tm,),
        in_specs=[pl.BlockSpec((N_LANE_BLOCKS, tm, LANE), lambda i: (0, i, 0)),
                  pl.BlockSpec((POOL_LANE_BLOCKS, tm, LANE), lambda i: (0, i, 0)),
                  pl.BlockSpec((tm, D_MODEL), lambda i: (i, 0)),
                  _const_spec((SSM_WIDTH, SSM_WIDTH)), _const_spec((1, SSM_WIDTH)),
                  _const_spec((D_MODEL, D_MODEL)), _const_spec((1, D_MODEL)),
                  _const_spec((D_MODEL, D_MODEL))] + extra_specs,
        out_specs=[pl.BlockSpec((tm, D_MODEL), lambda i: (i, 0)),
                   pl.BlockSpec((tm, D_MODEL), lambda i: (i, 0))],
        out_shape=[jax.ShapeDtypeStruct((rows, D_MODEL), F32),
                   jax.ShapeDtypeStruct((rows, D_MODEL), BF16)],
        compiler_params=_params(("arbitrary",), 56),
        name="mix_out_attn" if attend else "mix_out",
    )(g_rows, pool_out, x, w_glu, b_glu, w_out, g_cross, w_q, *extra)


HEAD_LANE_BLOCKS = XHEAD_DIM // LANE
CACHE_ROW_PITCH = N_XHEADS * HEAD_LANE_BLOCKS


def _cache_rows_view(cache):
    S = cache.shape[0]
    c5 = cache.reshape(S, N_MEM, N_XHEADS, HEAD_LANE_BLOCKS, LANE)
    return c5.transpose(0, 1, 3, 2, 4).reshape(S, N_MEM * CACHE_ROW_PITCH, LANE)


def _cache_rows_unview(rows, nseq):
    r5 = rows.reshape(nseq, N_MEM, HEAD_LANE_BLOCKS, N_XHEADS, LANE)
    return r5.transpose(0, 1, 3, 2, 4).reshape(nseq, N_MEM, N_XHEADS, XHEAD_DIM)


def _mem_kv_kernel(m_ref, g_ref, w_ref, rows_ref, kvb_ref):
    tm = m_ref.shape[0]
    kv = _dot(_rms(m_ref[...], g_ref[...]).astype(BF16), w_ref[...].astype(BF16))
    kvb_ref[...] = kv.astype(BF16)
    for c in range(HEAD_LANE_BLOCKS):
        for h in range(N_XHEADS):
            lo = h * XHEAD_DIM + c * LANE
            rows_ref[pl.ds(c * N_XHEADS + h, tm, stride=CACHE_ROW_PITCH), :] = kv[:, lo:lo + LANE]


def _mem_kv(mem, g_mem, w, tm):
    rows = mem.shape[0]
    return pl.pallas_call(
        _mem_kv_kernel,
        grid=(rows // tm,),
        in_specs=[pl.BlockSpec((tm, D_MODEL), lambda i: (i, 0)),
                  _const_spec((1, D_MODEL)),
                  _const_spec((D_MODEL, D_MODEL))],
        out_specs=[pl.BlockSpec((tm * CACHE_ROW_PITCH, LANE), lambda i: (i, 0)),
                   pl.BlockSpec((tm, D_MODEL), lambda i: (i, 0))],
        out_shape=[jax.ShapeDtypeStruct((rows * CACHE_ROW_PITCH, LANE), F32),
                   jax.ShapeDtypeStruct((rows, D_MODEL), BF16)],
        compiler_params=_params(("arbitrary",), 48),
        name="mem_kv",
    )(mem, g_mem, w)


_NT = (((1,), (1,)), ((), ()))


def _softmax_rows(s):
    e = jnp.exp(s - jnp.max(s, axis=-1, keepdims=True))
    return e / jnp.sum(e, axis=-1, keepdims=True)


def _attn_cache_kernel(q_ref, k_ref, v_ref, o_ref, *, bb):
    nc = HEAD_LANE_BLOCKS
    mc = N_MEM * nc
    lane_c = lax.broadcasted_iota(jnp.int32, (8, mc), 1) & (nc - 1)
    owns = []
    for b in range(bb):
        q8 = q_ref[b]
        for h in range(N_XHEADS):
            xk = k_ref[b, pl.ds(h, mc, stride=N_XHEADS), :].astype(BF16)
            qc = jnp.concatenate([q8[:, h * XHEAD_DIM + c * LANE:h * XHEAD_DIM + (c + 1) * LANE]
                                  for c in range(nc)], axis=0).astype(BF16)
            part = lax.dot_general(qc, xk, _NT, preferred_element_type=F32)
            own = part[8 * (nc - 1):]
            for c in range(nc - 2, -1, -1):
                own = jnp.where(lane_c == c, part[8 * c:8 * (c + 1)], own)
            owns.append(own)
    own_all = jnp.concatenate(owns, axis=0)
    lane_all = lax.broadcasted_iota(jnp.int32, own_all.shape, 1) & (nc - 1)
    pair = own_all + pltpu.roll(own_all, 1, 1)
    full = pair + pltpu.roll(pair, 2, 1)
    s = jnp.where(lane_all == nc - 1, full * (XHEAD_DIM ** -0.5), -1e30)
    e = jnp.exp(s - jnp.max(s, axis=1, keepdims=True))
    p = e / jnp.sum(e, axis=1, keepdims=True)
    p2 = p + pltpu.roll(p, mc - 1, 1)
    p4 = p2 + pltpu.roll(p2, mc - 2, 1)
    for b in range(bb):
        xv = jnp.concatenate([v_ref[b, pl.ds(h, mc, stride=N_XHEADS), :] for h in range(N_XHEADS)],
                             axis=1).astype(BF16)
        pm_rows = []
        for h in range(N_XHEADS):
            r0 = 8 * (b * N_XHEADS + h)
            pm_rows += [jnp.where(lane_c == c, p4[r0:r0 + 8], 0.0) for c in range(nc)]
        o_all = _dot(jnp.concatenate(pm_rows, axis=0).astype(BF16), xv)
        for h in range(N_XHEADS):
            for c in range(nc):
                g = h * nc + c
                o_ref[b, :, h * XHEAD_DIM + c * LANE:h * XHEAD_DIM + (c + 1) * LANE] = (
                    o_all[8 * g:8 * (g + 1), h * LANE:(h + 1) * LANE])


def _attn_cache(q8, k_rows, v_rows, bb):
    nseq = q8.shape[0]
    rows = N_MEM * CACHE_ROW_PITCH
    return pl.pallas_call(
        functools.partial(_attn_cache_kernel, bb=bb),
        grid=(nseq // bb,),
        in_specs=[pl.BlockSpec((bb, 8, D_MODEL), lambda i: (i, 0, 0)),
                  pl.BlockSpec((bb, rows, LANE), lambda i: (i, 0, 0)),
                  pl.BlockSpec((bb, rows, LANE), lambda i: (i, 0, 0))],
        out_specs=pl.BlockSpec((bb, 8, D_MODEL), lambda i: (i, 0, 0)),
        out_shape=jax.ShapeDtypeStruct((nseq, 8, D_MODEL), F32),
        compiler_params=_params(("arbitrary",), 48),
        name="attn_cache",
    )(q8, k_rows, v_rows)


def _oproj_kernel(o_ref, x1_ref, wo_ref, gf_ref, x2_ref, h_ref):
    x2 = x1_ref[...] + _dot(o_ref[...], wo_ref[...])
    x2_ref[...] = x2
    h_ref[...] = _rms(x2, gf_ref[...]).astype(BF16)


def _oproj(o, x1, w_o, g_ffn, tm):
    rows = x1.shape[0]
    row_spec = pl.BlockSpec((tm, D_MODEL), lambda i: (i, 0))
    return pl.pallas_call(
        _oproj_kernel,
        grid=(rows // tm,),
        in_specs=[row_spec, row_spec, _const_spec((D_MODEL, D_MODEL)), _const_spec((1, D_MODEL))],
        out_specs=[row_spec, row_spec],
        out_shape=[jax.ShapeDtypeStruct((rows, D_MODEL), F32),
                   jax.ShapeDtypeStruct((rows, D_MODEL), BF16)],
        compiler_params=_params(("arbitrary",), 40),
        name="oproj",
    )(o, x1, w_o, g_ffn)


def _ffn_kernel(x_ref, h_ref, wg_ref, wu_ref, wd_ref, gl_ref, y_ref, *bf16_out):
    f = pl.program_id(1)
    last = pl.num_programs(1) - 1

    def down():
        wg, wu, wd = (w[...].astype(BF16) for w in (wg_ref, wu_ref, wd_ref))
        for ref, w in zip(bf16_out, (wg, wu, wd)):
            ref[...] = w
        h = h_ref[...]
        z = (jax.nn.silu(_dot(h, wg)) * _dot(h, wu)).astype(BF16)
        return _dot(z, wd)

    @pl.when(f == 0)
    def _():
        y_ref[...] = down()

    @pl.when((f > 0) & (f < last))
    def _():
        y_ref[...] += down()

    @pl.when(f == last)
    def _():
        y_ref[...] = _rms(x_ref[...] + (y_ref[...] + down()), gl_ref[...])


def _ffn(x2, h, w_gate, w_up, w_down, g_final, tm, tf, emit_bf16=False):
    rows = x2.shape[0]
    nf = D_FF // tf
    row_spec = pl.BlockSpec((tm, D_MODEL), lambda i, f: (i, 0))
    out_specs = [pl.BlockSpec((tm, D_MODEL), lambda i, f: (i, 0), pipeline_mode=pl.Buffered(1))]
    out_shape = [jax.ShapeDtypeStruct((rows, D_MODEL), F32)]
    if emit_bf16:
        once = lambda i, f: jnp.where(i == 0, f, nf)
        out_specs += [pl.BlockSpec((D_MODEL, tf), lambda i, f: (0, once(i, f))),
                      pl.BlockSpec((D_MODEL, tf), lambda i, f: (0, once(i, f))),
                      pl.BlockSpec((tf, D_MODEL), lambda i, f: (once(i, f), 0))]
        out_shape += [jax.ShapeDtypeStruct((D_MODEL, D_FF + tf), BF16),
                      jax.ShapeDtypeStruct((D_MODEL, D_FF + tf), BF16),
                      jax.ShapeDtypeStruct((D_FF + tf, D_MODEL), BF16)]
    return pl.pallas_call(
        _ffn_kernel,
        grid=(rows // tm, nf),
        in_specs=[row_spec, row_spec,
                  pl.BlockSpec((D_MODEL, tf), lambda i, f: (0, f)),
                  pl.BlockSpec((D_MODEL, tf), lambda i, f: (0, f)),
                  pl.BlockSpec((tf, D_MODEL), lambda i, f: (f, 0)),
                  _const_spec((1, D_MODEL))],
        out_specs=out_specs,
        out_shape=out_shape,
        compiler_params=_params(("arbitrary", "arbitrary"), 60 if emit_bf16 else 56),
        name="ffn",
    )(x2, h, w_gate, w_up, w_down, g_final)


def _ssm_param_layouts(lam_re, lam_im, log_step, b_re, b_im, c_re, c_im):
    G, P = lam_re.shape
    ls = jnp.broadcast_to(log_step[:, None], (G, P))
    rep = lambda a: jnp.repeat(a, SSM_GROUP, axis=0)
    b_rows = lambda b: jnp.swapaxes(b, 1, 2).reshape(G * SSM_GROUP, P)
    row5 = jnp.stack([rep(lam_re), rep(lam_im), rep(ls), b_rows(b_re), b_rows(b_im)])
    col5 = jnp.stack([rep(lam_re).T, rep(lam_im).T, rep(ls).T,
                      c_re.reshape(G * SSM_GROUP, P).T, c_im.reshape(G * SSM_GROUP, P).T])
    flat = lambda a: a.reshape(N_LANE_BLOCKS, 1, STATE_BLOCK)
    flat3 = jnp.stack([flat(lam_re), flat(lam_im), flat(ls)])
    return row5, col5, flat3


def _states_to_blocks(h_re, h_im):
    S = h_re.shape[0]
    blk = lambda h: h.reshape(S, N_LANE_BLOCKS, STATE_BLOCK).transpose(1, 0, 2)
    return jnp.concatenate([blk(h_re), blk(h_im)], axis=-1)


def _blocks_to_states(st):
    S = st.shape[1]
    unblk = lambda a: a.transpose(1, 0, 2).reshape(1, S, N_SSM_GROUPS, SSM_STATE)
    return unblk(st[:, :, :STATE_BLOCK]), unblk(st[:, :, STATE_BLOCK:])


def kernel(x_prompt, x_sample, mem_prompt, state_pool_buf, state_ssm_re, state_ssm_im, cache_mem_k, cache_mem_v, g_mix, w_in, w_pool, pool_scale, ssm_lam_re, ssm_lam_im, ssm_log_step, ssm_b_re, ssm_b_im, ssm_c_re, ssm_c_im, ssm_d, w_glu, b_glu, w_out, g_cross, g_mem, w_q, w_k, w_v, w_o, g_ffn, w_gate, w_up, w_down, g_final):
    assert g_mix.shape[0] == 1, "single-layer step"
    B, T, _ = x_prompt.shape
    S, Ts, _ = x_sample.shape
    bf = lambda w: w.astype(BF16)
    vec = lambda v: v.reshape(1, -1)

    w_in_b, w_pool_b, w_glu_b, w_out_b = bf(w_in[0]), bf(w_pool[0]), bf(w_glu[0]), bf(w_out[0])
    w_q_b, w_o_b = bf(w_q[0]), bf(w_o[0])

    row5, col5, flat3 = _ssm_param_layouts(ssm_lam_re[0], ssm_lam_im[0], ssm_log_step[0],
                                           ssm_b_re[0], ssm_b_im[0], ssm_c_re[0], ssm_c_im[0])
    d_blocks = ssm_d[0].reshape(N_LANE_BLOCKS, 1, LANE)

    gains = dict(g_mix=vec(g_mix[0]), scale=vec(pool_scale[0]), b_glu=vec(b_glu[0]), g_cross=vec(g_cross[0]),
                 g_ffn=vec(g_ffn[0]), g_final=vec(g_final))

    chunk_ops = _ssm_prep(row5, col5, flat3, sorted({PROMPT_CHUNK, Ts}, reverse=True))

    def ssm(u_view, L, h0_blocks, **tiles):
        g_act, st = _ssm_mix(u_view, chunk_ops[L], jnp.tile(d_blocks, (1, 1, L)), h0_blocks, L=L, **tiles)
        rows = u_view.shape[1] * u_view.shape[2]
        return g_act.reshape(N_LANE_BLOCKS, rows, LANE), _blocks_to_states(st)

    def finish(o, x1, ffn_weights, ffn_tiles, **kw):
        x2, hf = _oproj(o, x1, w_o_b, gains["g_ffn"], ROW_TILE)
        return _ffn(x2, hf, *ffn_weights, gains["g_final"], *ffn_tiles, **kw)

    mem = mem_prompt.reshape(B * N_MEM, D_MODEL)
    k_rows, k_b = _mem_kv(mem, vec(g_mem[0]), w_k[0], ROW_TILE)
    v_rows, v_b = _mem_kv(mem, vec(g_mem[0]), w_v[0], ROW_TILE)
    xp = x_prompt.reshape(B * T, D_MODEL)
    us_p, po_p, tail_p = _in_proj_pool(xp, gains["g_mix"], w_in_b, w_pool_b, gains["scale"],
                                       tc=ROW_TILE, rows_per_seq=T)
    h0_p = jnp.zeros((N_LANE_BLOCKS, B, 2 * STATE_BLOCK), F32)
    g_p, (re_p, im_p) = ssm(us_p.reshape(N_LANE_BLOCKS, B, T, LANE), PROMPT_CHUNK, h0_p,
                            Rn=PROMPT_CHUNK_ROWS, seq=True)
    x1_p, o_p = _mix_out(g_p, po_p, xp, w_glu_b, gains["b_glu"], w_out_b, gains["g_cross"], w_q_b, ROW_TILE,
                         mem_kv=(k_b, v_b), rows_per_seq=T)
    yp, *ffn_bf16 = finish(o_p, x1_p, (w_gate[0], w_up[0], w_down[0]), PROMPT_FFN_TILES, emit_bf16=True)
    pb_p = tail_p[:, POOL_HIST - POOL_BUF:]

    assert PAST_LEN >= POOL_BUF
    xs = x_sample.reshape(S * Ts, D_MODEL)
    u_s = _in_proj(xs, gains["g_mix"], w_in_b, ROW_TILE)
    po_s, buf_tm = _pool_mix_buf(u_s, state_pool_buf[0].transpose(1, 0, 2), w_pool_b, gains["scale"], L=Ts)
    pb_s = buf_tm.transpose(1, 0, 2)
    h0_s = _states_to_blocks(state_ssm_re[0], state_ssm_im[0])
    g_s, (re_s, im_s) = ssm(u_s.reshape(MIX_LANE_BLOCKS, 1, S * Ts, LANE), Ts, h0_s, Rn=S, seq=False)
    x1_s, q_s = _mix_out(g_s, po_s, xs, w_glu_b, gains["b_glu"], w_out_b, gains["g_cross"], w_q_b, ROW_TILE)
    q8 = jnp.pad(q_s.reshape(S, Ts, D_MODEL).astype(F32), ((0, 0), (0, 8 - Ts), (0, 0)))
    o8 = _attn_cache(q8, _cache_rows_view(cache_mem_k[0]), _cache_rows_view(cache_mem_v[0]), CACHE_SEQS_PER_STEP)
    (ys,) = finish(o8[:, :Ts].reshape(S * Ts, D_MODEL).astype(BF16), x1_s, ffn_bf16, SAMPLE_FFN_TILES)

    mk = _cache_rows_unview(k_rows, B)[None]
    mv = _cache_rows_unview(v_rows, B)[None]
    return (yp.reshape(B, T, D_MODEL), ys.reshape(S, Ts, D_MODEL), pb_p[None], re_p, im_p, mk, mv,
            pb_s[None], re_s, im_s)
```

```python
import functools

import jax
import jax.numpy as jnp
from jax import lax
from jax.experimental import pallas as pl
from jax.experimental.pallas import tpu as pltpu

F32 = jnp.float32
BF16 = jnp.bfloat16

D_MODEL = 2048
POOL_WIDTH = 1024
SSM_WIDTH = 1024
POOL_WINDOWS = (2, 4, 8, 16)
POOL_GROUP = POOL_WIDTH // len(POOL_WINDOWS)
POOL_BUF = max(POOL_WINDOWS) - 1
SSM_GROUP = 16
N_SSM_GROUPS = SSM_WIDTH // SSM_GROUP
SSM_STATE = 64
N_MEM = 256
N_XHEADS = 4
XHEAD_DIM = D_MODEL // N_XHEADS
D_FF = 5632
EPS = 1e-6
PAST_LEN = 16384

LANE = 128
N_LANE_BLOCKS = SSM_WIDTH // LANE
GROUPS_PER_BLOCK = LANE // SSM_GROUP
STATE_BLOCK = GROUPS_PER_BLOCK * SSM_STATE
MIB = 1024 * 1024
HI = lax.Precision.HIGHEST

ROW_TILE = 512
ROW_GROUPS = 4
PROMPT_CHUNK = 8
PROMPT_CHUNK_ROWS = 128
PROMPT_FFN_TILES = (1024, 256)
SAMPLE_FFN_TILES = (512, 512)
CACHE_SEQS_PER_STEP = 4


def _params(semantics, vmem_mib):
    return pltpu.CompilerParams(dimension_semantics=semantics, vmem_limit_bytes=vmem_mib * MIB)


def _const_spec(shape):
    return pl.BlockSpec(shape, lambda *_: (0,) * len(shape), pipeline_mode=pl.Buffered(1))


def _rms(x, g):
    r = lax.rsqrt(jnp.mean(x * x, axis=-1, keepdims=True) + EPS)
    return x * r * g


def _dot(a, b):
    return jnp.dot(a, b, preferred_element_type=F32)


MIX_LANE_BLOCKS = D_MODEL // LANE


def _in_proj_kernel(x_ref, g_ref, w_ref, u_ref):
    h = _rms(x_ref[...], g_ref[...]).astype(BF16)
    u = _dot(h, w_ref[...].astype(BF16))
    for j in range(MIX_LANE_BLOCKS):
        u_ref[j] = u[:, j * LANE:(j + 1) * LANE]


def _in_proj(x, g, w, tm):
    rows = x.shape[0]
    return pl.pallas_call(
        _in_proj_kernel,
        grid=(rows // tm,),
        in_specs=[pl.BlockSpec((tm, D_MODEL), lambda i: (i, 0)),
                  _const_spec((1, D_MODEL)),
                  _const_spec((D_MODEL, D_MODEL))],
        out_specs=pl.BlockSpec((MIX_LANE_BLOCKS, tm, LANE), lambda i: (0, i, 0)),
        out_shape=jax.ShapeDtypeStruct((MIX_LANE_BLOCKS, rows, LANE), F32),
        compiler_params=_params(("arbitrary",), 52),
        name="in_proj",
    )(x, g, w)


def _lane_blocks(ref, rows=slice(None)):
    return jnp.concatenate([ref[j, rows, :] for j in range(ref.shape[0])], axis=1)


POOL_HIST = POOL_BUF + 1


def _in_proj_pool_kernel(x_ref, g_ref, w_ref, wp_ref, sc_ref, us_ref, o_ref, tail_ref, ext_ref, *, tc,
                         tiles_per_seq):
    it = lax.rem(pl.program_id(0), tiles_per_seq)

    @pl.when(it == 0)
    def _():
        ext_ref[:POOL_HIST, :] = jnp.zeros((POOL_HIST, POOL_WIDTH), F32)

    nq = ROW_GROUPS
    rq = tc // nq
    w = w_ref[...].astype(BF16)
    for c in range(nq):
        rows = slice(c * rq, (c + 1) * rq)
        u = _dot(_rms(x_ref[rows, :], g_ref[...]).astype(BF16), w)
        for j in range(N_LANE_BLOCKS):
            us_ref[j, rows, :] = u[:, POOL_WIDTH + j * LANE:POOL_WIDTH + (j + 1) * LANE]
        ext_ref[POOL_HIST + c * rq:POOL_HIST + (c + 1) * rq, :] = u[:, :POOL_WIDTH]
    for c in range(nq):
        rows = slice(c * rq, (c + 1) * rq)
        pos = it * tc + c * rq + lax.broadcasted_iota(jnp.int32, (rq, 1), 0)
        for g, w in enumerate(POOL_WINDOWS):
            sl = slice(g * POOL_GROUP, (g + 1) * POOL_GROUP)
            z = ext_ref[c * rq:(c + 1) * rq + POOL_HIST, sl]
            s, span = z, 1
            while span < w:
                s = s + pltpu.roll(s, span, 0)
                span *= 2
            cnt = jnp.minimum(pos + 1, w).astype(F32)
            pooled = (s[POOL_HIST:] / cnt - z[POOL_HIST:]).astype(BF16)
            out = _dot(pooled, wp_ref[g]) * sc_ref[:, sl]
            for jj in range(POOL_GROUP // LANE):
                o_ref[g * (POOL_GROUP // LANE) + jj, rows, :] = out[:, jj * LANE:(jj + 1) * LANE]
    last = ext_ref[tc:tc + POOL_HIST, :]
    tail_ref[0] = last
    ext_ref[:POOL_HIST, :] = last


POOL_LANE_BLOCKS = POOL_WIDTH // LANE


def _in_proj_pool(x, g, w, w_pool, scale, *, tc, rows_per_seq):
    rows = x.shape[0]
    tiles_per_seq = rows_per_seq // tc
    slab_spec = pl.BlockSpec((N_LANE_BLOCKS, tc, LANE), lambda i: (0, i, 0))
    slabs = jax.ShapeDtypeStruct((N_LANE_BLOCKS, rows, LANE), F32)
    return pl.pallas_call(
        functools.partial(_in_proj_pool_kernel, tc=tc, tiles_per_seq=tiles_per_seq),
        grid=(rows // tc,),
        in_specs=[pl.BlockSpec((tc, D_MODEL), lambda i: (i, 0)),
                  _const_spec((1, D_MODEL)),
                  _const_spec((D_MODEL, D_MODEL)),
                  _const_spec((len(POOL_WINDOWS), POOL_GROUP, POOL_GROUP)),
                  _const_spec((1, POOL_WIDTH))],
        out_specs=[slab_spec, slab_spec,
                   pl.BlockSpec((1, POOL_HIST, POOL_WIDTH), lambda i: (i // tiles_per_seq, 0, 0))],
        out_shape=[slabs, slabs,
                   jax.ShapeDtypeStruct((rows // rows_per_seq, POOL_HIST, POOL_WIDTH), F32)],
        scratch_shapes=[pltpu.VMEM((tc + POOL_HIST, POOL_WIDTH), F32)],
        compiler_params=_params(("arbitrary",), 48),
        name="in_proj_pool",
    )(x, g, w, w_pool, scale)


def _pool_buf_kernel(u_ref, buf_ref, wp_ref, sc_ref, o_ref, nb_ref, *, L):
    nseq = buf_ref.shape[1]
    per_group = POOL_GROUP // LANE

    def token_rows(t):
        return pl.ds(t, nseq, stride=L)

    def slab(idx, g):
        if idx < POOL_BUF:
            return buf_ref[idx, :, g * POOL_GROUP:(g + 1) * POOL_GROUP]
        return jnp.concatenate([u_ref[g * per_group + jj, token_rows(idx - POOL_BUF), :]
                                for jj in range(per_group)], axis=1)

    for g, w in enumerate(POOL_WINDOWS):
        sl = slice(g * POOL_GROUP, (g + 1) * POOL_GROUP)
        pooled = []
        for t in range(L):
            acc = slab(POOL_BUF + t, g)
            for back in range(1, w):
                acc = acc + slab(POOL_BUF + t - back, g)
            pooled.append((acc / float(w) - slab(POOL_BUF + t, g)).astype(BF16))
        out = _dot(jnp.concatenate(pooled, axis=0), wp_ref[g]) * sc_ref[:, sl]
        for t in range(L):
            for jj in range(per_group):
                o_ref[g * per_group + jj, token_rows(t), :] = out[t * nseq:(t + 1) * nseq,
                                                                  jj * LANE:(jj + 1) * LANE]
    keep = POOL_BUF - L
    for k in range(keep):
        nb_ref[k] = buf_ref[k + L]
    for t in range(L):
        nb_ref[keep + t] = _lane_blocks(u_ref, token_rows(t))


def _pool_mix_buf(u_blocks, buf_tm, w_pool, scale, *, L):
    rows = u_blocks.shape[1]
    slab_spec = pl.BlockSpec((POOL_LANE_BLOCKS, rows, LANE), lambda i: (0, 0, 0))
    buf_spec = pl.BlockSpec(buf_tm.shape, lambda i: (0, 0, 0))
    return pl.pallas_call(
        functools.partial(_pool_buf_kernel, L=L),
        grid=(1,),
        in_specs=[slab_spec, buf_spec, _const_spec((len(POOL_WINDOWS), POOL_GROUP, POOL_GROUP)),
                  _const_spec((1, POOL_WIDTH))],
        out_specs=[slab_spec, buf_spec],
        out_shape=[jax.ShapeDtypeStruct((POOL_LANE_BLOCKS, rows, LANE), F32),
                   jax.ShapeDtypeStruct(buf_tm.shape, F32)],
        compiler_params=_params(("arbitrary",), 48),
        name="pool_mix_buf",
    )(u_blocks, buf_tm, w_pool, scale)


def _cmul(ar, ai, br, bi):
    return ar * br - ai * bi, ar * bi + ai * br


def _cexp(lam_re, lam_im, log_step):
    delta = jnp.exp(log_step)
    mag = jnp.exp(lam_re * delta)
    ang = lam_im * delta
    return mag * jnp.cos(ang), mag * jnp.sin(ang)


def _ssm_prep_kernel(row_ref, col_ref, flat_ref, *out_refs, chunks):
    L = max(chunks)
    lr, li, ls, br, bi = (row_ref[i] for i in range(5))
    ar, ai = _cexp(lr, li, ls)
    den = lr * lr + li * li
    xr = ar - 1.0
    fr = (xr * lr + ai * li) / den
    fi = (ai * lr - xr * li) / den
    zs = [_cmul(fr, fi, br, bi)]
    for _ in range(1, L):
        zs.append(_cmul(ar, ai, *zs[-1]))

    lrc, lic, lsc, cr, ci = (col_ref[i] for i in range(5))
    acr, aci = _cexp(lrc, lic, lsc)
    xs = []
    cur = (cr, ci)
    for _ in range(L):
        cur = _cmul(acr, aci, *cur)
        xs.append(cur)

    def iota(shape, dim):
        return lax.broadcasted_iota(jnp.int32, shape, dim)

    same_tt = (iota((LANE, LANE), 0) >> 4) == (iota((LANE, LANE), 1) >> 4)
    c_stack = jnp.concatenate([cr, -ci], axis=0)
    zero_tile = jnp.zeros((LANE, LANE), BF16)
    lag = []
    for d in range(L):
        z_stack = jnp.concatenate([zs[d][0], zs[d][1]], axis=1)
        tile = jnp.dot(z_stack, c_stack, precision=HI, preferred_element_type=F32)
        lag.append(jnp.where(same_tt, tile, 0.0).astype(BF16))

    same_w = (iota((LANE, STATE_BLOCK), 0) >> 4) == (iota((LANE, STATE_BLOCK), 1) >> 6)
    w_tiles = []
    for zr, zi in zs:
        wr = jnp.where(same_w, jnp.concatenate([zr] * GROUPS_PER_BLOCK, axis=1), 0.0)
        wi = jnp.where(same_w, jnp.concatenate([zi] * GROUPS_PER_BLOCK, axis=1), 0.0)
        w_tiles.append(jnp.concatenate([wr, wi], axis=1).astype(BF16))

    same_v = (iota((STATE_BLOCK, LANE), 0) >> 6) == (iota((STATE_BLOCK, LANE), 1) >> 4)
    v_tiles = []
    for xr_t, xi_t in xs:
        vr = jnp.where(same_v, jnp.concatenate([xr_t] * GROUPS_PER_BLOCK, axis=0), 0.0)
        vi = jnp.where(same_v, jnp.concatenate([xi_t] * GROUPS_PER_BLOCK, axis=0), 0.0)
        v_tiles.append(jnp.concatenate([vr, -vi], axis=0).astype(BF16))

    flr, fli, fls = (flat_ref[i, 0] for i in range(3))
    far, fai = _cexp(flr, fli, fls)
    powers = [(far, fai)]
    for _ in range(L - 1):
        powers.append(_cmul(far, fai, *powers[-1]))

    for n, Lc in enumerate(chunks):
        t_ref, w_ref, v_ref, al_ref = out_refs[4 * n:4 * n + 4]
        for k in range(Lc):
            for t in range(Lc):
                t_ref[0, k * LANE:(k + 1) * LANE, t * LANE:(t + 1) * LANE] = lag[t - k] if t >= k else zero_tile
            w_ref[0, k * LANE:(k + 1) * LANE, :] = w_tiles[Lc - 1 - k]
        for t in range(Lc):
            v_ref[0, :, t * LANE:(t + 1) * LANE] = v_tiles[t]
        al_ref[0] = jnp.concatenate(powers[Lc - 1], axis=1)


def _ssm_prep(row5, col5, flat3, chunks):
    out_specs, out_shape = [], []
    for L in chunks:
        lk = L * LANE
        for shape, dtype in (((lk, lk), BF16), ((lk, 2 * STATE_BLOCK), BF16), ((2 * STATE_BLOCK, lk), BF16),
                             ((1, 2 * STATE_BLOCK), F32)):
            out_specs.append(pl.BlockSpec((1,) + shape, lambda j: (j, 0, 0)))
            out_shape.append(jax.ShapeDtypeStruct((N_LANE_BLOCKS,) + shape, dtype))
    outs = pl.pallas_call(
        functools.partial(_ssm_prep_kernel, chunks=tuple(chunks)),
        grid=(N_LANE_BLOCKS,),
        in_specs=[pl.BlockSpec((5, LANE, SSM_STATE), lambda j: (0, j, 0)),
                  pl.BlockSpec((5, SSM_STATE, LANE), lambda j: (0, 0, j)),
                  pl.BlockSpec((3, 1, 1, STATE_BLOCK), lambda j: (0, j, 0, 0))],
        out_specs=out_specs,
        out_shape=out_shape,
        compiler_params=_params(("arbitrary",), 40),
        name="ssm_prep",
    )(row5, col5, flat3)
    return {L: tuple(outs[4 * n:4 * n + 4]) for n, L in enumerate(chunks)}


def _ssm_kernel(u_ref, t_ref, w_ref, v_ref, al_ref, d_ref, h0_ref, g_ref, st_ref, *scratch, B, Rn, L, seq):
    M = B * Rn
    sb = STATE_BLOCK
    io_scr = scratch[0]
    if seq:
        c_scr = scratch[1]
        assert 2 * B == 8 and Rn % 2 == 0, "two chunk rows of B sequences fill one sublane tile"

        @pl.when(pl.program_id(1) == 0)
        def _():
            c_scr[:B, :] = h0_ref[0]
            c_scr[B:, :] = jnp.zeros((B, 2 * sb), F32)

    for b in range(B):
        for k in range(L):
            io_scr[k, pl.ds(b, Rn, stride=B), :] = u_ref[0, b, pl.ds(k, Rn, stride=L), :]
    u = jnp.concatenate([io_scr[k] for k in range(L)], axis=1)
    ub = u.astype(BF16)
    e = _dot(ub, w_ref[0])
    ar = al_ref[0][:, :sb]
    ai = al_ref[0][:, sb:]

    def advance(s, eb):
        sr, si = s[:, :sb], s[:, sb:]
        return jnp.concatenate([ar * sr - ai * si + eb[:, :sb], ar * si + ai * sr + eb[:, sb:]], axis=1)

    if seq:
        local = _dot(ub, t_ref[0]) + d_ref[0] * u
        top = lax.broadcasted_iota(jnp.int32, (2 * B, 1), 0) < B
        s = c_scr[...]
        starts = []
        for i in range(Rn // 2):
            e2 = e[i * 2 * B:(i + 1) * 2 * B]
            mid = pltpu.roll(advance(s, e2), B, 0)
            starts.append(jnp.where(top, s, mid))
            s = pltpu.roll(advance(mid, e2), B, 0)
        c_scr[...] = s
        st_ref[0] = s[:B, :]
        s_start = jnp.concatenate(starts, axis=0)
    else:
        local = _dot(ub, t_ref[0]) + d_ref[0] * u
        s_start = h0_ref[0]
        st_ref[0] = advance(s_start, e)
    g = jax.nn.gelu(local + _dot(s_start.astype(BF16), v_ref[0]))
    for k in range(L):
        io_scr[k] = g[:, k * LANE:(k + 1) * LANE]
    for b in range(B):
        for k in range(L):
            g_ref[0, b, pl.ds(k, Rn, stride=L), :] = io_scr[k, pl.ds(b, Rn, stride=B), :]


def _ssm_mix(u_view, ops, d_tiles, h0, *, L, Rn, seq):
    t_op, w_op, v_op, a_l = ops
    nj = N_LANE_BLOCKS
    _, B, trows, _ = u_view.shape
    lk = L * LANE
    nseq = h0.shape[1]
    tiles = trows // (Rn * L)
    M = B * Rn
    sb2 = 2 * STATE_BLOCK
    first = u_view.shape[0] - nj
    u_spec = pl.BlockSpec((1, B, Rn * L, LANE), lambda j, i: (j + first, 0, i, 0))
    g_spec = pl.BlockSpec((1, B, Rn * L, LANE), lambda j, i: (j, 0, i, 0))
    scratch = [pltpu.VMEM((L, M, LANE), F32)]
    if seq:
        scratch += [pltpu.VMEM((2 * B, sb2), F32)]
    return pl.pallas_call(
        functools.partial(_ssm_kernel, B=B, Rn=Rn, L=L, seq=seq),
        grid=(nj, tiles),
        in_specs=[u_spec,
                  pl.BlockSpec((1, lk, lk), lambda j, i: (j, 0, 0)),
                  pl.BlockSpec((1, lk, sb2), lambda j, i: (j, 0, 0)),
                  pl.BlockSpec((1, sb2, lk), lambda j, i: (j, 0, 0)),
                  pl.BlockSpec((1, 1, sb2), lambda j, i: (j, 0, 0)),
                  pl.BlockSpec((1, 1, lk), lambda j, i: (j, 0, 0)),
                  pl.BlockSpec((1, nseq, sb2), lambda j, i: (j, 0, 0))],
        out_specs=[g_spec, pl.BlockSpec((1, nseq, sb2), lambda j, i: (j, 0, 0))],
        out_shape=[jax.ShapeDtypeStruct((nj,) + u_view.shape[1:], F32),
                   jax.ShapeDtypeStruct((nj, nseq, sb2), F32)],
        scratch_shapes=scratch,
        compiler_params=_params(("arbitrary", "arbitrary"), 48),
        name="ssm_mix_seq" if seq else "ssm_mix_rows",
    )(u_view, t_op, w_op, v_op, a_l, d_tiles, h0)


def _mix_out_kernel(g_ref, po_ref, x_ref, wglu_ref, bglu_ref, wout_ref, gc_ref, wq_ref, *rest, attend):
    if attend:
        k_ref, v_ref, x1_ref, q_ref = rest
    else:
        x1_ref, q_ref = rest
    g = _lane_blocks(g_ref)
    gate = jax.nn.sigmoid(_dot(g.astype(BF16), wglu_ref[...]) + bglu_ref[...])
    ssm_out = (g * gate).astype(BF16)
    mix = (_dot(_lane_blocks(po_ref).astype(BF16), wout_ref[:POOL_WIDTH, :])
           + _dot(ssm_out, wout_ref[POOL_WIDTH:, :]))
    x1 = x_ref[...] + mix
    x1_ref[...] = x1
    q = _dot(_rms(x1, gc_ref[...]).astype(BF16), wq_ref[...]).astype(BF16)
    if attend:
        heads = [slice(h * XHEAD_DIM, (h + 1) * XHEAD_DIM) for h in range(N_XHEADS)]
        scores = [lax.dot_general(q[:, sl], k_ref[:, sl], _NT, preferred_element_type=F32) for sl in heads]
        probs = [_softmax_rows(s * (XHEAD_DIM ** -0.5)).astype(BF16) for s in scores]
        for sl, p in zip(heads, probs):
            q_ref[:, sl] = _dot(p, v_ref[:, sl]).astype(BF16)
    else:
        q_ref[...] = q


def _mix_out(g_rows, pool_out, x, w_glu, b_glu, w_out, g_cross, w_q, tm, mem_kv=None, rows_per_seq=None):
    rows = x.shape[0]
    attend = mem_kv is not None
    extra_specs, extra = [], []
    if attend:
        tiles = rows_per_seq // tm
        kv_spec = pl.BlockSpec((N_MEM, D_MODEL), lambda i: (i // tiles, 0))
        extra_specs, extra = [kv_spec, kv_spec], list(mem_kv)
    return pl.pallas_call(
        functools.partial(_mix_out_kernel, attend=attend),
        grid=(rows // tm,),
        in_specs=[pl.BlockSpec((N_LANE_BLOCKS, tm, LANE), lambda i: (0, i, 0)),
                  pl.BlockSpec((POOL_LANE_BLOCKS, tm, LANE), lambda i: (0, i, 0)),
                  pl.BlockSpec((tm, D_MODEL), lambda i: (i, 0)),
                  _const_spec((SSM_WIDTH, SSM_WIDTH)), _const_spec((1, SSM_WIDTH)),
                  _const_spec((D_MODEL, D_MODEL)), _const_spec((1, D_MODEL)),
                  _const_spec((D_MODEL, D_MODEL))] + extra_specs,
        out_specs=[pl.BlockSpec((tm, D_MODEL), lambda i: (i, 0)),
                   pl.BlockSpec((tm, D_MODEL), lambda i: (i, 0))],
        out_shape=[jax.ShapeDtypeStruct((rows, D_MODEL), F32),
                   jax.ShapeDtypeStruct((rows, D_MODEL), BF16)],
        compiler_params=_params(("arbitrary",), 56),
        name="mix_out_attn" if attend else "mix_out",
    )(g_rows, pool_out, x, w_glu, b_glu, w_out, g_cross, w_q, *extra)


HEAD_LANE_BLOCKS = XHEAD_DIM // LANE
CACHE_ROW_PITCH = N_XHEADS * HEAD_LANE_BLOCKS


def _cache_rows_view(cache):
    S = cache.shape[0]
    c5 = cache.reshape(S, N_MEM, N_XHEADS, HEAD_LANE_BLOCKS, LANE)
    return c5.transpose(0, 1, 3, 2, 4).reshape(S, N_MEM * CACHE_ROW_PITCH, LANE)


def _cache_rows_unview(rows, nseq):
    r5 = rows.reshape(nseq, N_MEM, HEAD_LANE_BLOCKS, N_XHEADS, LANE)
    return r5.transpose(0, 1, 3, 2, 4).reshape(nseq, N_MEM, N_XHEADS, XHEAD_DIM)


def _mem_kv_kernel(m_ref, g_ref, w_ref, rows_ref, kvb_ref):
    tm = m_ref.shape[0]
    kv = _dot(_rms(m_ref[...], g_ref[...]).astype(BF16), w_ref[...].astype(BF16))
    kvb_ref[...] = kv.astype(BF16)
    for c in range(HEAD_LANE_BLOCKS):
        for h in range(N_XHEADS):
            lo = h * XHEAD_DIM + c * LANE
            rows_ref[pl.ds(c * N_XHEADS + h, tm, stride=CACHE_ROW_PITCH), :] = kv[:, lo:lo + LANE]


def _mem_kv(mem, g_mem, w, tm):
    rows = mem.shape[0]
    return pl.pallas_call(
        _mem_kv_kernel,
        grid=(rows // tm,),
        in_specs=[pl.BlockSpec((tm, D_MODEL), lambda i: (i, 0)),
                  _const_spec((1, D_MODEL)),
                  _const_spec((D_MODEL, D_MODEL))],
        out_specs=[pl.BlockSpec((tm * CACHE_ROW_PITCH, LANE), lambda i: (i, 0)),
                   pl.BlockSpec((tm, D_MODEL), lambda i: (i, 0))],
        out_shape=[jax.ShapeDtypeStruct((rows * CACHE_ROW_PITCH, LANE), F32),
                   jax.ShapeDtypeStruct((rows, D_MODEL), BF16)],
        compiler_params=_params(("arbitrary",), 48),
        name="mem_kv",
    )(mem, g_mem, w)


_NT = (((1,), (1,)), ((), ()))


def _softmax_rows(s):
    e = jnp.exp(s - jnp.max(s, axis=-1, keepdims=True))
    return e / jnp.sum(e, axis=-1, keepdims=True)


def _attn_cache_kernel(q_ref, k_ref, v_ref, o_ref, *, bb):
    nc = HEAD_LANE_BLOCKS
    mc = N_MEM * nc
    lane_c = lax.broadcasted_iota(jnp.int32, (8, mc), 1) & (nc - 1)
    owns = []
    for b in range(bb):
        q8 = q_ref[b]
        for h in range(N_XHEADS):
            xk = k_ref[b, pl.ds(h, mc, stride=N_XHEADS), :].astype(BF16)
            qc = jnp.concatenate([q8[:, h * XHEAD_DIM + c * LANE:h * XHEAD_DIM + (c + 1) * LANE]
                                  for c in range(nc)], axis=0).astype(BF16)
            part = lax.dot_general(qc, xk, _NT, preferred_element_type=F32)
            own = part[8 * (nc - 1):]
            for c in range(nc - 2, -1, -1):
                own = jnp.where(lane_c == c, part[8 * c:8 * (c + 1)], own)
            owns.append(own)
    own_all = jnp.concatenate(owns, axis=0)
    lane_all = lax.broadcasted_iota(jnp.int32, own_all.shape, 1) & (nc - 1)
    pair = own_all + pltpu.roll(own_all, 1, 1)
    full = pair + pltpu.roll(pair, 2, 1)
    s = jnp.where(lane_all == nc - 1, full * (XHEAD_DIM ** -0.5), -1e30)
    e = jnp.exp(s - jnp.max(s, axis=1, keepdims=True))
    p = e / jnp.sum(e, axis=1, keepdims=True)
    p2 = p + pltpu.roll(p, mc - 1, 1)
    p4 = p2 + pltpu.roll(p2, mc - 2, 1)
    for b in range(bb):
        xv = jnp.concatenate([v_ref[b, pl.ds(h, mc, stride=N_XHEADS), :] for h in range(N_XHEADS)],
                             axis=1).astype(BF16)
        pm_rows = []
        for h in range(N_XHEADS):
            r0 = 8 * (b * N_XHEADS + h)
            pm_rows += [jnp.where(lane_c == c, p4[r0:r0 + 8], 0.0) for c in range(nc)]
        o_all = _dot(jnp.concatenate(pm_rows, axis=0).astype(BF16), xv)
        for h in range(N_XHEADS):
            for c in range(nc):
                g = h * nc + c
                o_ref[b, :, h * XHEAD_DIM + c * LANE:h * XHEAD_DIM + (c + 1) * LANE] = (
                    o_all[8 * g:8 * (g + 1), h * LANE:(h + 1) * LANE])


def _attn_cache(q8, k_rows, v_rows, bb):
    nseq = q8.shape[0]
    rows = N_MEM * CACHE_ROW_PITCH
    return pl.pallas_call(
        functools.partial(_attn_cache_kernel, bb=bb),
        grid=(nseq // bb,),
        in_specs=[pl.BlockSpec((bb, 8, D_MODEL), lambda i: (i, 0, 0)),
                  pl.BlockSpec((bb, rows, LANE), lambda i: (i, 0, 0)),
                  pl.BlockSpec((bb, rows, LANE), lambda i: (i, 0, 0))],
        out_specs=pl.BlockSpec((bb, 8, D_MODEL), lambda i: (i, 0, 0)),
        out_shape=jax.ShapeDtypeStruct((nseq, 8, D_MODEL), F32),
        compiler_params=_params(("arbitrary",), 48),
        name="attn_cache",
    )(q8, k_rows, v_rows)


def _oproj_kernel(o_ref, x1_ref, wo_ref, gf_ref, x2_ref, h_ref, *wb_ref):
    w = wo_ref[...].astype(BF16)
    for ref in wb_ref:
        ref[...] = w
    x2 = x1_ref[...] + _dot(o_ref[...], w)
    x2_ref[...] = x2
    h_ref[...] = _rms(x2, gf_ref[...]).astype(BF16)


def _oproj(o, x1, w_o, g_ffn, tm, emit_bf16=False):
    rows = x1.shape[0]
    row_spec = pl.BlockSpec((tm, D_MODEL), lambda i: (i, 0))
    out_specs = [row_spec, row_spec]
    out_shape = [jax.ShapeDtypeStruct((rows, D_MODEL), F32), jax.ShapeDtypeStruct((rows, D_MODEL), BF16)]
    if emit_bf16:
        out_specs.append(_const_spec((D_MODEL, D_MODEL)))
        out_shape.append(jax.ShapeDtypeStruct((D_MODEL, D_MODEL), BF16))
    return pl.pallas_call(
        _oproj_kernel,
        grid=(rows // tm,),
        in_specs=[row_spec, row_spec, _const_spec((D_MODEL, D_MODEL)), _const_spec((1, D_MODEL))],
        out_specs=out_specs,
        out_shape=out_shape,
        compiler_params=_params(("arbitrary",), 56),
        name="oproj",
    )(o, x1, w_o, g_ffn)


def _ffn_kernel(x_hbm, h_ref, wg_ref, wu_ref, wd_ref, gl_ref, y_ref, *rest):
    *bf16_out, x_buf, x_sem = rest
    f = pl.program_id(1)
    last = pl.num_programs(1) - 1
    tm = h_ref.shape[0]

    def x_copy():
        rows = pl.ds(pl.multiple_of(pl.program_id(0) * tm, tm), tm)
        return pltpu.make_async_copy(x_hbm.at[rows, :], x_buf, x_sem)

    def down():
        wg, wu, wd = (w[...].astype(BF16) for w in (wg_ref, wu_ref, wd_ref))
        for ref, w in zip(bf16_out, (wg, wu, wd)):
            ref[...] = w
        h = h_ref[...]
        z = (jax.nn.silu(_dot(h, wg)) * _dot(h, wu)).astype(BF16)
        return _dot(z, wd)

    @pl.when(f == 0)
    def _():
        x_copy().start()
        y_ref[...] = down()

    @pl.when((f > 0) & (f < last))
    def _():
        y_ref[...] += down()

    @pl.when(f == last)
    def _():
        x_copy().wait()
        y_ref[...] = _rms(x_buf[...] + (y_ref[...] + down()), gl_ref[...])


def _ffn(x2, h, w_gate, w_up, w_down, g_final, tm, tf, emit_bf16=False):
    rows = x2.shape[0]
    nf = D_FF // tf
    assert nf >= 2, "the residual copy is started on the first d_ff tile and waited on the last"
    row_spec = pl.BlockSpec((tm, D_MODEL), lambda i, f: (i, 0))
    out_specs = [row_spec]
    out_shape = [jax.ShapeDtypeStruct((rows, D_MODEL), F32)]
    if emit_bf16:
        once = lambda i, f: jnp.where(i == 0, f, nf)
        out_specs += [pl.BlockSpec((D_MODEL, tf), lambda i, f: (0, once(i, f))),
                      pl.BlockSpec((D_MODEL, tf), lambda i, f: (0, once(i, f))),
                      pl.BlockSpec((tf, D_MODEL), lambda i, f: (once(i, f), 0))]
        out_shape += [jax.ShapeDtypeStruct((D_MODEL, D_FF + tf), BF16),
                      jax.ShapeDtypeStruct((D_MODEL, D_FF + tf), BF16),
                      jax.ShapeDtypeStruct((D_FF + tf, D_MODEL), BF16)]
    return pl.pallas_call(
        _ffn_kernel,
        grid=(rows // tm, nf),
        in_specs=[pl.BlockSpec(memory_space=pl.ANY), row_spec,
                  pl.BlockSpec((D_MODEL, tf), lambda i, f: (0, f)),
                  pl.BlockSpec((D_MODEL, tf), lambda i, f: (0, f)),
                  pl.BlockSpec((tf, D_MODEL), lambda i, f: (f, 0)),
                  _const_spec((1, D_MODEL))],
        out_specs=out_specs,
        out_shape=out_shape,
        scratch_shapes=[pltpu.VMEM((tm, D_MODEL), F32), pltpu.SemaphoreType.DMA(())],
        compiler_params=_params(("arbitrary", "arbitrary"), 60 if emit_bf16 else 56),
        name="ffn",
    )(x2, h, w_gate, w_up, w_down, g_final)


def _ssm_param_layouts(lam_re, lam_im, log_step, b_re, b_im, c_re, c_im):
    G, P = lam_re.shape
    ls = jnp.broadcast_to(log_step[:, None], (G, P))
    rep = lambda a: jnp.repeat(a, SSM_GROUP, axis=0)
    b_rows = lambda b: jnp.swapaxes(b, 1, 2).reshape(G * SSM_GROUP, P)
    row5 = jnp.stack([rep(lam_re), rep(lam_im), rep(ls), b_rows(b_re), b_rows(b_im)])
    col5 = jnp.stack([rep(lam_re).T, rep(lam_im).T, rep(ls).T,
                      c_re.reshape(G * SSM_GROUP, P).T, c_im.reshape(G * SSM_GROUP, P).T])
    flat = lambda a: a.reshape(N_LANE_BLOCKS, 1, STATE_BLOCK)
    flat3 = jnp.stack([flat(lam_re), flat(lam_im), flat(ls)])
    return row5, col5, flat3


def _states_to_blocks(h_re, h_im):
    S = h_re.shape[0]
    blk = lambda h: h.reshape(S, N_LANE_BLOCKS, STATE_BLOCK).transpose(1, 0, 2)
    return jnp.concatenate([blk(h_re), blk(h_im)], axis=-1)


def _blocks_to_states(st):
    S = st.shape[1]
    unblk = lambda a: a.transpose(1, 0, 2).reshape(1, S, N_SSM_GROUPS, SSM_STATE)
    return unblk(st[:, :, :STATE_BLOCK]), unblk(st[:, :, STATE_BLOCK:])


def kernel(x_prompt, x_sample, mem_prompt, state_pool_buf, state_ssm_re, state_ssm_im, cache_mem_k, cache_mem_v, g_mix, w_in, w_pool, pool_scale, ssm_lam_re, ssm_lam_im, ssm_log_step, ssm_b_re, ssm_b_im, ssm_c_re, ssm_c_im, ssm_d, w_glu, b_glu, w_out, g_cross, g_mem, w_q, w_k, w_v, w_o, g_ffn, w_gate, w_up, w_down, g_final):
    assert g_mix.shape[0] == 1, "single-layer step"
    B, T, _ = x_prompt.shape
    S, Ts, _ = x_sample.shape
    bf = lambda w: w.astype(BF16)
    vec = lambda v: v.reshape(1, -1)

    w_pool_b, w_glu_b, w_out_b, w_q_b = bf(w_pool[0]), bf(w_glu[0]), bf(w_out[0]), bf(w_q[0])

    row5, col5, flat3 = _ssm_param_layouts(ssm_lam_re[0], ssm_lam_im[0], ssm_log_step[0],
                                           ssm_b_re[0], ssm_b_im[0], ssm_c_re[0], ssm_c_im[0])
    d_blocks = ssm_d[0].reshape(N_LANE_BLOCKS, 1, LANE)

    gains = dict(g_mix=vec(g_mix[0]), scale=vec(pool_scale[0]), b_glu=vec(b_glu[0]), g_cross=vec(g_cross[0]),
                 g_ffn=vec(g_ffn[0]), g_final=vec(g_final))

    chunk_ops = _ssm_prep(row5, col5, flat3, sorted({PROMPT_CHUNK, Ts}, reverse=True))

    def ssm(u_view, L, h0_blocks, **tiles):
        g_act, st = _ssm_mix(u_view, chunk_ops[L], jnp.tile(d_blocks, (1, 1, L)), h0_blocks, L=L, **tiles)
        rows = u_view.shape[1] * u_view.shape[2]
        return g_act.reshape(N_LANE_BLOCKS, rows, LANE), _blocks_to_states(st)

    def finish(o, x1, w_o_any, ffn_weights, ffn_tiles, emit_bf16=False):
        x2, hf, *w_o_bf16 = _oproj(o, x1, w_o_any, gains["g_ffn"], ROW_TILE, emit_bf16=emit_bf16)
        y, *ffn_bf16 = _ffn(x2, hf, *ffn_weights, gains["g_final"], *ffn_tiles, emit_bf16=emit_bf16)
        return y, w_o_bf16, ffn_bf16

    mem = mem_prompt.reshape(B * N_MEM, D_MODEL)
    k_rows, k_b = _mem_kv(mem, vec(g_mem[0]), w_k[0], ROW_TILE)
    v_rows, v_b = _mem_kv(mem, vec(g_mem[0]), w_v[0], ROW_TILE)
    xp = x_prompt.reshape(B * T, D_MODEL)
    us_p, po_p, tail_p = _in_proj_pool(xp, gains["g_mix"], w_in[0], w_pool_b, gains["scale"],
                                       tc=ROW_TILE, rows_per_seq=T)
    h0_p = jnp.zeros((N_LANE_BLOCKS, B, 2 * STATE_BLOCK), F32)
    g_p, (re_p, im_p) = ssm(us_p.reshape(N_LANE_BLOCKS, B, T, LANE), PROMPT_CHUNK, h0_p,
                            Rn=PROMPT_CHUNK_ROWS, seq=True)
    x1_p, o_p = _mix_out(g_p, po_p, xp, w_glu_b, gains["b_glu"], w_out_b, gains["g_cross"], w_q_b, ROW_TILE,
                         mem_kv=(k_b, v_b), rows_per_seq=T)
    yp, (w_o_b,), ffn_bf16 = finish(o_p, x1_p, w_o[0], (w_gate[0], w_up[0], w_down[0]), PROMPT_FFN_TILES,
                                    emit_bf16=True)
    pb_p = tail_p[:, POOL_HIST - POOL_BUF:]

    assert PAST_LEN >= POOL_BUF
    xs = x_sample.reshape(S * Ts, D_MODEL)
    u_s = _in_proj(xs, gains["g_mix"], w_in[0], ROW_TILE)
    po_s, buf_tm = _pool_mix_buf(u_s, state_pool_buf[0].transpose(1, 0, 2), w_pool_b, gains["scale"], L=Ts)
    pb_s = buf_tm.transpose(1, 0, 2)
    h0_s = _states_to_blocks(state_ssm_re[0], state_ssm_im[0])
    g_s, (re_s, im_s) = ssm(u_s.reshape(MIX_LANE_BLOCKS, 1, S * Ts, LANE), Ts, h0_s, Rn=S, seq=False)
    x1_s, q_s = _mix_out(g_s, po_s, xs, w_glu_b, gains["b_glu"], w_out_b, gains["g_cross"], w_q_b, ROW_TILE)
    q8 = jnp.pad(q_s.reshape(S, Ts, D_MODEL).astype(F32), ((0, 0), (0, 8 - Ts), (0, 0)))
    o8 = _attn_cache(q8, _cache_rows_view(cache_mem_k[0]), _cache_rows_view(cache_mem_v[0]), CACHE_SEQS_PER_STEP)
    ys, _, _ = finish(o8[:, :Ts].reshape(S * Ts, D_MODEL).astype(BF16), x1_s, w_o_b, ffn_bf16, SAMPLE_FFN_TILES)

    mk = _cache_rows_unview(k_rows, B)[None]
    mv = _cache_rows_unview(v_rows, B)[None]
    return (yp.reshape(B, T, D_MODEL), ys.reshape(S, Ts, D_MODEL), pb_p[None], re_p, im_p, mk, mv,
            pb_s[None], re_s, im_s)
```

```python
import functools

import jax
import jax.numpy as jnp
from jax import lax
from jax.experimental import pallas as pl
from jax.experimental.pallas import tpu as pltpu

F32 = jnp.float32
BF16 = jnp.bfloat16

D_MODEL = 2048
POOL_WIDTH = 1024
SSM_WIDTH = 1024
POOL_WINDOWS = (2, 4, 8, 16)
POOL_GROUP = POOL_WIDTH // len(POOL_WINDOWS)
POOL_BUF = max(POOL_WINDOWS) - 1
SSM_GROUP = 16
N_SSM_GROUPS = SSM_WIDTH // SSM_GROUP
SSM_STATE = 64
N_MEM = 256
N_XHEADS = 4
XHEAD_DIM = D_MODEL // N_XHEADS
D_FF = 5632
EPS = 1e-6
PAST_LEN = 16384

LANE = 128
BF16_SUBLANES = 16
N_LANE_BLOCKS = SSM_WIDTH // LANE
GROUPS_PER_BLOCK = LANE // SSM_GROUP
STATE_BLOCK = GROUPS_PER_BLOCK * SSM_STATE
MIB = 1024 * 1024
HI = lax.Precision.HIGHEST

ROW_TILE = 512
ROW_GROUPS = 4
PROMPT_CHUNK = 8
PROMPT_CHUNK_ROWS = 128
PROMPT_FFN_TILES = (1024, 256)
SAMPLE_FFN_TILES = (512, 512)
CACHE_SEQS_PER_STEP = 4


def _params(semantics, vmem_mib, fuse_inputs=None):
    return pltpu.CompilerParams(dimension_semantics=semantics, vmem_limit_bytes=vmem_mib * MIB,
                                allow_input_fusion=fuse_inputs)


def _const_spec(shape):
    return pl.BlockSpec(shape, lambda *_: (0,) * len(shape), pipeline_mode=pl.Buffered(1))


def _rms(x, g):
    r = lax.rsqrt(jnp.mean(x * x, axis=-1, keepdims=True) + EPS)
    return x * r * g


def _dot(a, b):
    return jnp.dot(a, b, preferred_element_type=F32)


MIX_LANE_BLOCKS = D_MODEL // LANE


def _in_proj_kernel(x_ref, g_ref, w_ref, u_ref):
    h = _rms(x_ref[...], g_ref[...]).astype(BF16)
    u = _dot(h, w_ref[...].astype(BF16))
    for j in range(MIX_LANE_BLOCKS):
        u_ref[j] = u[:, j * LANE:(j + 1) * LANE]


def _in_proj(x, g, w, tm):
    rows = x.shape[0]
    return pl.pallas_call(
        _in_proj_kernel,
        grid=(rows // tm,),
        in_specs=[pl.BlockSpec((tm, D_MODEL), lambda i: (i, 0)),
                  _const_spec((1, D_MODEL)),
                  _const_spec((D_MODEL, D_MODEL))],
        out_specs=pl.BlockSpec((MIX_LANE_BLOCKS, tm, LANE), lambda i: (0, i, 0)),
        out_shape=jax.ShapeDtypeStruct((MIX_LANE_BLOCKS, rows, LANE), F32),
        compiler_params=_params(("arbitrary",), 52),
        name="in_proj",
    )(x, g, w)


def _lane_blocks(ref, rows=slice(None)):
    return jnp.concatenate([ref[j, rows, :] for j in range(ref.shape[0])], axis=1)


POOL_HIST = POOL_BUF + 1


def _in_proj_pool_kernel(x_ref, g_ref, w_ref, wp_ref, sc_ref, us_ref, o_ref, tail_ref, ext_ref, *, tc,
                         tiles_per_seq):
    it = lax.rem(pl.program_id(0), tiles_per_seq)

    @pl.when(it == 0)
    def _():
        ext_ref[:POOL_HIST, :] = jnp.zeros((POOL_HIST, POOL_WIDTH), F32)

    nq = ROW_GROUPS
    rq = tc // nq
    w = w_ref[...].astype(BF16)
    for c in range(nq):
        rows = slice(c * rq, (c + 1) * rq)
        u = _dot(_rms(x_ref[rows, :], g_ref[...]).astype(BF16), w)
        for j in range(N_LANE_BLOCKS):
            us_ref[j, rows, :] = u[:, POOL_WIDTH + j * LANE:POOL_WIDTH + (j + 1) * LANE]
        ext_ref[POOL_HIST + c * rq:POOL_HIST + (c + 1) * rq, :] = u[:, :POOL_WIDTH]
    for c in range(nq):
        rows = slice(c * rq, (c + 1) * rq)
        pos = it * tc + c * rq + lax.broadcasted_iota(jnp.int32, (rq, 1), 0)
        for g, w in enumerate(POOL_WINDOWS):
            sl = slice(g * POOL_GROUP, (g + 1) * POOL_GROUP)
            z = ext_ref[c * rq:(c + 1) * rq + POOL_HIST, sl]
            s, span = z, 1
            while span < w:
                s = s + pltpu.roll(s, span, 0)
                span *= 2
            cnt = jnp.minimum(pos + 1, w).astype(F32)
            pooled = (s[POOL_HIST:] / cnt - z[POOL_HIST:]).astype(BF16)
            out = _dot(pooled, wp_ref[g]) * sc_ref[:, sl]
            for jj in range(POOL_GROUP // LANE):
                o_ref[g * (POOL_GROUP // LANE) + jj, rows, :] = out[:, jj * LANE:(jj + 1) * LANE]
    last = ext_ref[tc:tc + POOL_HIST, :]
    tail_ref[0] = last
    ext_ref[:POOL_HIST, :] = last


POOL_LANE_BLOCKS = POOL_WIDTH // LANE


def _in_proj_pool(x, g, w, w_pool, scale, *, tc, rows_per_seq):
    rows = x.shape[0]
    tiles_per_seq = rows_per_seq // tc
    slab_spec = pl.BlockSpec((N_LANE_BLOCKS, tc, LANE), lambda i: (0, i, 0))
    slabs = jax.ShapeDtypeStruct((N_LANE_BLOCKS, rows, LANE), F32)
    return pl.pallas_call(
        functools.partial(_in_proj_pool_kernel, tc=tc, tiles_per_seq=tiles_per_seq),
        grid=(rows // tc,),
        in_specs=[pl.BlockSpec((tc, D_MODEL), lambda i: (i, 0)),
                  _const_spec((1, D_MODEL)),
                  _const_spec((D_MODEL, D_MODEL)),
                  _const_spec((len(POOL_WINDOWS), POOL_GROUP, POOL_GROUP)),
                  _const_spec((1, POOL_WIDTH))],
        out_specs=[slab_spec, slab_spec,
                   pl.BlockSpec((1, POOL_HIST, POOL_WIDTH), lambda i: (i // tiles_per_seq, 0, 0))],
        out_shape=[slabs, slabs,
                   jax.ShapeDtypeStruct((rows // rows_per_seq, POOL_HIST, POOL_WIDTH), F32)],
        scratch_shapes=[pltpu.VMEM((tc + POOL_HIST, POOL_WIDTH), F32)],
        compiler_params=_params(("arbitrary",), 48),
        name="in_proj_pool",
    )(x, g, w, w_pool, scale)


def _pool_buf_kernel(u_ref, buf_ref, wp_ref, sc_ref, o_ref, nb_ref, *, L):
    nseq = buf_ref.shape[1]
    per_group = POOL_GROUP // LANE

    def token_rows(t):
        return pl.ds(t, nseq, stride=L)

    def slab(idx, g):
        if idx < POOL_BUF:
            return buf_ref[idx, :, g * POOL_GROUP:(g + 1) * POOL_GROUP]
        return jnp.concatenate([u_ref[g * per_group + jj, token_rows(idx - POOL_BUF), :]
                                for jj in range(per_group)], axis=1)

    for g, w in enumerate(POOL_WINDOWS):
        sl = slice(g * POOL_GROUP, (g + 1) * POOL_GROUP)
        pooled = []
        for t in range(L):
            acc = slab(POOL_BUF + t, g)
            for back in range(1, w):
                acc = acc + slab(POOL_BUF + t - back, g)
            pooled.append((acc / float(w) - slab(POOL_BUF + t, g)).astype(BF16))
        out = _dot(jnp.concatenate(pooled, axis=0), wp_ref[g]) * sc_ref[:, sl]
        for t in range(L):
            for jj in range(per_group):
                o_ref[g * per_group + jj, token_rows(t), :] = out[t * nseq:(t + 1) * nseq,
                                                                  jj * LANE:(jj + 1) * LANE]
    keep = POOL_BUF - L
    for k in range(keep):
        nb_ref[k] = buf_ref[k + L]
    for t in range(L):
        nb_ref[keep + t] = _lane_blocks(u_ref, token_rows(t))


def _pool_mix_buf(u_blocks, buf_tm, w_pool, scale, *, L):
    rows = u_blocks.shape[1]
    slab_spec = pl.BlockSpec((POOL_LANE_BLOCKS, rows, LANE), lambda i: (0, 0, 0))
    buf_spec = pl.BlockSpec(buf_tm.shape, lambda i: (0, 0, 0))
    return pl.pallas_call(
        functools.partial(_pool_buf_kernel, L=L),
        grid=(1,),
        in_specs=[slab_spec, buf_spec, _const_spec((len(POOL_WINDOWS), POOL_GROUP, POOL_GROUP)),
                  _const_spec((1, POOL_WIDTH))],
        out_specs=[slab_spec, buf_spec],
        out_shape=[jax.ShapeDtypeStruct((POOL_LANE_BLOCKS, rows, LANE), F32),
                   jax.ShapeDtypeStruct(buf_tm.shape, F32)],
        compiler_params=_params(("arbitrary",), 48),
        name="pool_mix_buf",
    )(u_blocks, buf_tm, w_pool, scale)


def _cmul(ar, ai, br, bi):
    return ar * br - ai * bi, ar * bi + ai * br


def _cexp(lam_re, lam_im, log_step):
    delta = jnp.exp(log_step)
    mag = jnp.exp(lam_re * delta)
    ang = lam_im * delta
    return mag * jnp.cos(ang), mag * jnp.sin(ang)


def _ssm_prep_kernel(row_ref, col_ref, flat_ref, *out_refs, chunks):
    L = max(chunks)
    lr, li, ls, br, bi = (row_ref[i] for i in range(5))
    ar, ai = _cexp(lr, li, ls)
    den = lr * lr + li * li
    xr = ar - 1.0
    fr = (xr * lr + ai * li) / den
    fi = (ai * lr - xr * li) / den
    zs = [_cmul(fr, fi, br, bi)]
    for _ in range(1, L):
        zs.append(_cmul(ar, ai, *zs[-1]))

    lrc, lic, lsc, cr, ci = (col_ref[i] for i in range(5))
    acr, aci = _cexp(lrc, lic, lsc)
    xs = []
    cur = (cr, ci)
    for _ in range(L):
        cur = _cmul(acr, aci, *cur)
        xs.append(cur)

    def iota(shape, dim):
        return lax.broadcasted_iota(jnp.int32, shape, dim)

    same_tt = (iota((LANE, LANE), 0) >> 4) == (iota((LANE, LANE), 1) >> 4)
    c_stack = jnp.concatenate([cr, -ci], axis=0)
    zero_tile = jnp.zeros((LANE, LANE), BF16)
    lag = []
    for d in range(L):
        z_stack = jnp.concatenate([zs[d][0], zs[d][1]], axis=1)
        tile = jnp.dot(z_stack, c_stack, precision=HI, preferred_element_type=F32)
        lag.append(jnp.where(same_tt, tile, 0.0).astype(BF16))

    same_w = (iota((LANE, STATE_BLOCK), 0) >> 4) == (iota((LANE, STATE_BLOCK), 1) >> 6)
    w_tiles = []
    for zr, zi in zs:
        wr = jnp.where(same_w, jnp.concatenate([zr] * GROUPS_PER_BLOCK, axis=1), 0.0)
        wi = jnp.where(same_w, jnp.concatenate([zi] * GROUPS_PER_BLOCK, axis=1), 0.0)
        w_tiles.append(jnp.concatenate([wr, wi], axis=1).astype(BF16))

    same_v = (iota((STATE_BLOCK, LANE), 0) >> 6) == (iota((STATE_BLOCK, LANE), 1) >> 4)
    v_tiles = []
    for xr_t, xi_t in xs:
        vr = jnp.where(same_v, jnp.concatenate([xr_t] * GROUPS_PER_BLOCK, axis=0), 0.0)
        vi = jnp.where(same_v, jnp.concatenate([xi_t] * GROUPS_PER_BLOCK, axis=0), 0.0)
        v_tiles.append(jnp.concatenate([vr, -vi], axis=0).astype(BF16))

    flr, fli, fls = (flat_ref[i, 0] for i in range(3))
    far, fai = _cexp(flr, fli, fls)
    powers = [(far, fai)]
    for _ in range(L - 1):
        powers.append(_cmul(far, fai, *powers[-1]))

    for n, Lc in enumerate(chunks):
        t_ref, w_ref, v_ref, al_ref = out_refs[4 * n:4 * n + 4]
        for k in range(Lc):
            for t in range(Lc):
                t_ref[0, k * LANE:(k + 1) * LANE, t * LANE:(t + 1) * LANE] = lag[t - k] if t >= k else zero_tile
            w_ref[0, k * LANE:(k + 1) * LANE, :] = w_tiles[Lc - 1 - k]
        for t in range(Lc):
            v_ref[0, :, t * LANE:(t + 1) * LANE] = v_tiles[t]
        al_ref[0] = jnp.concatenate(powers[Lc - 1], axis=1)


def _ssm_prep(row5, col5, flat3, chunks):
    out_specs, out_shape = [], []
    for L in chunks:
        lk = L * LANE
        for shape, dtype in (((lk, lk), BF16), ((lk, 2 * STATE_BLOCK), BF16), ((2 * STATE_BLOCK, lk), BF16),
                             ((1, 2 * STATE_BLOCK), F32)):
            out_specs.append(pl.BlockSpec((1,) + shape, lambda j: (j, 0, 0)))
            out_shape.append(jax.ShapeDtypeStruct((N_LANE_BLOCKS,) + shape, dtype))
    outs = pl.pallas_call(
        functools.partial(_ssm_prep_kernel, chunks=tuple(chunks)),
        grid=(N_LANE_BLOCKS,),
        in_specs=[pl.BlockSpec((5, LANE, SSM_STATE), lambda j: (0, j, 0)),
                  pl.BlockSpec((5, SSM_STATE, LANE), lambda j: (0, 0, j)),
                  pl.BlockSpec((3, 1, 1, STATE_BLOCK), lambda j: (0, j, 0, 0))],
        out_specs=out_specs,
        out_shape=out_shape,
        compiler_params=_params(("arbitrary",), 40),
        name="ssm_prep",
    )(row5, col5, flat3)
    return {L: tuple(outs[4 * n:4 * n + 4]) for n, L in enumerate(chunks)}


def _ssm_kernel(u_ref, t_ref, w_ref, v_ref, al_ref, d_ref, h0_ref, *rest, B, Rn, L, seq, n_cast):
    cast_in, (g_ref, st_ref) = rest[:n_cast], rest[n_cast:n_cast + 2]
    cast_out, scratch = rest[n_cast + 2:2 * n_cast + 2], rest[2 * n_cast + 2:]
    M = B * Rn
    sb = STATE_BLOCK
    io_scr = scratch[0]
    if seq:
        c_scr = scratch[1]
        assert 2 * B == 8 and Rn % 2 == 0, "two chunk rows of B sequences fill one sublane tile"

        @pl.when(pl.program_id(1) == 0)
        def _():
            c_scr[:B, :] = h0_ref[0]
            c_scr[B:, :] = jnp.zeros((B, 2 * sb), F32)

    for b in range(B):
        for k in range(L):
            io_scr[k, pl.ds(b, Rn, stride=B), :] = u_ref[0, b, pl.ds(k, Rn, stride=L), :]
    u = jnp.concatenate([io_scr[k] for k in range(L)], axis=1)
    ub = u.astype(BF16)
    e = _dot(ub, w_ref[0])
    ar = al_ref[0][:, :sb]
    ai = al_ref[0][:, sb:]

    def advance(s, eb):
        sr, si = s[:, :sb], s[:, sb:]
        return jnp.concatenate([ar * sr - ai * si + eb[:, :sb], ar * si + ai * sr + eb[:, sb:]], axis=1)

    if seq:
        local = _dot(ub, t_ref[0]) + d_ref[0] * u
        for src, dst in zip(cast_in, cast_out):
            dst[...] = src[...].astype(BF16)
        top = lax.broadcasted_iota(jnp.int32, (2 * B, 1), 0) < B
        s = c_scr[...]
        starts = []
        for i in range(Rn // 2):
            e2 = e[i * 2 * B:(i + 1) * 2 * B]
            mid = pltpu.roll(advance(s, e2), B, 0)
            starts.append(jnp.where(top, s, mid))
            s = pltpu.roll(advance(mid, e2), B, 0)
        c_scr[...] = s
        st_ref[0] = s[:B, :]
        s_start = jnp.concatenate(starts, axis=0)
    else:
        local = _dot(ub, t_ref[0]) + d_ref[0] * u
        s_start = h0_ref[0]
        st_ref[0] = advance(s_start, e)
    g = jax.nn.gelu(local + _dot(s_start.astype(BF16), v_ref[0]))
    for k in range(L):
        io_scr[k] = g[:, k * LANE:(k + 1) * LANE]
    for b in range(B):
        for k in range(L):
            g_ref[0, b, pl.ds(k, Rn, stride=L), :] = io_scr[k, pl.ds(b, Rn, stride=B), :]


def _ssm_mix(u_view, ops, d_tiles, h0, *, L, Rn, seq, cast=()):
    t_op, w_op, v_op, a_l = ops
    nj = N_LANE_BLOCKS
    _, B, trows, _ = u_view.shape
    lk = L * LANE
    nseq = h0.shape[1]
    tiles = trows // (Rn * L)
    M = B * Rn
    sb2 = 2 * STATE_BLOCK
    first = u_view.shape[0] - nj
    u_spec = pl.BlockSpec((1, B, Rn * L, LANE), lambda j, i: (j + first, 0, i, 0))
    g_spec = pl.BlockSpec((1, B, Rn * L, LANE), lambda j, i: (j, 0, i, 0))
    scratch = [pltpu.VMEM((L, M, LANE), F32)]
    if seq:
        scratch += [pltpu.VMEM((2 * B, sb2), F32)]
    steps = nj * tiles
    assert seq or not cast, "the casts ride along the serial recurrence"
    assert all(w.shape[0] % (steps * BF16_SUBLANES) == 0 for w in cast), "one packed row slice per grid step"
    cast_specs = [pl.BlockSpec((w.shape[0] // steps, w.shape[1]), lambda j, i: (j * tiles + i, 0)) for w in cast]
    return pl.pallas_call(
        functools.partial(_ssm_kernel, B=B, Rn=Rn, L=L, seq=seq, n_cast=len(cast)),
        grid=(nj, tiles),
        in_specs=[u_spec,
                  pl.BlockSpec((1, lk, lk), lambda j, i: (j, 0, 0)),
                  pl.BlockSpec((1, lk, sb2), lambda j, i: (j, 0, 0)),
                  pl.BlockSpec((1, sb2, lk), lambda j, i: (j, 0, 0)),
                  pl.BlockSpec((1, 1, sb2), lambda j, i: (j, 0, 0)),
                  pl.BlockSpec((1, 1, lk), lambda j, i: (j, 0, 0)),
                  pl.BlockSpec((1, nseq, sb2), lambda j, i: (j, 0, 0))] + cast_specs,
        out_specs=[g_spec, pl.BlockSpec((1, nseq, sb2), lambda j, i: (j, 0, 0))] + cast_specs,
        out_shape=[jax.ShapeDtypeStruct((nj,) + u_view.shape[1:], F32),
                   jax.ShapeDtypeStruct((nj, nseq, sb2), F32)]
                  + [jax.ShapeDtypeStruct(w.shape, BF16) for w in cast],
        scratch_shapes=scratch,
        compiler_params=_params(("arbitrary", "arbitrary"), 48,
                                fuse_inputs=[False, False, False, False, False, True, True] + [False] * len(cast)),
        name="ssm_mix_seq" if seq else "ssm_mix_rows",
    )(u_view, t_op, w_op, v_op, a_l, d_tiles, h0, *cast)


def _mix_out_kernel(g_ref, po_ref, x_ref, wglu_ref, bglu_ref, wout_ref, gc_ref, wq_ref, *rest, attend):
    if attend:
        k_ref, v_ref, x1_ref, q_ref = rest
    else:
        x1_ref, q_ref = rest
    g = _lane_blocks(g_ref)
    gate = jax.nn.sigmoid(_dot(g.astype(BF16), wglu_ref[...]) + bglu_ref[...])
    ssm_out = (g * gate).astype(BF16)
    mix = (_dot(_lane_blocks(po_ref).astype(BF16), wout_ref[:POOL_WIDTH, :])
           + _dot(ssm_out, wout_ref[POOL_WIDTH:, :]))
    x1 = x_ref[...] + mix
    x1_ref[...] = x1
    q = _dot(_rms(x1, gc_ref[...]).astype(BF16), wq_ref[...]).astype(BF16)
    if attend:
        heads = [slice(h * XHEAD_DIM, (h + 1) * XHEAD_DIM) for h in range(N_XHEADS)]
        scores = [lax.dot_general(q[:, sl], k_ref[:, sl], _NT, preferred_element_type=F32) for sl in heads]
        probs = [_softmax_rows(s * (XHEAD_DIM ** -0.5)).astype(BF16) for s in scores]
        for sl, p in zip(heads, probs):
            q_ref[:, sl] = _dot(p, v_ref[:, sl]).astype(BF16)
    else:
        q_ref[...] = q


def _mix_out(g_rows, pool_out, x, w_glu, b_glu, w_out, g_cross, w_q, tm, mem_kv=None, rows_per_seq=None):
    rows = x.shape[0]
    attend = mem_kv is not None
    extra_specs, extra = [], []
    if attend:
        tiles = rows_per_seq // tm
        kv_spec = pl.BlockSpec((N_MEM, D_MODEL), lambda i: (i // tiles, 0))
        extra_specs, extra = [kv_spec, kv_spec], list(mem_kv)
    return pl.pallas_call(
        functools.partial(_mix_out_kernel, attend=attend),
        grid=(rows // tm,),
        in_specs=[pl.BlockSpec((N_LANE_BLOCKS, tm, LANE), lambda i: (0, i, 0)),
                  pl.BlockSpec((POOL_LANE_BLOCKS, tm, LANE), lambda i: (0, i, 0)),
                  pl.BlockSpec((tm, D_MODEL), lambda i: (i, 0)),
                  _const_spec((SSM_WIDTH, SSM_WIDTH)), _const_spec((1, SSM_WIDTH)),
                  _const_spec((D_MODEL, D_MODEL)), _const_spec((1, D_MODEL)),
                  _const_spec((D_MODEL, D_MODEL))] + extra_specs,
        out_specs=[pl.BlockSpec((tm, D_MODEL), lambda i: (i, 0)),
                   pl.BlockSpec((tm, D_MODEL), lambda i: (i, 0))],
        out_shape=[jax.ShapeDtypeStruct((rows, D_MODEL), F32),
                   jax.ShapeDtypeStruct((rows, D_MODEL), BF16)],
        compiler_params=_params(("arbitrary",), 56),
        name="mix_out_attn" if attend else "mix_out",
    )(g_rows, pool_out, x, w_glu, b_glu, w_out, g_cross, w_q, *extra)


HEAD_LANE_BLOCKS = XHEAD_DIM // LANE
CACHE_ROW_PITCH = N_XHEADS * HEAD_LANE_BLOCKS


def _cache_rows_view(cache):
    S = cache.shape[0]
    c5 = cache.reshape(S, N_MEM, N_XHEADS, HEAD_LANE_BLOCKS, LANE)
    return c5.transpose(0, 1, 3, 2, 4).reshape(S, N_MEM * CACHE_ROW_PITCH, LANE)


def _cache_rows_unview(rows, nseq):
    r5 = rows.reshape(nseq, N_MEM, HEAD_LANE_BLOCKS, N_XHEADS, LANE)
    return r5.transpose(0, 1, 3, 2, 4).reshape(nseq, N_MEM, N_XHEADS, XHEAD_DIM)


def _mem_kv_kernel(m_ref, g_ref, w_ref, rows_ref, kvb_ref):
    tm = m_ref.shape[0]
    kv = _dot(_rms(m_ref[...], g_ref[...]).astype(BF16), w_ref[...].astype(BF16))
    kvb_ref[...] = kv.astype(BF16)
    for c in range(HEAD_LANE_BLOCKS):
        for h in range(N_XHEADS):
            lo = h * XHEAD_DIM + c * LANE
            rows_ref[pl.ds(c * N_XHEADS + h, tm, stride=CACHE_ROW_PITCH), :] = kv[:, lo:lo + LANE]


def _mem_kv(mem, g_mem, w, tm):
    rows = mem.shape[0]
    return pl.pallas_call(
        _mem_kv_kernel,
        grid=(rows // tm,),
        in_specs=[pl.BlockSpec((tm, D_MODEL), lambda i: (i, 0)),
                  _const_spec((1, D_MODEL)),
                  _const_spec((D_MODEL, D_MODEL))],
        out_specs=[pl.BlockSpec((tm * CACHE_ROW_PITCH, LANE), lambda i: (i, 0)),
                   pl.BlockSpec((tm, D_MODEL), lambda i: (i, 0))],
        out_shape=[jax.ShapeDtypeStruct((rows * CACHE_ROW_PITCH, LANE), F32),
                   jax.ShapeDtypeStruct((rows, D_MODEL), BF16)],
        compiler_params=_params(("arbitrary",), 48),
        name="mem_kv",
    )(mem, g_mem, w)


_NT = (((1,), (1,)), ((), ()))


def _softmax_rows(s):
    e = jnp.exp(s - jnp.max(s, axis=-1, keepdims=True))
    return e / jnp.sum(e, axis=-1, keepdims=True)


def _attn_cache_kernel(q_ref, k_ref, v_ref, o_ref, *, bb):
    nc = HEAD_LANE_BLOCKS
    mc = N_MEM * nc
    lane_c = lax.broadcasted_iota(jnp.int32, (8, mc), 1) & (nc - 1)
    owns = []
    for b in range(bb):
        q8 = q_ref[b]
        for h in range(N_XHEADS):
            xk = k_ref[b, pl.ds(h, mc, stride=N_XHEADS), :].astype(BF16)
            qc = jnp.concatenate([q8[:, h * XHEAD_DIM + c * LANE:h * XHEAD_DIM + (c + 1) * LANE]
                                  for c in range(nc)], axis=0).astype(BF16)
            part = lax.dot_general(qc, xk, _NT, preferred_element_type=F32)
            own = part[8 * (nc - 1):]
            for c in range(nc - 2, -1, -1):
                own = jnp.where(lane_c == c, part[8 * c:8 * (c + 1)], own)
            owns.append(own)
    own_all = jnp.concatenate(owns, axis=0)
    lane_all = lax.broadcasted_iota(jnp.int32, own_all.shape, 1) & (nc - 1)
    pair = own_all + pltpu.roll(own_all, 1, 1)
    full = pair + pltpu.roll(pair, 2, 1)
    s = jnp.where(lane_all == nc - 1, full * (XHEAD_DIM ** -0.5), -1e30)
    e = jnp.exp(s - jnp.max(s, axis=1, keepdims=True))
    p = e / jnp.sum(e, axis=1, keepdims=True)
    p2 = p + pltpu.roll(p, mc - 1, 1)
    p4 = p2 + pltpu.roll(p2, mc - 2, 1)
    for b in range(bb):
        xv = jnp.concatenate([v_ref[b, pl.ds(h, mc, stride=N_XHEADS), :] for h in range(N_XHEADS)],
                             axis=1).astype(BF16)
        pm_rows = []
        for h in range(N_XHEADS):
            r0 = 8 * (b * N_XHEADS + h)
            pm_rows += [jnp.where(lane_c == c, p4[r0:r0 + 8], 0.0) for c in range(nc)]
        o_all = _dot(jnp.concatenate(pm_rows, axis=0).astype(BF16), xv)
        for h in range(N_XHEADS):
            for c in range(nc):
                g = h * nc + c
                o_ref[b, :, h * XHEAD_DIM + c * LANE:h * XHEAD_DIM + (c + 1) * LANE] = (
                    o_all[8 * g:8 * (g + 1), h * LANE:(h + 1) * LANE])


def _attn_cache(q8, k_rows, v_rows, bb):
    nseq = q8.shape[0]
    rows = N_MEM * CACHE_ROW_PITCH
    return pl.pallas_call(
        functools.partial(_attn_cache_kernel, bb=bb),
        grid=(nseq // bb,),
        in_specs=[pl.BlockSpec((bb, 8, D_MODEL), lambda i: (i, 0, 0)),
                  pl.BlockSpec((bb, rows, LANE), lambda i: (i, 0, 0)),
                  pl.BlockSpec((bb, rows, LANE), lambda i: (i, 0, 0))],
        out_specs=pl.BlockSpec((bb, 8, D_MODEL), lambda i: (i, 0, 0)),
        out_shape=jax.ShapeDtypeStruct((nseq, 8, D_MODEL), F32),
        compiler_params=_params(("arbitrary",), 48),
        name="attn_cache",
    )(q8, k_rows, v_rows)


def _oproj_kernel(o_ref, x1_ref, wo_ref, gf_ref, x2_ref, h_ref, *wb_ref):
    w = wo_ref[...].astype(BF16)
    for ref in wb_ref:
        ref[...] = w
    x2 = x1_ref[...] + _dot(o_ref[...], w)
    x2_ref[...] = x2
    h_ref[...] = _rms(x2, gf_ref[...]).astype(BF16)


def _oproj(o, x1, w_o, g_ffn, tm, emit_bf16=False):
    rows = x1.shape[0]
    row_spec = pl.BlockSpec((tm, D_MODEL), lambda i: (i, 0))
    out_specs = [row_spec, row_spec]
    out_shape = [jax.ShapeDtypeStruct((rows, D_MODEL), F32), jax.ShapeDtypeStruct((rows, D_MODEL), BF16)]
    if emit_bf16:
        out_specs.append(_const_spec((D_MODEL, D_MODEL)))
        out_shape.append(jax.ShapeDtypeStruct((D_MODEL, D_MODEL), BF16))
    return pl.pallas_call(
        _oproj_kernel,
        grid=(rows // tm,),
        in_specs=[row_spec, row_spec, _const_spec((D_MODEL, D_MODEL)), _const_spec((1, D_MODEL))],
        out_specs=out_specs,
        out_shape=out_shape,
        compiler_params=_params(("arbitrary",), 56),
        name="oproj",
    )(o, x1, w_o, g_ffn)


def _ffn_kernel(x_hbm, h_ref, wg_ref, wu_ref, wd_ref, gl_ref, y_ref, *rest):
    *bf16_out, x_buf, x_sem = rest
    f = pl.program_id(1)
    last = pl.num_programs(1) - 1
    tm = h_ref.shape[0]

    def x_copy():
        rows = pl.ds(pl.multiple_of(pl.program_id(0) * tm, tm), tm)
        return pltpu.make_async_copy(x_hbm.at[rows, :], x_buf, x_sem)

    def down():
        wg, wu, wd = (w[...].astype(BF16) for w in (wg_ref, wu_ref, wd_ref))
        for ref, w in zip(bf16_out, (wg, wu, wd)):
            ref[...] = w
        h = h_ref[...]
        z = (jax.nn.silu(_dot(h, wg)) * _dot(h, wu)).astype(BF16)
        return _dot(z, wd)

    @pl.when(f == 0)
    def _():
        x_copy().start()
        y_ref[...] = down()

    @pl.when((f > 0) & (f < last))
    def _():
        y_ref[...] += down()

    @pl.when(f == last)
    def _():
        x_copy().wait()
        y_ref[...] = _rms(x_buf[...] + (y_ref[...] + down()), gl_ref[...])


def _ffn(x2, h, w_gate, w_up, w_down, g_final, tm, tf, emit_bf16=False):
    rows = x2.shape[0]
    nf = D_FF // tf
    assert nf >= 2, "the residual copy is started on the first d_ff tile and waited on the last"
    row_spec = pl.BlockSpec((tm, D_MODEL), lambda i, f: (i, 0))
    out_specs = [row_spec]
    out_shape = [jax.ShapeDtypeStruct((rows, D_MODEL), F32)]
    if emit_bf16:
        once = lambda i, f: jnp.where(i == 0, f, nf)
        out_specs += [pl.BlockSpec((D_MODEL, tf), lambda i, f: (0, once(i, f))),
                      pl.BlockSpec((D_MODEL, tf), lambda i, f: (0, once(i, f))),
                      pl.BlockSpec((tf, D_MODEL), lambda i, f: (once(i, f), 0))]
        out_shape += [jax.ShapeDtypeStruct((D_MODEL, D_FF + tf), BF16),
                      jax.ShapeDtypeStruct((D_MODEL, D_FF + tf), BF16),
                      jax.ShapeDtypeStruct((D_FF + tf, D_MODEL), BF16)]
    return pl.pallas_call(
        _ffn_kernel,
        grid=(rows // tm, nf),
        in_specs=[pl.BlockSpec(memory_space=pl.ANY), row_spec,
                  pl.BlockSpec((D_MODEL, tf), lambda i, f: (0, f)),
                  pl.BlockSpec((D_MODEL, tf), lambda i, f: (0, f)),
                  pl.BlockSpec((tf, D_MODEL), lambda i, f: (f, 0)),
                  _const_spec((1, D_MODEL))],
        out_specs=out_specs,
        out_shape=out_shape,
        scratch_shapes=[pltpu.VMEM((tm, D_MODEL), F32), pltpu.SemaphoreType.DMA(())],
        compiler_params=_params(("arbitrary", "arbitrary"), 60 if emit_bf16 else 56),
        name="ffn",
    )(x2, h, w_gate, w_up, w_down, g_final)


def _ssm_param_layouts(lam_re, lam_im, log_step, b_re, b_im, c_re, c_im):
    G, P = lam_re.shape
    ls = jnp.broadcast_to(log_step[:, None], (G, P))
    rep = lambda a: jnp.repeat(a, SSM_GROUP, axis=0)
    b_rows = lambda b: jnp.swapaxes(b, 1, 2).reshape(G * SSM_GROUP, P)
    row5 = jnp.stack([rep(lam_re), rep(lam_im), rep(ls), b_rows(b_re), b_rows(b_im)])
    col5 = jnp.stack([rep(lam_re).T, rep(lam_im).T, rep(ls).T,
                      c_re.reshape(G * SSM_GROUP, P).T, c_im.reshape(G * SSM_GROUP, P).T])
    flat = lambda a: a.reshape(N_LANE_BLOCKS, 1, STATE_BLOCK)
    flat3 = jnp.stack([flat(lam_re), flat(lam_im), flat(ls)])
    return row5, col5, flat3


def _states_to_blocks(h_re, h_im):
    S = h_re.shape[0]
    blk = lambda h: h.reshape(S, N_LANE_BLOCKS, STATE_BLOCK).transpose(1, 0, 2)
    return jnp.concatenate([blk(h_re), blk(h_im)], axis=-1)


def _blocks_to_states(st):
    S = st.shape[1]
    unblk = lambda a: a.transpose(1, 0, 2).reshape(1, S, N_SSM_GROUPS, SSM_STATE)
    return unblk(st[:, :, :STATE_BLOCK]), unblk(st[:, :, STATE_BLOCK:])


def kernel(x_prompt, x_sample, mem_prompt, state_pool_buf, state_ssm_re, state_ssm_im, cache_mem_k, cache_mem_v, g_mix, w_in, w_pool, pool_scale, ssm_lam_re, ssm_lam_im, ssm_log_step, ssm_b_re, ssm_b_im, ssm_c_re, ssm_c_im, ssm_d, w_glu, b_glu, w_out, g_cross, g_mem, w_q, w_k, w_v, w_o, g_ffn, w_gate, w_up, w_down, g_final):
    assert g_mix.shape[0] == 1, "single-layer step"
    B, T, _ = x_prompt.shape
    S, Ts, _ = x_sample.shape
    bf = lambda w: w.astype(BF16)
    vec = lambda v: v.reshape(1, -1)

    w_pool_b = bf(w_pool[0])

    row5, col5, flat3 = _ssm_param_layouts(ssm_lam_re[0], ssm_lam_im[0], ssm_log_step[0],
                                           ssm_b_re[0], ssm_b_im[0], ssm_c_re[0], ssm_c_im[0])
    d_blocks = ssm_d[0].reshape(N_LANE_BLOCKS, 1, LANE)

    gains = dict(g_mix=vec(g_mix[0]), scale=vec(pool_scale[0]), b_glu=vec(b_glu[0]), g_cross=vec(g_cross[0]),
                 g_ffn=vec(g_ffn[0]), g_final=vec(g_final))

    chunk_ops = _ssm_prep(row5, col5, flat3, sorted({PROMPT_CHUNK, Ts}, reverse=True))

    def ssm(u_view, L, h0_blocks, **tiles):
        g_act, st, *cast = _ssm_mix(u_view, chunk_ops[L], jnp.tile(d_blocks, (1, 1, L)), h0_blocks, L=L, **tiles)
        rows = u_view.shape[1] * u_view.shape[2]
        return g_act.reshape(N_LANE_BLOCKS, rows, LANE), _blocks_to_states(st), cast

    def finish(o, x1, w_o_any, ffn_weights, ffn_tiles, emit_bf16=False):
        x2, hf, *w_o_bf16 = _oproj(o, x1, w_o_any, gains["g_ffn"], ROW_TILE, emit_bf16=emit_bf16)
        y, *ffn_bf16 = _ffn(x2, hf, *ffn_weights, gains["g_final"], *ffn_tiles, emit_bf16=emit_bf16)
        return y, w_o_bf16, ffn_bf16

    mem = mem_prompt.reshape(B * N_MEM, D_MODEL)
    k_rows, k_b = _mem_kv(mem, vec(g_mem[0]), w_k[0], ROW_TILE)
    v_rows, v_b = _mem_kv(mem, vec(g_mem[0]), w_v[0], ROW_TILE)
    xp = x_prompt.reshape(B * T, D_MODEL)
    us_p, po_p, tail_p = _in_proj_pool(xp, gains["g_mix"], w_in[0], w_pool_b, gains["scale"],
                                       tc=ROW_TILE, rows_per_seq=T)
    h0_p = jnp.zeros((N_LANE_BLOCKS, B, 2 * STATE_BLOCK), F32)
    g_p, (re_p, im_p), (w_glu_b, w_out_b, w_q_b) = ssm(us_p.reshape(N_LANE_BLOCKS, B, T, LANE), PROMPT_CHUNK, h0_p,
                                                       Rn=PROMPT_CHUNK_ROWS, seq=True,
                                                       cast=(w_glu[0], w_out[0], w_q[0]))
    x1_p, o_p = _mix_out(g_p, po_p, xp, w_glu_b, gains["b_glu"], w_out_b, gains["g_cross"], w_q_b, ROW_TILE,
                         mem_kv=(k_b, v_b), rows_per_seq=T)
    yp, (w_o_b,), ffn_bf16 = finish(o_p, x1_p, w_o[0], (w_gate[0], w_up[0], w_down[0]), PROMPT_FFN_TILES,
                                    emit_bf16=True)
    pb_p = tail_p[:, POOL_HIST - POOL_BUF:]

    assert PAST_LEN >= POOL_BUF
    xs = x_sample.reshape(S * Ts, D_MODEL)
    u_s = _in_proj(xs, gains["g_mix"], w_in[0], ROW_TILE)
    po_s, buf_tm = _pool_mix_buf(u_s, state_pool_buf[0].transpose(1, 0, 2), w_pool_b, gains["scale"], L=Ts)
    pb_s = buf_tm.transpose(1, 0, 2)
    h0_s = _states_to_blocks(state_ssm_re[0], state_ssm_im[0])
    g_s, (re_s, im_s), _ = ssm(u_s.reshape(MIX_LANE_BLOCKS, 1, S * Ts, LANE), Ts, h0_s, Rn=S, seq=False)
    x1_s, q_s = _mix_out(g_s, po_s, xs, w_glu_b, gains["b_glu"], w_out_b, gains["g_cross"], w_q_b, ROW_TILE)
    q8 = jnp.pad(q_s.reshape(S, Ts, D_MODEL).astype(F32), ((0, 0), (0, 8 - Ts), (0, 0)))
    o8 = _attn_cache(q8, _cache_rows_view(cache_mem_k[0]), _cache_rows_view(cache_mem_v[0]), CACHE_SEQS_PER_STEP)
    ys, _, _ = finish(o8[:, :Ts].reshape(S * Ts, D_MODEL).astype(BF16), x1_s, w_o_b, ffn_bf16, SAMPLE_FFN_TILES)

    mk = _cache_rows_unview(k_rows, B)[None]
    mv = _cache_rows_unview(v_rows, B)[None]
    return (yp.reshape(B, T, D_MODEL), ys.reshape(S, Ts, D_MODEL), pb_p[None], re_p, im_p, mk, mv,
            pb_s[None], re_s, im_s)
```

```python
import functools

import jax
import jax.numpy as jnp
from jax import lax
from jax.experimental import pallas as pl
from jax.experimental.pallas import tpu as pltpu

F32 = jnp.float32
BF16 = jnp.bfloat16

D_MODEL = 2048
POOL_WIDTH = 1024
SSM_WIDTH = 1024
POOL_WINDOWS = (2, 4, 8, 16)
POOL_GROUP = POOL_WIDTH // len(POOL_WINDOWS)
POOL_BUF = max(POOL_WINDOWS) - 1
SSM_GROUP = 16
N_SSM_GROUPS = SSM_WIDTH // SSM_GROUP
SSM_STATE = 64
N_MEM = 256
N_XHEADS = 4
XHEAD_DIM = D_MODEL // N_XHEADS
D_FF = 5632
EPS = 1e-6
PAST_LEN = 16384

LANE = 128
BF16_SUBLANES = 16
N_LANE_BLOCKS = SSM_WIDTH // LANE
GROUPS_PER_BLOCK = LANE // SSM_GROUP
STATE_BLOCK = GROUPS_PER_BLOCK * SSM_STATE
MIB = 1024 * 1024
HI = lax.Precision.HIGHEST

ROW_TILE = 512
ROW_GROUPS = 4
PROMPT_CHUNK = 8
PROMPT_CHUNK_ROWS = 128
PROMPT_FFN_TILES = (1024, 256)
SAMPLE_FFN_TILES = (512, 512)
CACHE_SEQS_PER_STEP = 2
CACHE_RING = 3


def _params(semantics, vmem_mib, fuse_inputs=None):
    return pltpu.CompilerParams(dimension_semantics=semantics, vmem_limit_bytes=vmem_mib * MIB,
                                allow_input_fusion=fuse_inputs)


def _const_spec(shape):
    return pl.BlockSpec(shape, lambda *_: (0,) * len(shape), pipeline_mode=pl.Buffered(1))


def _rms(x, g):
    r = lax.rsqrt(jnp.mean(x * x, axis=-1, keepdims=True) + EPS)
    return x * r * g


def _dot(a, b):
    return jnp.dot(a, b, preferred_element_type=F32)


MIX_LANE_BLOCKS = D_MODEL // LANE


def _in_proj_kernel(x_ref, g_ref, w_ref, u_ref):
    h = _rms(x_ref[...], g_ref[...]).astype(BF16)
    u = _dot(h, w_ref[...].astype(BF16))
    for j in range(MIX_LANE_BLOCKS):
        u_ref[j] = u[:, j * LANE:(j + 1) * LANE]


def _in_proj(x, g, w, tm):
    rows = x.shape[0]
    return pl.pallas_call(
        _in_proj_kernel,
        grid=(rows // tm,),
        in_specs=[pl.BlockSpec((tm, D_MODEL), lambda i: (i, 0)),
                  _const_spec((1, D_MODEL)),
                  _const_spec((D_MODEL, D_MODEL))],
        out_specs=pl.BlockSpec((MIX_LANE_BLOCKS, tm, LANE), lambda i: (0, i, 0)),
        out_shape=jax.ShapeDtypeStruct((MIX_LANE_BLOCKS, rows, LANE), F32),
        compiler_params=_params(("arbitrary",), 52),
        name="in_proj",
    )(x, g, w)


def _lane_blocks(ref, rows=slice(None)):
    return jnp.concatenate([ref[j, rows, :] for j in range(ref.shape[0])], axis=1)


POOL_HIST = POOL_BUF + 1


def _in_proj_pool_kernel(x_ref, g_ref, w_ref, wp_ref, sc_ref, us_ref, o_ref, tail_ref, ext_ref, *, tc,
                         tiles_per_seq):
    it = lax.rem(pl.program_id(0), tiles_per_seq)

    @pl.when(it == 0)
    def _():
        ext_ref[:POOL_HIST, :] = jnp.zeros((POOL_HIST, POOL_WIDTH), F32)

    nq = ROW_GROUPS
    rq = tc // nq
    w = w_ref[...].astype(BF16)
    for c in range(nq):
        rows = slice(c * rq, (c + 1) * rq)
        u = _dot(_rms(x_ref[rows, :], g_ref[...]).astype(BF16), w)
        for j in range(N_LANE_BLOCKS):
            us_ref[j, rows, :] = u[:, POOL_WIDTH + j * LANE:POOL_WIDTH + (j + 1) * LANE]
        ext_ref[POOL_HIST + c * rq:POOL_HIST + (c + 1) * rq, :] = u[:, :POOL_WIDTH]
    for c in range(nq):
        rows = slice(c * rq, (c + 1) * rq)
        pos = it * tc + c * rq + lax.broadcasted_iota(jnp.int32, (rq, 1), 0)
        for g, w in enumerate(POOL_WINDOWS):
            sl = slice(g * POOL_GROUP, (g + 1) * POOL_GROUP)
            z = ext_ref[c * rq:(c + 1) * rq + POOL_HIST, sl]
            s, span = z, 1
            while span < w:
                s = s + pltpu.roll(s, span, 0)
                span *= 2
            cnt = jnp.minimum(pos + 1, w).astype(F32)
            pooled = (s[POOL_HIST:] / cnt - z[POOL_HIST:]).astype(BF16)
            out = _dot(pooled, wp_ref[g]) * sc_ref[:, sl]
            for jj in range(POOL_GROUP // LANE):
                o_ref[g * (POOL_GROUP // LANE) + jj, rows, :] = out[:, jj * LANE:(jj + 1) * LANE]
    last = ext_ref[tc:tc + POOL_HIST, :]
    tail_ref[0] = last
    ext_ref[:POOL_HIST, :] = last


POOL_LANE_BLOCKS = POOL_WIDTH // LANE


def _in_proj_pool(x, g, w, w_pool, scale, *, tc, rows_per_seq):
    rows = x.shape[0]
    tiles_per_seq = rows_per_seq // tc
    slab_spec = pl.BlockSpec((N_LANE_BLOCKS, tc, LANE), lambda i: (0, i, 0))
    slabs = jax.ShapeDtypeStruct((N_LANE_BLOCKS, rows, LANE), F32)
    return pl.pallas_call(
        functools.partial(_in_proj_pool_kernel, tc=tc, tiles_per_seq=tiles_per_seq),
        grid=(rows // tc,),
        in_specs=[pl.BlockSpec((tc, D_MODEL), lambda i: (i, 0)),
                  _const_spec((1, D_MODEL)),
                  _const_spec((D_MODEL, D_MODEL)),
                  _const_spec((len(POOL_WINDOWS), POOL_GROUP, POOL_GROUP)),
                  _const_spec((1, POOL_WIDTH))],
        out_specs=[slab_spec, slab_spec,
                   pl.BlockSpec((1, POOL_HIST, POOL_WIDTH), lambda i: (i // tiles_per_seq, 0, 0))],
        out_shape=[slabs, slabs,
                   jax.ShapeDtypeStruct((rows // rows_per_seq, POOL_HIST, POOL_WIDTH), F32)],
        scratch_shapes=[pltpu.VMEM((tc + POOL_HIST, POOL_WIDTH), F32)],
        compiler_params=_params(("arbitrary",), 48),
        name="in_proj_pool",
    )(x, g, w, w_pool, scale)


def _pool_buf_kernel(u_ref, buf_ref, wp_ref, sc_ref, o_ref, nb_ref, *, L):
    nseq = buf_ref.shape[1]
    per_group = POOL_GROUP // LANE

    def token_rows(t):
        return pl.ds(t, nseq, stride=L)

    def slab(idx, g):
        if idx < POOL_BUF:
            return buf_ref[idx, :, g * POOL_GROUP:(g + 1) * POOL_GROUP]
        return jnp.concatenate([u_ref[g * per_group + jj, token_rows(idx - POOL_BUF), :]
                                for jj in range(per_group)], axis=1)

    for g, w in enumerate(POOL_WINDOWS):
        sl = slice(g * POOL_GROUP, (g + 1) * POOL_GROUP)
        pooled = []
        for t in range(L):
            acc = slab(POOL_BUF + t, g)
            for back in range(1, w):
                acc = acc + slab(POOL_BUF + t - back, g)
            pooled.append((acc / float(w) - slab(POOL_BUF + t, g)).astype(BF16))
        out = _dot(jnp.concatenate(pooled, axis=0), wp_ref[g]) * sc_ref[:, sl]
        for t in range(L):
            for jj in range(per_group):
                o_ref[g * per_group + jj, token_rows(t), :] = out[t * nseq:(t + 1) * nseq,
                                                                  jj * LANE:(jj + 1) * LANE]
    keep = POOL_BUF - L
    for k in range(keep):
        nb_ref[k] = buf_ref[k + L]
    for t in range(L):
        nb_ref[keep + t] = _lane_blocks(u_ref, token_rows(t))


def _pool_mix_buf(u_blocks, buf_tm, w_pool, scale, *, L):
    rows = u_blocks.shape[1]
    slab_spec = pl.BlockSpec((POOL_LANE_BLOCKS, rows, LANE), lambda i: (0, 0, 0))
    buf_spec = pl.BlockSpec(buf_tm.shape, lambda i: (0, 0, 0))
    return pl.pallas_call(
        functools.partial(_pool_buf_kernel, L=L),
        grid=(1,),
        in_specs=[slab_spec, buf_spec, _const_spec((len(POOL_WINDOWS), POOL_GROUP, POOL_GROUP)),
                  _const_spec((1, POOL_WIDTH))],
        out_specs=[slab_spec, buf_spec],
        out_shape=[jax.ShapeDtypeStruct((POOL_LANE_BLOCKS, rows, LANE), F32),
                   jax.ShapeDtypeStruct(buf_tm.shape, F32)],
        compiler_params=_params(("arbitrary",), 48),
        name="pool_mix_buf",
    )(u_blocks, buf_tm, w_pool, scale)


def _cmul(ar, ai, br, bi):
    return ar * br - ai * bi, ar * bi + ai * br


def _cexp(lam_re, lam_im, log_step):
    delta = jnp.exp(log_step)
    mag = jnp.exp(lam_re * delta)
    ang = lam_im * delta
    return mag * jnp.cos(ang), mag * jnp.sin(ang)


def _ssm_prep_kernel(row_ref, col_ref, flat_ref, *out_refs, chunks):
    L = max(chunks)
    lr, li, ls, br, bi = (row_ref[i] for i in range(5))
    ar, ai = _cexp(lr, li, ls)
    den = lr * lr + li * li
    xr = ar - 1.0
    fr = (xr * lr + ai * li) / den
    fi = (ai * lr - xr * li) / den
    zs = [_cmul(fr, fi, br, bi)]
    for _ in range(1, L):
        zs.append(_cmul(ar, ai, *zs[-1]))

    lrc, lic, lsc, cr, ci = (col_ref[i] for i in range(5))
    acr, aci = _cexp(lrc, lic, lsc)
    xs = []
    cur = (cr, ci)
    for _ in range(L):
        cur = _cmul(acr, aci, *cur)
        xs.append(cur)

    def iota(shape, dim):
        return lax.broadcasted_iota(jnp.int32, shape, dim)

    same_tt = (iota((LANE, LANE), 0) >> 4) == (iota((LANE, LANE), 1) >> 4)
    c_stack = jnp.concatenate([cr, -ci], axis=0)
    zero_tile = jnp.zeros((LANE, LANE), BF16)
    lag = []
    for d in range(L):
        z_stack = jnp.concatenate([zs[d][0], zs[d][1]], axis=1)
        tile = jnp.dot(z_stack, c_stack, precision=HI, preferred_element_type=F32)
        lag.append(jnp.where(same_tt, tile, 0.0).astype(BF16))

    same_w = (iota((LANE, STATE_BLOCK), 0) >> 4) == (iota((LANE, STATE_BLOCK), 1) >> 6)
    w_tiles = []
    for zr, zi in zs:
        wr = jnp.where(same_w, jnp.concatenate([zr] * GROUPS_PER_BLOCK, axis=1), 0.0)
        wi = jnp.where(same_w, jnp.concatenate([zi] * GROUPS_PER_BLOCK, axis=1), 0.0)
        w_tiles.append(jnp.concatenate([wr, wi], axis=1).astype(BF16))

    same_v = (iota((STATE_BLOCK, LANE), 0) >> 6) == (iota((STATE_BLOCK, LANE), 1) >> 4)
    v_tiles = []
    for xr_t, xi_t in xs:
        vr = jnp.where(same_v, jnp.concatenate([xr_t] * GROUPS_PER_BLOCK, axis=0), 0.0)
        vi = jnp.where(same_v, jnp.concatenate([xi_t] * GROUPS_PER_BLOCK, axis=0), 0.0)
        v_tiles.append(jnp.concatenate([vr, -vi], axis=0).astype(BF16))

    flr, fli, fls = (flat_ref[i, 0] for i in range(3))
    far, fai = _cexp(flr, fli, fls)
    powers = [(far, fai)]
    for _ in range(L - 1):
        powers.append(_cmul(far, fai, *powers[-1]))

    for n, Lc in enumerate(chunks):
        t_ref, w_ref, v_ref, al_ref = out_refs[4 * n:4 * n + 4]
        for k in range(Lc):
            for t in range(Lc):
                t_ref[0, k * LANE:(k + 1) * LANE, t * LANE:(t + 1) * LANE] = lag[t - k] if t >= k else zero_tile
            w_ref[0, k * LANE:(k + 1) * LANE, :] = w_tiles[Lc - 1 - k]
        for t in range(Lc):
            v_ref[0, :, t * LANE:(t + 1) * LANE] = v_tiles[t]
        al_ref[0] = jnp.concatenate(powers[Lc - 1], axis=1)


def _ssm_prep(row5, col5, flat3, chunks):
    out_specs, out_shape = [], []
    for L in chunks:
        lk = L * LANE
        for shape, dtype in (((lk, lk), BF16), ((lk, 2 * STATE_BLOCK), BF16), ((2 * STATE_BLOCK, lk), BF16),
                             ((1, 2 * STATE_BLOCK), F32)):
            out_specs.append(pl.BlockSpec((1,) + shape, lambda j: (j, 0, 0)))
            out_shape.append(jax.ShapeDtypeStruct((N_LANE_BLOCKS,) + shape, dtype))
    outs = pl.pallas_call(
        functools.partial(_ssm_prep_kernel, chunks=tuple(chunks)),
        grid=(N_LANE_BLOCKS,),
        in_specs=[pl.BlockSpec((5, LANE, SSM_STATE), lambda j: (0, j, 0)),
                  pl.BlockSpec((5, SSM_STATE, LANE), lambda j: (0, 0, j)),
                  pl.BlockSpec((3, 1, 1, STATE_BLOCK), lambda j: (0, j, 0, 0))],
        out_specs=out_specs,
        out_shape=out_shape,
        compiler_params=_params(("arbitrary",), 40),
        name="ssm_prep",
    )(row5, col5, flat3)
    return {L: tuple(outs[4 * n:4 * n + 4]) for n, L in enumerate(chunks)}


def _ssm_kernel(u_ref, t_ref, w_ref, v_ref, al_ref, d_ref, h0_ref, *rest, B, Rn, L, seq, n_cast):
    cast_in, (g_ref, st_ref) = rest[:n_cast], rest[n_cast:n_cast + 2]
    cast_out, scratch = rest[n_cast + 2:2 * n_cast + 2], rest[2 * n_cast + 2:]
    M = B * Rn
    sb = STATE_BLOCK
    io_scr = scratch[0]
    if seq:
        c_scr = scratch[1]
        assert 2 * B == 8 and Rn % 2 == 0, "two chunk rows of B sequences fill one sublane tile"

        @pl.when(pl.program_id(1) == 0)
        def _():
            c_scr[:B, :] = h0_ref[0]
            c_scr[B:, :] = jnp.zeros((B, 2 * sb), F32)

    for src, dst in zip(cast_in, cast_out):
        dst[...] = src[...].astype(BF16)

    for b in range(B):
        for k in range(L):
            io_scr[k, pl.ds(b, Rn, stride=B), :] = u_ref[0, b, pl.ds(k, Rn, stride=L), :]
    u = jnp.concatenate([io_scr[k] for k in range(L)], axis=1)
    ub = u.astype(BF16)
    e = _dot(ub, w_ref[0])
    ar = al_ref[0][:, :sb]
    ai = al_ref[0][:, sb:]

    def advance(s, eb):
        sr, si = s[:, :sb], s[:, sb:]
        return jnp.concatenate([ar * sr - ai * si + eb[:, :sb], ar * si + ai * sr + eb[:, sb:]], axis=1)

    if seq:
        local = _dot(ub, t_ref[0]) + d_ref[0] * u
        top = lax.broadcasted_iota(jnp.int32, (2 * B, 1), 0) < B
        s = c_scr[...]
        starts = []
        for i in range(Rn // 2):
            e2 = e[i * 2 * B:(i + 1) * 2 * B]
            mid = pltpu.roll(advance(s, e2), B, 0)
            starts.append(jnp.where(top, s, mid))
            s = pltpu.roll(advance(mid, e2), B, 0)
        c_scr[...] = s
        st_ref[0] = s[:B, :]
        s_start = jnp.concatenate(starts, axis=0)
    else:
        local = _dot(ub, t_ref[0]) + d_ref[0] * u
        s_start = h0_ref[0]
        st_ref[0] = advance(s_start, e)
    g = jax.nn.gelu(local + _dot(s_start.astype(BF16), v_ref[0]))
    for k in range(L):
        io_scr[k] = g[:, k * LANE:(k + 1) * LANE]
    for b in range(B):
        for k in range(L):
            g_ref[0, b, pl.ds(k, Rn, stride=L), :] = io_scr[k, pl.ds(b, Rn, stride=B), :]


def _ssm_mix(u_view, ops, d_tiles, h0, *, L, Rn, seq, cast=()):
    t_op, w_op, v_op, a_l = ops
    nj = N_LANE_BLOCKS
    _, B, trows, _ = u_view.shape
    lk = L * LANE
    nseq = h0.shape[1]
    tiles = trows // (Rn * L)
    M = B * Rn
    sb2 = 2 * STATE_BLOCK
    first = u_view.shape[0] - nj
    u_spec = pl.BlockSpec((1, B, Rn * L, LANE), lambda j, i: (j + first, 0, i, 0))
    g_spec = pl.BlockSpec((1, B, Rn * L, LANE), lambda j, i: (j, 0, i, 0))
    scratch = [pltpu.VMEM((L, M, LANE), F32)]
    if seq:
        scratch += [pltpu.VMEM((2 * B, sb2), F32)]
    steps = nj * tiles
    assert all(w.shape[0] % (steps * BF16_SUBLANES) == 0 for w in cast), "one packed row slice per grid step"
    cast_specs = [pl.BlockSpec((w.shape[0] // steps, w.shape[1]), lambda j, i: (j * tiles + i, 0)) for w in cast]
    return pl.pallas_call(
        functools.partial(_ssm_kernel, B=B, Rn=Rn, L=L, seq=seq, n_cast=len(cast)),
        grid=(nj, tiles),
        in_specs=[u_spec,
                  pl.BlockSpec((1, lk, lk), lambda j, i: (j, 0, 0)),
                  pl.BlockSpec((1, lk, sb2), lambda j, i: (j, 0, 0)),
                  pl.BlockSpec((1, sb2, lk), lambda j, i: (j, 0, 0)),
                  pl.BlockSpec((1, 1, sb2), lambda j, i: (j, 0, 0)),
                  pl.BlockSpec((1, 1, lk), lambda j, i: (j, 0, 0)),
                  pl.BlockSpec((1, nseq, sb2), lambda j, i: (j, 0, 0))] + cast_specs,
        out_specs=[g_spec, pl.BlockSpec((1, nseq, sb2), lambda j, i: (j, 0, 0))] + cast_specs,
        out_shape=[jax.ShapeDtypeStruct((nj,) + u_view.shape[1:], F32),
                   jax.ShapeDtypeStruct((nj, nseq, sb2), F32)]
                  + [jax.ShapeDtypeStruct(w.shape, BF16) for w in cast],
        scratch_shapes=scratch,
        compiler_params=_params(("arbitrary", "arbitrary"), 48,
                                fuse_inputs=[False, False, False, False, False, True, True] + [False] * len(cast)),
        name="ssm_mix_seq" if seq else "ssm_mix_rows",
    )(u_view, t_op, w_op, v_op, a_l, d_tiles, h0, *cast)


def _mix_out_kernel(g_ref, po_ref, x_ref, wglu_ref, bglu_ref, wout_ref, gc_ref, wq_ref, *rest, attend):
    if attend:
        k_ref, v_ref, x1_ref, q_ref = rest
    else:
        x1_ref, q_ref = rest
    g = _lane_blocks(g_ref)
    gate = jax.nn.sigmoid(_dot(g.astype(BF16), wglu_ref[...]) + bglu_ref[...])
    ssm_out = (g * gate).astype(BF16)
    mix = (_dot(_lane_blocks(po_ref).astype(BF16), wout_ref[:POOL_WIDTH, :])
           + _dot(ssm_out, wout_ref[POOL_WIDTH:, :]))
    x1 = x_ref[...] + mix
    x1_ref[...] = x1
    q = _dot(_rms(x1, gc_ref[...]).astype(BF16), wq_ref[...]).astype(BF16)
    if attend:
        heads = [slice(h * XHEAD_DIM, (h + 1) * XHEAD_DIM) for h in range(N_XHEADS)]
        scores = [lax.dot_general(q[:, sl], k_ref[:, sl], _NT, preferred_element_type=F32) for sl in heads]
        probs = [_softmax_rows(s * (XHEAD_DIM ** -0.5)).astype(BF16) for s in scores]
        for sl, p in zip(heads, probs):
            q_ref[:, sl] = _dot(p, v_ref[:, sl]).astype(BF16)
    else:
        q_ref[...] = q


def _mix_out(g_rows, pool_out, x, w_glu, b_glu, w_out, g_cross, w_q, tm, mem_kv=None, rows_per_seq=None):
    rows = x.shape[0]
    attend = mem_kv is not None
    extra_specs, extra = [], []
    if attend:
        tiles = rows_per_seq // tm
        kv_spec = pl.BlockSpec((N_MEM, D_MODEL), lambda i: (i // tiles, 0))
        extra_specs, extra = [kv_spec, kv_spec], list(mem_kv)
    return pl.pallas_call(
        functools.partial(_mix_out_kernel, attend=attend),
        grid=(rows // tm,),
        in_specs=[pl.BlockSpec((N_LANE_BLOCKS, tm, LANE), lambda i: (0, i, 0)),
                  pl.BlockSpec((POOL_LANE_BLOCKS, tm, LANE), lambda i: (0, i, 0)),
                  pl.BlockSpec((tm, D_MODEL), lambda i: (i, 0)),
                  _const_spec((SSM_WIDTH, SSM_WIDTH)), _const_spec((1, SSM_WIDTH)),
                  _const_spec((D_MODEL, D_MODEL)), _const_spec((1, D_MODEL)),
                  _const_spec((D_MODEL, D_MODEL))] + extra_specs,
        out_specs=[pl.BlockSpec((tm, D_MODEL), lambda i: (i, 0)),
                   pl.BlockSpec((tm, D_MODEL), lambda i: (i, 0))],
        out_shape=[jax.ShapeDtypeStruct((rows, D_MODEL), F32),
                   jax.ShapeDtypeStruct((rows, D_MODEL), BF16)],
        compiler_params=_params(("arbitrary",), 56),
        name="mix_out_attn" if attend else "mix_out",
    )(g_rows, pool_out, x, w_glu, b_glu, w_out, g_cross, w_q, *extra)


HEAD_LANE_BLOCKS = XHEAD_DIM // LANE
CACHE_ROW_PITCH = N_XHEADS * HEAD_LANE_BLOCKS


def _cache_rows_view(cache):
    S = cache.shape[0]
    c5 = cache.reshape(S, N_MEM, N_XHEADS, HEAD_LANE_BLOCKS, LANE)
    return c5.transpose(0, 1, 3, 2, 4).reshape(S, N_MEM * CACHE_ROW_PITCH, LANE)


def _cache_rows_unview(rows, nseq):
    r5 = rows.reshape(nseq, N_MEM, HEAD_LANE_BLOCKS, N_XHEADS, LANE)
    return r5.transpose(0, 1, 3, 2, 4).reshape(nseq, N_MEM, N_XHEADS, XHEAD_DIM)


def _mem_kv_kernel(m_ref, g_ref, w_ref, rows_ref, kvb_ref):
    tm = m_ref.shape[0]
    kv = _dot(_rms(m_ref[...], g_ref[...]).astype(BF16), w_ref[...].astype(BF16))
    kvb_ref[...] = kv.astype(BF16)
    for c in range(HEAD_LANE_BLOCKS):
        for h in range(N_XHEADS):
            lo = h * XHEAD_DIM + c * LANE
            rows_ref[pl.ds(c * N_XHEADS + h, tm, stride=CACHE_ROW_PITCH), :] = kv[:, lo:lo + LANE]


def _mem_kv(mem, g_mem, w, tm):
    rows = mem.shape[0]
    return pl.pallas_call(
        _mem_kv_kernel,
        grid=(rows // tm,),
        in_specs=[pl.BlockSpec((tm, D_MODEL), lambda i: (i, 0)),
                  _const_spec((1, D_MODEL)),
                  _const_spec((D_MODEL, D_MODEL))],
        out_specs=[pl.BlockSpec((tm * CACHE_ROW_PITCH, LANE), lambda i: (i, 0)),
                   pl.BlockSpec((tm, D_MODEL), lambda i: (i, 0))],
        out_shape=[jax.ShapeDtypeStruct((rows * CACHE_ROW_PITCH, LANE), F32),
                   jax.ShapeDtypeStruct((rows, D_MODEL), BF16)],
        compiler_params=_params(("arbitrary",), 48),
        name="mem_kv",
    )(mem, g_mem, w)


_NT = (((1,), (1,)), ((), ()))


def _softmax_rows(s):
    e = jnp.exp(s - jnp.max(s, axis=-1, keepdims=True))
    return e / jnp.sum(e, axis=-1, keepdims=True)


def _attn_cache_kernel(q_ref, k_hbm, v_hbm, o_ref, k_buf, v_buf, sem, *, bb):
    step = pl.program_id(0)

    def copies(s):
        return [pltpu.make_async_copy(hbm.at[pl.ds(s * bb, bb)], buf.at[s % CACHE_RING], sem.at[w, s % CACHE_RING])
                for w, (hbm, buf) in enumerate(((k_hbm, k_buf), (v_hbm, v_buf)))]

    @pl.when(step == 0)
    def _():
        for s in range(CACHE_RING - 1):
            for c in copies(s):
                c.start()

    @pl.when(step + (CACHE_RING - 1) < pl.num_programs(0))
    def _():
        for c in copies(step + (CACHE_RING - 1)):
            c.start()

    for c in copies(step):
        c.wait()
    k_ref = k_buf.at[step % CACHE_RING]
    v_ref = v_buf.at[step % CACHE_RING]

    nc = HEAD_LANE_BLOCKS
    mc = N_MEM * nc
    lane_c = lax.broadcasted_iota(jnp.int32, (8, mc), 1) & (nc - 1)
    owns = []
    for b in range(bb):
        q8 = q_ref[b]
        for h in range(N_XHEADS):
            xk = k_ref[b, pl.ds(h, mc, stride=N_XHEADS), :].astype(BF16)
            qc = jnp.concatenate([q8[:, h * XHEAD_DIM + c * LANE:h * XHEAD_DIM + (c + 1) * LANE]
                                  for c in range(nc)], axis=0).astype(BF16)
            part = lax.dot_general(qc, xk, _NT, preferred_element_type=F32)
            own = part[8 * (nc - 1):]
            for c in range(nc - 2, -1, -1):
                own = jnp.where(lane_c == c, part[8 * c:8 * (c + 1)], own)
            owns.append(own)
    own_all = jnp.concatenate(owns, axis=0)
    lane_all = lax.broadcasted_iota(jnp.int32, own_all.shape, 1) & (nc - 1)
    pair = own_all + pltpu.roll(own_all, 1, 1)
    full = pair + pltpu.roll(pair, 2, 1)
    s = jnp.where(lane_all == nc - 1, full * (XHEAD_DIM ** -0.5), -1e30)
    e = jnp.exp(s - jnp.max(s, axis=1, keepdims=True))
    p = e / jnp.sum(e, axis=1, keepdims=True)
    p2 = p + pltpu.roll(p, mc - 1, 1)
    p4 = p2 + pltpu.roll(p2, mc - 2, 1)
    for b in range(bb):
        xv = jnp.concatenate([v_ref[b, pl.ds(h, mc, stride=N_XHEADS), :] for h in range(N_XHEADS)],
                             axis=1).astype(BF16)
        pm_rows = []
        for h in range(N_XHEADS):
            r0 = 8 * (b * N_XHEADS + h)
            pm_rows += [jnp.where(lane_c == c, p4[r0:r0 + 8], 0.0) for c in range(nc)]
        o_all = _dot(jnp.concatenate(pm_rows, axis=0).astype(BF16), xv)
        for h in range(N_XHEADS):
            for c in range(nc):
                g = h * nc + c
                o_ref[b, :, h * XHEAD_DIM + c * LANE:h * XHEAD_DIM + (c + 1) * LANE] = (
                    o_all[8 * g:8 * (g + 1), h * LANE:(h + 1) * LANE])


def _attn_cache(q8, k_rows, v_rows, bb):
    nseq = q8.shape[0]
    rows = N_MEM * CACHE_ROW_PITCH
    assert nseq // bb >= CACHE_RING
    return pl.pallas_call(
        functools.partial(_attn_cache_kernel, bb=bb),
        grid=(nseq // bb,),
        in_specs=[pl.BlockSpec((bb, 8, D_MODEL), lambda i: (i, 0, 0)),
                  pl.BlockSpec(memory_space=pl.ANY),
                  pl.BlockSpec(memory_space=pl.ANY)],
        out_specs=pl.BlockSpec((bb, 8, D_MODEL), lambda i: (i, 0, 0)),
        out_shape=jax.ShapeDtypeStruct((nseq, 8, D_MODEL), F32),
        scratch_shapes=[pltpu.VMEM((CACHE_RING, bb, rows, LANE), F32),
                        pltpu.VMEM((CACHE_RING, bb, rows, LANE), F32),
                        pltpu.SemaphoreType.DMA((2, CACHE_RING))],
        compiler_params=_params(("arbitrary",), 48),
        name="attn_cache",
    )(q8, k_rows, v_rows)


def _oproj_kernel(o_ref, x1_ref, wo_ref, gf_ref, x2_ref, h_ref, *wb_ref):
    w = wo_ref[...].astype(BF16)
    for ref in wb_ref:
        ref[...] = w
    x2 = x1_ref[...] + _dot(o_ref[...], w)
    x2_ref[...] = x2
    h_ref[...] = _rms(x2, gf_ref[...]).astype(BF16)


def _oproj(o, x1, w_o, g_ffn, tm, emit_bf16=False):
    rows = x1.shape[0]
    row_spec = pl.BlockSpec((tm, D_MODEL), lambda i: (i, 0))
    out_specs = [row_spec, row_spec]
    out_shape = [jax.ShapeDtypeStruct((rows, D_MODEL), F32), jax.ShapeDtypeStruct((rows, D_MODEL), BF16)]
    if emit_bf16:
        out_specs.append(_const_spec((D_MODEL, D_MODEL)))
        out_shape.append(jax.ShapeDtypeStruct((D_MODEL, D_MODEL), BF16))
    return pl.pallas_call(
        _oproj_kernel,
        grid=(rows // tm,),
        in_specs=[row_spec, row_spec, _const_spec((D_MODEL, D_MODEL)), _const_spec((1, D_MODEL))],
        out_specs=out_specs,
        out_shape=out_shape,
        compiler_params=_params(("arbitrary",), 56),
        name="oproj",
    )(o, x1, w_o, g_ffn)


def _ffn_kernel(x_hbm, h_ref, wg_ref, wu_ref, wd_ref, gl_ref, y_ref, *rest):
    *bf16_out, x_buf, x_sem = rest
    f = pl.program_id(1)
    last = pl.num_programs(1) - 1
    tm = h_ref.shape[0]

    def x_copy():
        rows = pl.ds(pl.multiple_of(pl.program_id(0) * tm, tm), tm)
        return pltpu.make_async_copy(x_hbm.at[rows, :], x_buf, x_sem)

    def down():
        wg, wu, wd = (w[...].astype(BF16) for w in (wg_ref, wu_ref, wd_ref))
        for ref, w in zip(bf16_out, (wg, wu, wd)):
            ref[...] = w
        h = h_ref[...]
        z = (jax.nn.silu(_dot(h, wg)) * _dot(h, wu)).astype(BF16)
        return _dot(z, wd)

    @pl.when(f == 0)
    def _():
        x_copy().start()
        y_ref[...] = down()

    @pl.when((f > 0) & (f < last))
    def _():
        y_ref[...] += down()

    @pl.when(f == last)
    def _():
        x_copy().wait()
        y_ref[...] = _rms(x_buf[...] + (y_ref[...] + down()), gl_ref[...])


def _ffn(x2, h, w_gate, w_up, w_down, g_final, tm, tf, emit_bf16=False):
    rows = x2.shape[0]
    nf = D_FF // tf
    assert nf >= 2, "the residual copy is started on the first d_ff tile and waited on the last"
    row_spec = pl.BlockSpec((tm, D_MODEL), lambda i, f: (i, 0))
    out_specs = [row_spec]
    out_shape = [jax.ShapeDtypeStruct((rows, D_MODEL), F32)]
    if emit_bf16:
        once = lambda i, f: jnp.where(i == 0, f, nf)
        out_specs += [pl.BlockSpec((D_MODEL, tf), lambda i, f: (0, once(i, f))),
                      pl.BlockSpec((D_MODEL, tf), lambda i, f: (0, once(i, f))),
                      pl.BlockSpec((tf, D_MODEL), lambda i, f: (once(i, f), 0))]
        out_shape += [jax.ShapeDtypeStruct((D_MODEL, D_FF + tf), BF16),
                      jax.ShapeDtypeStruct((D_MODEL, D_FF + tf), BF16),
                      jax.ShapeDtypeStruct((D_FF + tf, D_MODEL), BF16)]
    return pl.pallas_call(
        _ffn_kernel,
        grid=(rows // tm, nf),
        in_specs=[pl.BlockSpec(memory_space=pl.ANY), row_spec,
                  pl.BlockSpec((D_MODEL, tf), lambda i, f: (0, f)),
                  pl.BlockSpec((D_MODEL, tf), lambda i, f: (0, f)),
                  pl.BlockSpec((tf, D_MODEL), lambda i, f: (f, 0)),
                  _const_spec((1, D_MODEL))],
        out_specs=out_specs,
        out_shape=out_shape,
        scratch_shapes=[pltpu.VMEM((tm, D_MODEL), F32), pltpu.SemaphoreType.DMA(())],
        compiler_params=_params(("arbitrary", "arbitrary"), 60 if emit_bf16 else 56),
        name="ffn",
    )(x2, h, w_gate, w_up, w_down, g_final)


def _ssm_param_layouts(lam_re, lam_im, log_step, b_re, b_im, c_re, c_im):
    G, P = lam_re.shape
    ls = jnp.broadcast_to(log_step[:, None], (G, P))
    rep = lambda a: jnp.repeat(a, SSM_GROUP, axis=0)
    b_rows = lambda b: jnp.swapaxes(b, 1, 2).reshape(G * SSM_GROUP, P)
    row5 = jnp.stack([rep(lam_re), rep(lam_im), rep(ls), b_rows(b_re), b_rows(b_im)])
    col5 = jnp.stack([rep(lam_re).T, rep(lam_im).T, rep(ls).T,
                      c_re.reshape(G * SSM_GROUP, P).T, c_im.reshape(G * SSM_GROUP, P).T])
    flat = lambda a: a.reshape(N_LANE_BLOCKS, 1, STATE_BLOCK)
    flat3 = jnp.stack([flat(lam_re), flat(lam_im), flat(ls)])
    return row5, col5, flat3


def _states_to_blocks(h_re, h_im):
    S = h_re.shape[0]
    blk = lambda h: h.reshape(S, N_LANE_BLOCKS, STATE_BLOCK).transpose(1, 0, 2)
    return jnp.concatenate([blk(h_re), blk(h_im)], axis=-1)


def _blocks_to_states(st):
    S = st.shape[1]
    unblk = lambda a: a.transpose(1, 0, 2).reshape(1, S, N_SSM_GROUPS, SSM_STATE)
    return unblk(st[:, :, :STATE_BLOCK]), unblk(st[:, :, STATE_BLOCK:])


def kernel(x_prompt, x_sample, mem_prompt, state_pool_buf, state_ssm_re, state_ssm_im, cache_mem_k, cache_mem_v, g_mix, w_in, w_pool, pool_scale, ssm_lam_re, ssm_lam_im, ssm_log_step, ssm_b_re, ssm_b_im, ssm_c_re, ssm_c_im, ssm_d, w_glu, b_glu, w_out, g_cross, g_mem, w_q, w_k, w_v, w_o, g_ffn, w_gate, w_up, w_down, g_final):
    assert g_mix.shape[0] == 1, "single-layer step"
    B, T, _ = x_prompt.shape
    S, Ts, _ = x_sample.shape
    bf = lambda w: w.astype(BF16)
    vec = lambda v: v.reshape(1, -1)

    w_pool_b = bf(w_pool[0])

    row5, col5, flat3 = _ssm_param_layouts(ssm_lam_re[0], ssm_lam_im[0], ssm_log_step[0],
                                           ssm_b_re[0], ssm_b_im[0], ssm_c_re[0], ssm_c_im[0])
    d_blocks = ssm_d[0].reshape(N_LANE_BLOCKS, 1, LANE)

    gains = dict(g_mix=vec(g_mix[0]), scale=vec(pool_scale[0]), b_glu=vec(b_glu[0]), g_cross=vec(g_cross[0]),
                 g_ffn=vec(g_ffn[0]), g_final=vec(g_final))

    chunk_ops = _ssm_prep(row5, col5, flat3, sorted({PROMPT_CHUNK, Ts}, reverse=True))

    def ssm(u_view, L, h0_blocks, **tiles):
        g_act, st, *cast = _ssm_mix(u_view, chunk_ops[L], jnp.tile(d_blocks, (1, 1, L)), h0_blocks, L=L, **tiles)
        rows = u_view.shape[1] * u_view.shape[2]
        return g_act.reshape(N_LANE_BLOCKS, rows, LANE), _blocks_to_states(st), cast

    def finish(o, x1, w_o_any, ffn_weights, ffn_tiles, emit_bf16=False):
        x2, hf, *w_o_bf16 = _oproj(o, x1, w_o_any, gains["g_ffn"], ROW_TILE, emit_bf16=emit_bf16)
        y, *ffn_bf16 = _ffn(x2, hf, *ffn_weights, gains["g_final"], *ffn_tiles, emit_bf16=emit_bf16)
        return y, w_o_bf16, ffn_bf16

    mem = mem_prompt.reshape(B * N_MEM, D_MODEL)
    k_rows, k_b = _mem_kv(mem, vec(g_mem[0]), w_k[0], ROW_TILE)
    v_rows, v_b = _mem_kv(mem, vec(g_mem[0]), w_v[0], ROW_TILE)
    xp = x_prompt.reshape(B * T, D_MODEL)
    us_p, po_p, tail_p = _in_proj_pool(xp, gains["g_mix"], w_in[0], w_pool_b, gains["scale"],
                                       tc=ROW_TILE, rows_per_seq=T)
    h0_p = jnp.zeros((N_LANE_BLOCKS, B, 2 * STATE_BLOCK), F32)
    g_p, (re_p, im_p), (w_glu_b, w_out_b, w_q_b) = ssm(us_p.reshape(N_LANE_BLOCKS, B, T, LANE), PROMPT_CHUNK, h0_p,
                                                       Rn=PROMPT_CHUNK_ROWS, seq=True,
                                                       cast=(w_glu[0], w_out[0], w_q[0]))
    x1_p, o_p = _mix_out(g_p, po_p, xp, w_glu_b, gains["b_glu"], w_out_b, gains["g_cross"], w_q_b, ROW_TILE,
                         mem_kv=(k_b, v_b), rows_per_seq=T)
    yp, (w_o_b,), ffn_bf16 = finish(o_p, x1_p, w_o[0], (w_gate[0], w_up[0], w_down[0]), PROMPT_FFN_TILES,
                                    emit_bf16=True)
    pb_p = tail_p[:, POOL_HIST - POOL_BUF:]

    assert PAST_LEN >= POOL_BUF
    xs = x_sample.reshape(S * Ts, D_MODEL)
    u_s = _in_proj(xs, gains["g_mix"], w_in[0], ROW_TILE)
    po_s, buf_tm = _pool_mix_buf(u_s, state_pool_buf[0].transpose(1, 0, 2), w_pool_b, gains["scale"], L=Ts)
    pb_s = buf_tm.transpose(1, 0, 2)
    h0_s = _states_to_blocks(state_ssm_re[0], state_ssm_im[0])
    g_s, (re_s, im_s), _ = ssm(u_s.reshape(MIX_LANE_BLOCKS, 1, S * Ts, LANE), Ts, h0_s, Rn=S, seq=False)
    x1_s, q_s = _mix_out(g_s, po_s, xs, w_glu_b, gains["b_glu"], w_out_b, gains["g_cross"], w_q_b, ROW_TILE)
    q8 = jnp.pad(q_s.reshape(S, Ts, D_MODEL).astype(F32), ((0, 0), (0, 8 - Ts), (0, 0)))
    o8 = _attn_cache(q8, _cache_rows_view(cache_mem_k[0]), _cache_rows_view(cache_mem_v[0]), CACHE_SEQS_PER_STEP)
    ys, _, _ = finish(o8[:, :Ts].reshape(S * Ts, D_MODEL).astype(BF16), x1_s, w_o_b, ffn_bf16, SAMPLE_FFN_TILES)

    mk = _cache_rows_unview(k_rows, B)[None]
    mv = _cache_rows_unview(v_rows, B)[None]
    return (yp.reshape(B, T, D_MODEL), ys.reshape(S, Ts, D_MODEL), pb_p[None], re_p, im_p, mk, mv,
            pb_s[None], re_s, im_s)
```
